```python
import jax, jax.numpy as jnp
from jax import lax
import numpy as np

D_MODEL = 1024
BATCH = 2
SEQ = 8192
DEPTH = 2

N_MIXERS = 2
N_RWKV_LAYERS = (DEPTH + 1) // 2
N_ATTN_LAYERS = DEPTH // 2

RWKV_HEAD_SIZE = 64
RWKV_HEADS = D_MODEL // RWKV_HEAD_SIZE
DECAY_LORA = max(32, int(round(1.8 * D_MODEL ** 0.5 / 32)) * 32)
AAA_LORA = max(32, int(round(1.8 * D_MODEL ** 0.5 / 32)) * 32)
GATE_LORA = max(32, int(round(0.6 * D_MODEL ** 0.8 / 32)) * 32)
N_SHIFT_MIX = 6
GN_EPS = 64e-5

HEAD_DIM = 64
N_HEADS = D_MODEL // HEAD_DIM
N_KV_HEADS = max(1, N_HEADS // 8)
GROUP = N_HEADS // N_KV_HEADS
WINDOW = 128
BLOCK = WINDOW
ROPE_THETA = 10000.0
QKV_DIM = (N_HEADS + 2 * N_KV_HEADS) * HEAD_DIM

D_FF = 4 * D_MODEL
RMS_EPS = 1e-5

kernel_name = "hybrid_rwkv7_swa_sink_sqrelu"

F32 = jnp.float32


def rms_norm(x, g):
    xf = x.astype(F32)
    y = xf * lax.rsqrt(jnp.mean(xf * xf, axis=-1, keepdims=True) + RMS_EPS)
    return (y * g.astype(F32)).astype(x.dtype)


def wkv7_scan(r, w, k, v, a, b):
    bsz, _, nh, n = r.shape
    xs = tuple(jnp.moveaxis(t, 1, 0) for t in (r, w, k, v, a, b))

    def step(state, inp):
        r_t, w_t, k_t, v_t, a_t, b_t = inp
        sa = jnp.einsum('bhvk,bhk->bhv', state, a_t)
        state = (state * w_t[:, :, None, :]
                 + sa[..., None] * b_t[:, :, None, :]
                 + v_t[..., None] * k_t[:, :, None, :])
        y_t = jnp.einsum('bhvk,bhk->bhv', state, r_t)
        return state, y_t

    s0 = jnp.zeros((bsz, nh, n, n), F32)
    _, y = lax.scan(step, s0, xs)
    return jnp.moveaxis(y, 0, 1)


def rwkv7_time_mix(h, mu, w_r, w_k, w_v, w_o, w0, w1, w2, a0, a1, a2, g1, g2,
                   k_k, k_a, r_k, ln_w, ln_b):
    b, s, d = h.shape
    h_prev = jnp.pad(h[:, :-1], ((0, 0), (1, 0), (0, 0)))
    dx = h_prev - h
    xr = h + dx * mu[0]
    xw = h + dx * mu[1]
    xk = h + dx * mu[2]
    xv = h + dx * mu[3]
    xa = h + dx * mu[4]
    xg = h + dx * mu[5]

    r = xr @ w_r
    k = xk @ w_k
    v = xv @ w_v
    w_log = -jax.nn.softplus(-(w0 + jnp.tanh(xw @ w1) @ w2)) - 0.5
    decay = jnp.exp(-jnp.exp(w_log.astype(F32)))
    a = jax.nn.sigmoid(a0 + (xa @ a1) @ a2)
    g = jax.nn.sigmoid(xg @ g1) @ g2

    def heads(t):
        return t.reshape(b, s, RWKV_HEADS, RWKV_HEAD_SIZE).astype(F32)

    kk = heads(k * k_k)
    kk = kk * lax.rsqrt(jnp.maximum(jnp.sum(kk * kk, axis=-1, keepdims=True), 1e-24))
    k = k * (1 + (a - 1) * k_a)
    rh, kh, vh, ah, wh = heads(r), heads(k), heads(v), heads(a), heads(decay)

    y = wkv7_scan(rh, wh, kh, vh, -kk, kk * ah)
    mean = jnp.mean(y, axis=-1, keepdims=True)
    var = jnp.mean(jnp.square(y - mean), axis=-1, keepdims=True)
    yn = ((y - mean) * lax.rsqrt(var + GN_EPS)).reshape(b, s, d)
    yn = yn * ln_w.astype(F32) + ln_b.astype(F32)
    bonus = (jnp.sum(rh * kh * r_k.astype(F32), axis=-1, keepdims=True) * vh).reshape(b, s, d)
    return ((yn + bonus).astype(h.dtype) * g) @ w_o


def rope(x, cos, sin):
    half = x.shape[-1] // 2
    shape = cos.shape[:2] + (1,) * (x.ndim - 3) + (half,)
    c = cos.reshape(shape).astype(x.dtype)
    sn = sin.reshape(shape).astype(x.dtype)
    x1, x2 = x[..., :half], x[..., half:]
    return jnp.concatenate([x1 * c - x2 * sn, x2 * c + x1 * sn], axis=-1)


def sliding_window_sink_attention(q, k, v, sinks):
    b, s = q.shape[:2]
    nb = s // BLOCK
    qb = q.reshape(b, nb, BLOCK, N_KV_HEADS, GROUP, HEAD_DIM)
    kb = k.reshape(b, nb, BLOCK, N_KV_HEADS, HEAD_DIM)
    vb = v.reshape(b, nb, BLOCK, N_KV_HEADS, HEAD_DIM)
    zero = jnp.zeros_like(kb[:, :1])
    k2 = jnp.concatenate([jnp.concatenate([zero, kb[:, :-1]], axis=1), kb], axis=2)
    v2 = jnp.concatenate([jnp.concatenate([zero, vb[:, :-1]], axis=1), vb], axis=2)

    scale = HEAD_DIM ** -0.5
    scores = jnp.einsum('bnqhgd,bnkhd->bnhgqk', qb, k2, preferred_element_type=F32) * scale
    q_pos = jnp.arange(BLOCK)[:, None] + BLOCK
    k_pos = jnp.arange(2 * BLOCK)[None, :]
    rel = q_pos - k_pos
    band = (rel >= 0) & (rel < WINDOW)
    has_prev = (jnp.arange(nb) > 0)[:, None, None]
    valid = band[None] & (has_prev | (k_pos >= BLOCK)[None])
    scores = jnp.where(valid[None, :, None, None], scores, -jnp.inf)

    sink = sinks.astype(F32).reshape(N_KV_HEADS, GROUP)[None, None, :, :, None, None]
    m = jnp.maximum(jnp.max(scores, axis=-1, keepdims=True), sink)
    p = jnp.exp(scores - m)
    probs = p / (jnp.sum(p, axis=-1, keepdims=True) + jnp.exp(sink - m))
    out = jnp.einsum('bnhgqk,bnkhd->bnqhgd', probs.astype(v.dtype), v2)
    return out.reshape(b, s, N_HEADS * HEAD_DIM)


def swa_attention(h, cos, sin, w_qkv, b_qkv, sinks, w_o, b_o):
    b, s, _ = h.shape
    qkv = h @ w_qkv + b_qkv
    nq, nkv = N_HEADS * HEAD_DIM, N_KV_HEADS * HEAD_DIM
    q = qkv[..., :nq].reshape(b, s, N_KV_HEADS, GROUP, HEAD_DIM)
    k = qkv[..., nq:nq + nkv].reshape(b, s, N_KV_HEADS, HEAD_DIM)
    v = qkv[..., nq + nkv:].reshape(b, s, N_KV_HEADS, HEAD_DIM)
    q = rope(q, cos, sin)
    k = rope(k, cos, sin)
    o = sliding_window_sink_attention(q, k, v, sinks)
    return o @ w_o + b_o


def sqrelu_mlp(h, w_in, w_out):
    return jnp.square(jax.nn.relu(h @ w_in)) @ w_out


def setup_inputs(seed: int = 0) -> dict:
    key = jax.random.key(seed)
    ks = iter(jax.random.split(key, 40))
    nr, na, d = N_RWKV_LAYERS, N_ATTN_LAYERS, D_MODEL

    def nrm(shape, scale):
        return jax.random.normal(next(ks), shape, F32) * scale

    x = jax.random.normal(next(ks), (BATCH, SEQ, d), F32)
    offsets = jax.random.randint(next(ks), (BATCH, 1), 0, 1024, dtype=jnp.int32)
    positions = offsets + jnp.arange(SEQ, dtype=jnp.int32)[None, :]

    return {
        "x": x,
        "positions": positions,
        "norm_mix_g": 1.0 + nrm((DEPTH, d), 0.05),
        "norm_mlp_g": 1.0 + nrm((DEPTH, d), 0.05),
        "norm_final_g": 1.0 + nrm((d,), 0.05),
        "rwkv_mu": jax.random.uniform(next(ks), (nr, N_SHIFT_MIX, d), F32),
        "rwkv_w_r": nrm((nr, d, d), d ** -0.5),
        "rwkv_w_k": nrm((nr, d, d), d ** -0.5),
        "rwkv_w_v": nrm((nr, d, d), d ** -0.5),
        "rwkv_w_o": nrm((nr, d, d), d ** -0.5),
        "rwkv_w0": jax.random.uniform(next(ks), (nr, d), F32, -6.5, -1.5),
        "rwkv_w1": nrm((nr, d, DECAY_LORA), d ** -0.5),
        "rwkv_w2": nrm((nr, DECAY_LORA, d), 0.3 * DECAY_LORA ** -0.5),
        "rwkv_a0": nrm((nr, d), 0.1),
        "rwkv_a1": nrm((nr, d, AAA_LORA), d ** -0.5),
        "rwkv_a2": nrm((nr, AAA_LORA, d), 0.3 * AAA_LORA ** -0.5),
        "rwkv_g1": nrm((nr, d, GATE_LORA), d ** -0.5),
        "rwkv_g2": nrm((nr, GATE_LORA, d), GATE_LORA ** -0.5),
        "rwkv_k_k": 0.85 + nrm((nr, d), 0.05),
        "rwkv_k_a": 1.0 + nrm((nr, d), 0.05),
        "rwkv_r_k": -0.04 + nrm((nr, RWKV_HEADS, RWKV_HEAD_SIZE), 0.1),
        "rwkv_ln_w": 1.0 + nrm((nr, d), 0.05),
        "rwkv_ln_b": nrm((nr, d), 0.02),
        "attn_w_qkv": nrm((na, d, QKV_DIM), d ** -0.5),
        "attn_b_qkv": nrm((na, QKV_DIM), 0.02),
        "attn_sinks": nrm((na, N_HEADS), 0.5),
        "attn_w_o": nrm((na, N_HEADS * HEAD_DIM, d), (N_HEADS * HEAD_DIM) ** -0.5),
        "attn_b_o": nrm((na, d), 0.02),
        "mlp_w_in": nrm((DEPTH, d, D_FF), d ** -0.5),
        "mlp_w_out": nrm((DEPTH, D_FF, d), D_FF ** -0.5),
    }


def reference(x, positions, norm_mix_g, norm_mlp_g, norm_final_g,
              rwkv_mu, rwkv_w_r, rwkv_w_k, rwkv_w_v, rwkv_w_o, rwkv_w0, rwkv_w1, rwkv_w2,
              rwkv_a0, rwkv_a1, rwkv_a2, rwkv_g1, rwkv_g2, rwkv_k_k, rwkv_k_a, rwkv_r_k,
              rwkv_ln_w, rwkv_ln_b,
              attn_w_qkv, attn_b_qkv, attn_sinks, attn_w_o, attn_b_o,
              mlp_w_in, mlp_w_out):
    inv_freq = ROPE_THETA ** (-jnp.arange(0, HEAD_DIM, 2, dtype=F32) / HEAD_DIM)
    angles = positions.astype(F32)[..., None] * inv_freq
    cos, sin = jnp.cos(angles), jnp.sin(angles)

    for i in range(DEPTH):
        h = rms_norm(x, norm_mix_g[i])
        j = i // N_MIXERS
        if i % N_MIXERS == 0:
            mix = rwkv7_time_mix(h, rwkv_mu[j], rwkv_w_r[j], rwkv_w_k[j], rwkv_w_v[j], rwkv_w_o[j],
                                 rwkv_w0[j], rwkv_w1[j], rwkv_w2[j], rwkv_a0[j], rwkv_a1[j], rwkv_a2[j],
                                 rwkv_g1[j], rwkv_g2[j], rwkv_k_k[j], rwkv_k_a[j], rwkv_r_k[j],
                                 rwkv_ln_w[j], rwkv_ln_b[j])
        else:
            mix = swa_attention(h, cos, sin, attn_w_qkv[j], attn_b_qkv[j], attn_sinks[j],
                                attn_w_o[j], attn_b_o[j])
        x = x + mix
        x = x + sqrelu_mlp(rms_norm(x, norm_mlp_g[i]), mlp_w_in[i], mlp_w_out[i])
    return rms_norm(x, norm_final_g)
```

```python
import functools

import jax
import jax.numpy as jnp
from jax import lax
from jax.experimental import pallas as pl
from jax.experimental.pallas import tpu as pltpu

F32 = jnp.float32
BF16 = jnp.bfloat16

D_MODEL = 1024
HEAD = 64
N_HEADS = D_MODEL // HEAD
N_KV_HEADS = 2
GROUP = N_HEADS // N_KV_HEADS
WINDOW = 128
QKV_DIM = (N_HEADS + 2 * N_KV_HEADS) * HEAD
D_FF = 4 * D_MODEL
ROPE_THETA = 10000.0
RMS_EPS = 1e-5
GN_EPS = 64e-5

LANES = 128
CHUNK = 64
VMEM_LIMIT = 56 * 1024 * 1024


def _rms(x, g):
    return x * lax.rsqrt(jnp.mean(x * x, axis=-1, keepdims=True) + RMS_EPS) * g


def _dot(a, b):
    return jnp.dot(a.astype(BF16), b.astype(BF16), preferred_element_type=F32)


def _dot_nt(a, b):
    return lax.dot_general(a.astype(BF16), b.astype(BF16), (((1,), (1,)), ((), ())),
                           preferred_element_type=F32)


def _dot_tn(a, b):
    return lax.dot_general(a.astype(BF16), b.astype(BF16), (((0,), (0,)), ((), ())),
                           preferred_element_type=F32)


def _const_spec(shape):
    nd = len(shape)
    return pl.BlockSpec(shape, lambda *_: (0,) * nd)


def _rwkv_pre_kernel(seq_tiles, x_ref, xp_ref, g_ref, mu_ref, wr_ref, wk_ref, wv_ref,
                     w0_ref, w1_ref, w2_ref, a0_ref, a1_ref, a2_ref, g1_ref, g2_ref,
                     r_out, k_out, v_out, lw_out, a_out, g_out):
    i = pl.program_id(0)
    g = g_ref[...]
    h = _rms(x_ref[...], g)
    hp = _rms(xp_ref[...], g)
    last = jnp.where(i % seq_tiles == 0, 0.0, hp[7:8, :])
    row = lax.broadcasted_iota(jnp.int32, h.shape, 0)
    h_prev = jnp.where(row == 0, last, pltpu.roll(h, 1, axis=0))
    dx = h_prev - h
    mu = mu_ref[...]
    xr = h + dx * mu[0:1]
    xw = h + dx * mu[1:2]
    xk = h + dx * mu[2:3]
    xv = h + dx * mu[3:4]
    xa = h + dx * mu[4:5]
    xg = h + dx * mu[5:6]
    r_out[...] = _dot(xr, wr_ref[...])
    k_out[...] = _dot(xk, wk_ref[...])
    v_out[...] = _dot(xv, wv_ref[...])
    w_pre = w0_ref[...] + _dot(jnp.tanh(_dot(xw, w1_ref[...])), w2_ref[...])
    w_log = -jax.nn.softplus(-w_pre) - 0.5
    lw_out[...] = -jnp.exp(w_log)
    a_out[...] = jax.nn.sigmoid(a0_ref[...] + _dot(_dot(xa, a1_ref[...]), a2_ref[...]))
    g_out[...] = _dot(jax.nn.sigmoid(_dot(xg, g1_ref[...])), g2_ref[...])


def _rwkv_pre(x2d, seq_len, g, mu, wr, wk, wv, w0, w1, w2, a0, a1, a2, g1, g2, tm=256):
    t, d = x2d.shape
    seq_tiles = seq_len // tm
    row_spec = pl.BlockSpec((tm, d), lambda i: (i, 0))
    prev_spec = pl.BlockSpec((8, d), lambda i: (jnp.maximum(i * (tm // 8) - 1, 0), 0))
    consts = (g, mu, wr, wk, wv, w0, w1, w2, a0, a1, a2, g1, g2)
    out = jax.ShapeDtypeStruct((t, d), F32)
    return pl.pallas_call(
        functools.partial(_rwkv_pre_kernel, seq_tiles),
        out_shape=(out,) * 6,
        grid=(t // tm,),
        in_specs=[row_spec, prev_spec] + [_const_spec(c.shape) for c in consts],
        out_specs=(row_spec,) * 6,
        compiler_params=pltpu.CompilerParams(
            dimension_semantics=("parallel",), vmem_limit_bytes=VMEM_LIMIT),
        name="rwkv_pre",
    )(x2d, x2d, *consts)


def _cumsum_rows(x):
    n = x.shape[0]
    row = lax.broadcasted_iota(jnp.int32, x.shape, 0)
    s = 1
    while s < n:
        x = x + jnp.where(row >= s, pltpu.roll(x, s, axis=0), 0.0)
        s *= 2
    return x


INV_BASE = 8


def _inverse_masks(n):
    ri = lax.broadcasted_iota(jnp.int32, (n, n), 0)
    ci = lax.broadcasted_iota(jnp.int32, (n, n), 1)
    blk = lambda idx, size: lax.shift_right_logical(idx, size.bit_length() - 1)
    eye = (ri == ci).astype(F32)
    diag = blk(ri, INV_BASE) == blk(ci, INV_BASE)
    bands = []
    size = INV_BASE
    while size < n:
        inner = blk(ri, size) == blk(ci, size)
        outer = blk(ri, 2 * size) == blk(ci, 2 * size)
        bands.append(outer & jnp.logical_not(inner))
        size *= 2
    return eye, diag, bands


def _unit_lower_inverse(a_strict, masks):
    eye, diag, bands = masks
    d1 = jnp.where(diag, a_strict, 0.0)
    d2 = _dot(d1, d1)
    d4 = _dot(d2, d2)
    t = eye + d1
    t = t + _dot(t, d2)
    t = t + _dot(t, d4)
    for band in bands:
        t = t + _dot(_dot(t, jnp.where(band, a_strict, 0.0)), t)
    return t


def _wkv_kernel(heads, chunks, r_ref, k_ref, v_ref, lw_ref, a_ref,
                kk_ref, ka_ref, rk_ref, lnw_ref, lnb_ref, o_ref, state_ref):
    c_idx = pl.program_id(2)

    @pl.when(c_idx == 0)
    def _():
        state_ref[...] = jnp.zeros_like(state_ref)

    ri = lax.broadcasted_iota(jnp.int32, (CHUNK, 2 * CHUNK), 0)
    ci = lax.broadcasted_iota(jnp.int32, (CHUNK, 2 * CHUNK), 1)
    cj = ci % CHUNK
    strict = cj < ri
    incl = cj <= ri
    right = ci >= CHUNK
    inv_masks = _inverse_masks(CHUNK)

    def chunk_body(c, carry):
        rows = pl.ds(pl.multiple_of(c * CHUNK, CHUNK), CHUNK)
        lw_all = lw_ref[0, rows, :]
        linc_all = _cumsum_rows(lw_all)
        for h in range(heads):
            lanes = slice(h * HEAD, (h + 1) * HEAD)
            lw = lw_all[:, lanes]
            linc = linc_all[:, lanes]
            r = r_ref[0, rows, lanes]
            k = k_ref[0, rows, lanes]
            v = v_ref[0, rows, lanes]
            a = a_ref[0, rows, lanes]
            p_inc = jnp.exp(linc)
            p_exc = jnp.exp(linc - lw)
            p_inv = jnp.exp(-linc)
            kk = k * kk_ref[:, lanes]
            kk = kk * lax.rsqrt(jnp.maximum(jnp.sum(kk * kk, axis=-1, keepdims=True), 1e-24))
            kmod = k * (1.0 + (a - 1.0) * ka_ref[:, lanes])
            at = -kk * p_exc
            rt = r * p_inc
            bt = kk * a * p_inv
            kt = kmod * p_inv
            s0 = state_ref[h]

            lhs = jnp.concatenate([at, rt], axis=0)
            rhs = jnp.concatenate([bt, kt], axis=0)
            aa = _dot_nt(lhs, rhs)
            top = jnp.where(strict, aa[:CHUNK], 0.0)
            bot = jnp.where(incl, aa[CHUNK:], 0.0)
            t_inv = _unit_lower_inverse(top[:, :CHUNK], inv_masks)
            vv = jnp.concatenate([v, v], axis=0)
            z = _dot_nt(at, s0) + _dot(jnp.where(right, top, 0.0), vv)
            u = _dot(t_inv, z)
            uv = jnp.concatenate([u, v], axis=0)
            y = _dot_nt(rt, s0) + _dot(bot, uv)
            state_ref[h] = (s0 + _dot_tn(uv, rhs)) * p_inc[CHUNK - 1:CHUNK, :]

            mean = jnp.mean(y, axis=-1, keepdims=True)
            yc = y - mean
            var = jnp.mean(yc * yc, axis=-1, keepdims=True)
            yn = yc * lax.rsqrt(var + GN_EPS) * lnw_ref[:, lanes] + lnb_ref[:, lanes]
            bonus = jnp.sum(r * kmod * rk_ref[:, lanes], axis=-1, keepdims=True) * v
            o_ref[0, rows, lanes] = yn + bonus
        return carry

    lax.fori_loop(0, chunks, chunk_body, 0)


def _wkv(r, k, v, lw, a, kk_p, ka_p, rk_p, lnw_p, lnb_p, heads=8, rows=256):
    b, s, d = r.shape
    width = heads * HEAD
    seq_spec = pl.BlockSpec((1, rows, width), lambda bi, hi, ci: (bi, ci, hi))
    par_spec = pl.BlockSpec((1, width), lambda bi, hi, ci: (0, hi))
    return pl.pallas_call(
        functools.partial(_wkv_kernel, heads, rows // CHUNK),
        out_shape=jax.ShapeDtypeStruct((b, s, d), F32),
        grid=(b, d // width, s // rows),
        in_specs=[seq_spec] * 5 + [par_spec] * 5,
        out_specs=seq_spec,
        scratch_shapes=[pltpu.VMEM((heads, HEAD, HEAD), F32)],
        compiler_params=pltpu.CompilerParams(
            dimension_semantics=("parallel", "parallel", "arbitrary"),
            vmem_limit_bytes=VMEM_LIMIT),
        name="wkv",
    )(r, k, v, lw, a, kk_p, ka_p, rk_p, lnw_p, lnb_p)


def _proj_mlp_kernel(has_gate, final_norm, ff_chunk, *refs):
    refs = list(refs)
    y_ref = refs.pop(0)
    gate_ref = refs.pop(0) if has_gate else None
    x_ref, wo_ref, bo_ref, gm_ref, win_ref, wout_ref = refs[:6]
    refs = refs[6:]
    gf_ref = refs.pop(0) if final_norm else None
    o_ref = refs.pop(0)

    y = y_ref[...].astype(F32)
    if has_gate:
        y = y * gate_ref[...]
    x1 = x_ref[...] + _dot(y, wo_ref[...]) + bo_ref[...]
    hb = _rms(x1, gm_ref[...]).astype(BF16)
    acc = x1
    for c in range(D_FF // ff_chunk):
        cols = slice(c * ff_chunk, (c + 1) * ff_chunk)
        hid = jnp.maximum(jnp.dot(hb, win_ref[:, cols], preferred_element_type=F32), 0.0)
        acc = acc + _dot(hid * hid, wout_ref[cols, :])
    if final_norm:
        acc = _rms(acc, gf_ref[...])
    o_ref[...] = acc


def _proj_mlp(y, gate, x, wo, bo, gm, win, wout, gf, tm=256, ff_chunk=1024):
    t, d = x.shape
    row_spec = pl.BlockSpec((tm, d), lambda i: (i, 0))
    args = [y] + ([gate] if gate is not None else []) + [x, wo, bo, gm, win, wout]
    specs = [row_spec] * (len(args) - 5) + [_const_spec(c.shape) for c in args[-5:]]
    if gf is not None:
        args.append(gf)
        specs.append(_const_spec(gf.shape))
    return pl.pallas_call(
        functools.partial(_proj_mlp_kernel, gate is not None, gf is not None, ff_chunk),
        out_shape=jax.ShapeDtypeStruct((t, d), F32),
        grid=(t // tm,),
        in_specs=specs,
        out_specs=row_spec,
        compiler_params=pltpu.CompilerParams(
            dimension_semantics=("parallel",), vmem_limit_bytes=VMEM_LIMIT),
        name="proj_mlp",
    )(*args)


def _rope_tab_kernel(pos_ref, freq_ref, cos_ref, sin_ref):
    ang = pos_ref[...] * freq_ref[...]
    lane = lax.broadcasted_iota(jnp.int32, ang.shape, 1)
    cos_ref[...] = jnp.cos(ang)
    sin_ref[...] = jnp.where(lane % HEAD < HEAD // 2, -1.0, 1.0) * jnp.sin(ang)


def _rope_tab(pos_col, freq_row, tm=1024):
    t = pos_col.shape[0]
    out = jax.ShapeDtypeStruct((t, LANES), F32)
    spec = pl.BlockSpec((tm, LANES), lambda i: (i, 0))
    return pl.pallas_call(
        _rope_tab_kernel,
        out_shape=(out, out),
        grid=(t // tm,),
        in_specs=[pl.BlockSpec((tm, 1), lambda i: (i, 0)), _const_spec(freq_row.shape)],
        out_specs=(spec, spec),
        compiler_params=pltpu.CompilerParams(dimension_semantics=("parallel",)),
        name="rope_tab",
    )(pos_col, freq_row)


def _attn_qkv_kernel(x_ref, g_ref, w_ref, b_ref, cos_ref, sin_ref, o_ref):
    h = _rms(x_ref[...], g_ref[...])
    qkv = _dot(h, w_ref[...]) + b_ref[...]
    cos = cos_ref[...]
    sin = sin_ref[...]
    lane = lax.broadcasted_iota(jnp.int32, cos.shape, 1)
    first_half = lane % HEAD < HEAD // 2
    n_q = N_HEADS * HEAD // LANES
    n_rot = (N_HEADS + N_KV_HEADS) * HEAD // LANES
    for j in range(QKV_DIM // LANES):
        cols = slice(j * LANES, (j + 1) * LANES)
        blk = qkv[:, cols]
        if j < n_rot:
            rot = jnp.where(first_half, pltpu.roll(blk, LANES - HEAD // 2, axis=1),
                            pltpu.roll(blk, HEAD // 2, axis=1))
            blk = blk * cos + rot * sin
        if j < n_q:
            blk = blk * (HEAD ** -0.5)
        o_ref[:, cols] = blk.astype(o_ref.dtype)


def _attn_qkv(x, g, w, b, cos, sin, tm=256):
    t, d = x.shape
    row = lambda width: pl.BlockSpec((tm, width), lambda i: (i, 0))
    return pl.pallas_call(
        _attn_qkv_kernel,
        out_shape=jax.ShapeDtypeStruct((t, QKV_DIM), BF16),
        grid=(t // tm,),
        in_specs=[row(d), _const_spec(g.shape), _const_spec(w.shape), _const_spec(b.shape),
                  row(LANES), row(LANES)],
        out_specs=row(QKV_DIM),
        compiler_params=pltpu.CompilerParams(
            dimension_semantics=("parallel",), vmem_limit_bytes=VMEM_LIMIT),
        name="attn_qkv",
    )(x, g, w, b, cos, sin)


def _attn_core_kernel(q_ref, kp_ref, kc_ref, vp_ref, vc_ref, sink_ref, o_ref):
    n = pl.program_id(1)
    ri = lax.broadcasted_iota(jnp.int32, (WINDOW, WINDOW), 0)
    ci = lax.broadcasted_iota(jnp.int32, (WINDOW, WINDOW), 1)
    prev_ok = (ci > ri) & (n > 0)
    cur_ok = ci <= ri
    neg = -jnp.inf
    for h in range(N_HEADS):
        kvl = slice((h // GROUP) * HEAD, (h // GROUP + 1) * HEAD)
        q = q_ref[0, :, h * HEAD:(h + 1) * HEAD]
        s_p = jnp.where(prev_ok, _dot_nt(q, kp_ref[0, :, kvl]), neg)
        s_c = jnp.where(cur_ok, _dot_nt(q, kc_ref[0, :, kvl]), neg)
        sink = sink_ref[:, h:h + 1]
        m = jnp.maximum(jnp.maximum(jnp.max(s_p, axis=-1, keepdims=True),
                                    jnp.max(s_c, axis=-1, keepdims=True)), sink)
        p_p = jnp.exp(s_p - m)
        p_c = jnp.exp(s_c - m)
        den = (jnp.sum(p_p, axis=-1, keepdims=True) + jnp.sum(p_c, axis=-1, keepdims=True)
               + jnp.exp(sink - m))
        o = _dot(p_p, vp_ref[0, :, kvl]) + _dot(p_c, vc_ref[0, :, kvl])
        o_ref[0, :, h * HEAD:(h + 1) * HEAD] = (o / den).astype(o_ref.dtype)


def _attn_core(qkv, sinks):
    b, s, _ = qkv.shape
    nq = N_HEADS * HEAD
    k_blk = nq // LANES
    v_blk = k_blk + N_KV_HEADS * HEAD // LANES
    q_spec = pl.BlockSpec((1, WINDOW, nq), lambda bi, n: (bi, n, 0))
    cur = lambda blk: pl.BlockSpec((1, WINDOW, LANES), lambda bi, n: (bi, n, blk))
    prev = lambda blk: pl.BlockSpec((1, WINDOW, LANES),
                                    lambda bi, n: (bi, jnp.maximum(n - 1, 0), blk))
    return pl.pallas_call(
        _attn_core_kernel,
        out_shape=jax.ShapeDtypeStruct((b, s, nq), BF16),
        grid=(b, s // WINDOW),
        in_specs=[q_spec, prev(k_blk), cur(k_blk), prev(v_blk), cur(v_blk),
                  _const_spec(sinks.shape)],
        out_specs=q_spec,
        compiler_params=pltpu.CompilerParams(
            dimension_semantics=("parallel", "arbitrary"), vmem_limit_bytes=VMEM_LIMIT),
        name="attn_core",
    )(qkv, qkv, qkv, qkv, qkv, sinks)


def kernel(x, positions, norm_mix_g, norm_mlp_g, norm_final_g, rwkv_mu, rwkv_w_r, rwkv_w_k, rwkv_w_v, rwkv_w_o, rwkv_w0, rwkv_w1, rwkv_w2, rwkv_a0, rwkv_a1, rwkv_a2, rwkv_g1, rwkv_g2, rwkv_k_k, rwkv_k_a, rwkv_r_k, rwkv_ln_w, rwkv_ln_b, attn_w_qkv, attn_b_qkv, attn_sinks, attn_w_o, attn_b_o, mlp_w_in, mlp_w_out):
    b, s, d = x.shape
    t = b * s
    x2d = x.reshape(t, d)
    row = lambda p: p.reshape(1, -1)
    bf = lambda w: w.astype(BF16)
    zero_bias = jnp.zeros((1, d), F32)

    r, k, v, lw, a, gate = _rwkv_pre(
        x2d, s, row(norm_mix_g[0]), rwkv_mu[0], bf(rwkv_w_r[0]), bf(rwkv_w_k[0]),
        bf(rwkv_w_v[0]), row(rwkv_w0[0]), bf(rwkv_w1[0]), bf(rwkv_w2[0]), row(rwkv_a0[0]),
        bf(rwkv_a1[0]), bf(rwkv_a2[0]), bf(rwkv_g1[0]), bf(rwkv_g2[0]))
    seq = lambda z: z.reshape(b, s, d)
    y = _wkv(seq(r), seq(k), seq(v), seq(lw), seq(a), row(rwkv_k_k[0]), row(rwkv_k_a[0]),
             row(rwkv_r_k[0]), row(rwkv_ln_w[0]), row(rwkv_ln_b[0]))
    x2d = _proj_mlp(y.reshape(t, d), gate, x2d, bf(rwkv_w_o[0]), zero_bias,
                    row(norm_mlp_g[0]), bf(mlp_w_in[0]), bf(mlp_w_out[0]), None)

    inv_freq = ROPE_THETA ** (-jnp.arange(0, HEAD, 2, dtype=F32) / HEAD)
    freq_row = jnp.tile(inv_freq, LANES // (HEAD // 2)).reshape(1, LANES)
    cos, sin = _rope_tab(positions.astype(F32).reshape(t, 1), freq_row)
    qkv = _attn_qkv(x2d, row(norm_mix_g[1]), bf(attn_w_qkv[0]), row(attn_b_qkv[0]), cos, sin)
    o = _attn_core(qkv.reshape(b, s, QKV_DIM), row(attn_sinks[0]))
    out = _proj_mlp(o.reshape(t, d), None, x2d, bf(attn_w_o[0]), row(attn_b_o[0]),
                    row(norm_mlp_g[1]), bf(mlp_w_in[1]), bf(mlp_w_out[1]), row(norm_final_g))
    return out.reshape(b, s, d)
```

```python
import functools

import jax
import jax.numpy as jnp
from jax import lax
from jax.experimental import pallas as pl
from jax.experimental.pallas import tpu as pltpu

F32 = jnp.float32
BF16 = jnp.bfloat16

D_MODEL = 1024
HEAD = 64
N_HEADS = D_MODEL // HEAD
N_KV_HEADS = 2
GROUP = N_HEADS // N_KV_HEADS
WINDOW = 128
QKV_DIM = (N_HEADS + 2 * N_KV_HEADS) * HEAD
D_FF = 4 * D_MODEL
ROPE_THETA = 10000.0
RMS_EPS = 1e-5
GN_EPS = 64e-5

LANES = 128
CHUNK = 64
VMEM_LIMIT = 56 * 1024 * 1024


def _rms(x, g):
    return x * lax.rsqrt(jnp.mean(x * x, axis=-1, keepdims=True) + RMS_EPS) * g


def _dot(a, b):
    return jnp.dot(a.astype(BF16), b.astype(BF16), preferred_element_type=F32)


def _dot_nt(a, b):
    return lax.dot_general(a.astype(BF16), b.astype(BF16), (((1,), (1,)), ((), ())),
                           preferred_element_type=F32)


def _dot_tn(a, b):
    return lax.dot_general(a.astype(BF16), b.astype(BF16), (((0,), (0,)), ((), ())),
                           preferred_element_type=F32)


def _const_spec(shape):
    nd = len(shape)
    return pl.BlockSpec(shape, lambda *_: (0,) * nd)


def _rwkv_pre_kernel(seq_tiles, x_ref, xp_ref, g_ref, mu_ref, wr_ref, wk_ref, wv_ref,
                     w0_ref, w1_ref, w2_ref, a0_ref, a1_ref, a2_ref, g1_ref, g2_ref,
                     r_out, k_out, v_out, lw_out, a_out, g_out):
    i = pl.program_id(0)
    g = g_ref[...]
    h = _rms(x_ref[...], g)
    hp = _rms(xp_ref[...], g)
    last = jnp.where(i % seq_tiles == 0, 0.0, hp[7:8, :])
    row = lax.broadcasted_iota(jnp.int32, h.shape, 0)
    h_prev = jnp.where(row == 0, last, pltpu.roll(h, 1, axis=0))
    dx = h_prev - h
    mu = mu_ref[...]
    xr = h + dx * mu[0:1]
    xw = h + dx * mu[1:2]
    xk = h + dx * mu[2:3]
    xv = h + dx * mu[3:4]
    xa = h + dx * mu[4:5]
    xg = h + dx * mu[5:6]
    r_out[...] = _dot(xr, wr_ref[...])
    k_out[...] = _dot(xk, wk_ref[...])
    v_out[...] = _dot(xv, wv_ref[...])
    w_pre = w0_ref[...] + _dot(jnp.tanh(_dot(xw, w1_ref[...])), w2_ref[...])
    w_log = -jax.nn.softplus(-w_pre) - 0.5
    lw_out[...] = -jnp.exp(w_log)
    a_out[...] = jax.nn.sigmoid(a0_ref[...] + _dot(_dot(xa, a1_ref[...]), a2_ref[...]))
    g_out[...] = _dot(jax.nn.sigmoid(_dot(xg, g1_ref[...])), g2_ref[...])


def _rwkv_pre(x2d, seq_len, g, mu, wr, wk, wv, w0, w1, w2, a0, a1, a2, g1, g2, tm=256):
    t, d = x2d.shape
    seq_tiles = seq_len // tm
    row_spec = pl.BlockSpec((tm, d), lambda i: (i, 0))
    prev_spec = pl.BlockSpec((8, d), lambda i: (jnp.maximum(i * (tm // 8) - 1, 0), 0))
    consts = (g, mu, wr, wk, wv, w0, w1, w2, a0, a1, a2, g1, g2)
    out = jax.ShapeDtypeStruct((t, d), F32)
    return pl.pallas_call(
        functools.partial(_rwkv_pre_kernel, seq_tiles),
        out_shape=(out,) * 6,
        grid=(t // tm,),
        in_specs=[row_spec, prev_spec] + [_const_spec(c.shape) for c in consts],
        out_specs=(row_spec,) * 6,
        compiler_params=pltpu.CompilerParams(
            dimension_semantics=("parallel",), vmem_limit_bytes=VMEM_LIMIT),
        name="rwkv_pre",
    )(x2d, x2d, *consts)


def _cumsum_rows(x):
    n = x.shape[0]
    row = lax.broadcasted_iota(jnp.int32, x.shape, 0)
    s = 1
    while s < n:
        x = x + jnp.where(row >= s, pltpu.roll(x, s, axis=0), 0.0)
        s *= 2
    return x


INV_BASE = 8


def _inverse_masks(n):
    ri = lax.broadcasted_iota(jnp.int32, (n, n), 0)
    ci = lax.broadcasted_iota(jnp.int32, (n, n), 1)
    blk = lambda idx, size: lax.shift_right_logical(idx, size.bit_length() - 1)
    eye = (ri == ci).astype(F32)
    diag = blk(ri, INV_BASE) == blk(ci, INV_BASE)
    bands = []
    size = INV_BASE
    while size < n:
        inner = blk(ri, size) == blk(ci, size)
        outer = blk(ri, 2 * size) == blk(ci, 2 * size)
        bands.append(outer & jnp.logical_not(inner))
        size *= 2
    return eye, diag, bands


def _unit_lower_inverse(a_strict, masks):
    eye, diag, bands = masks
    d1 = jnp.where(diag, a_strict, 0.0)
    d2 = _dot(d1, d1)
    d4 = _dot(d2, d2)
    t = eye + d1
    t = t + _dot(t, d2)
    t = t + _dot(t, d4)
    for band in bands:
        t = t + _dot(_dot(t, jnp.where(band, a_strict, 0.0)), t)
    return t


def _wkv_kernel(heads, chunks, r_ref, k_ref, v_ref, lw_ref, a_ref,
                kk_ref, ka_ref, rk_ref, lnw_ref, lnb_ref, o_ref, state_ref):
    c_idx = pl.program_id(2)

    @pl.when(c_idx == 0)
    def _():
        state_ref[...] = jnp.zeros_like(state_ref)

    ri = lax.broadcasted_iota(jnp.int32, (CHUNK, 2 * CHUNK), 0)
    ci = lax.broadcasted_iota(jnp.int32, (CHUNK, 2 * CHUNK), 1)
    cj = ci % CHUNK
    strict = cj < ri
    incl = cj <= ri
    right = ci >= CHUNK
    inv_masks = _inverse_masks(CHUNK)

    def chunk_body(c, carry):
        rows = pl.ds(pl.multiple_of(c * CHUNK, CHUNK), CHUNK)
        lw_all = lw_ref[0, rows, :]
        linc_all = _cumsum_rows(lw_all)
        hs = range(heads)
        lanes = [slice(h * HEAD, (h + 1) * HEAD) for h in hs]
        lw = [lw_all[:, l] for l in lanes]
        linc = [linc_all[:, l] for l in lanes]
        r = [r_ref[0, rows, l] for l in lanes]
        k = [k_ref[0, rows, l] for l in lanes]
        v = [v_ref[0, rows, l] for l in lanes]
        a = [a_ref[0, rows, l] for l in lanes]
        p_inc = [jnp.exp(x) for x in linc]
        p_exc = [jnp.exp(x - y) for x, y in zip(linc, lw)]
        p_inv = [jnp.exp(-x) for x in linc]
        kk = [k[h] * kk_ref[:, lanes[h]] for h in hs]
        kk = [x * lax.rsqrt(jnp.maximum(jnp.sum(x * x, axis=-1, keepdims=True), 1e-24)) for x in kk]
        kmod = [k[h] * (1.0 + (a[h] - 1.0) * ka_ref[:, lanes[h]]) for h in hs]
        at = [-kk[h] * p_exc[h] for h in hs]
        rt = [r[h] * p_inc[h] for h in hs]
        bt = [kk[h] * a[h] * p_inv[h] for h in hs]
        kt = [kmod[h] * p_inv[h] for h in hs]
        s0 = [state_ref[h] for h in hs]
        lhs = [jnp.concatenate([at[h], rt[h]], axis=0) for h in hs]
        rhs = [jnp.concatenate([bt[h], kt[h]], axis=0) for h in hs]
        aa = [_dot_nt(lhs[h], rhs[h]) for h in hs]
        top = [jnp.where(strict, x[:CHUNK], 0.0) for x in aa]
        bot = [jnp.where(incl, x[CHUNK:], 0.0) for x in aa]
        eye, diag, bands = inv_masks
        a_s = [x[:, :CHUNK] for x in top]
        d1 = [jnp.where(diag, x, 0.0) for x in a_s]
        d2 = [_dot(x, x) for x in d1]
        d4 = [_dot(x, x) for x in d2]
        t = [eye + x for x in d1]
        t = [t[h] + _dot(t[h], d2[h]) for h in hs]
        t = [t[h] + _dot(t[h], d4[h]) for h in hs]
        for band in bands:
            te = [_dot(t[h], jnp.where(band, a_s[h], 0.0)) for h in hs]
            t = [t[h] + _dot(te[h], t[h]) for h in hs]
        vv = [jnp.concatenate([x, x], axis=0) for x in v]
        z = [_dot_nt(at[h], s0[h]) + _dot(jnp.where(right, top[h], 0.0), vv[h]) for h in hs]
        u = [_dot(t[h], z[h]) for h in hs]
        uv = [jnp.concatenate([u[h], v[h]], axis=0) for h in hs]
        y = [_dot_nt(rt[h], s0[h]) + _dot(bot[h], uv[h]) for h in hs]
        for h in hs:
            state_ref[h] = (s0[h] + _dot_tn(uv[h], rhs[h])) * p_inc[h][CHUNK - 1:CHUNK, :]
        for h in hs:
            mean = jnp.mean(y[h], axis=-1, keepdims=True)
            yc = y[h] - mean
            var = jnp.mean(yc * yc, axis=-1, keepdims=True)
            yn = yc * lax.rsqrt(var + GN_EPS) * lnw_ref[:, lanes[h]] + lnb_ref[:, lanes[h]]
            bonus = jnp.sum(r[h] * kmod[h] * rk_ref[:, lanes[h]], axis=-1, keepdims=True) * v[h]
            o_ref[0, rows, lanes[h]] = yn + bonus
        return carry

    lax.fori_loop(0, chunks, chunk_body, 0)


def _wkv(r, k, v, lw, a, kk_p, ka_p, rk_p, lnw_p, lnb_p, heads=8, rows=256):
    b, s, d = r.shape
    width = heads * HEAD
    seq_spec = pl.BlockSpec((1, rows, width), lambda bi, hi, ci: (bi, ci, hi))
    par_spec = pl.BlockSpec((1, width), lambda bi, hi, ci: (0, hi))
    return pl.pallas_call(
        functools.partial(_wkv_kernel, heads, rows // CHUNK),
        out_shape=jax.ShapeDtypeStruct((b, s, d), F32),
        grid=(b, d // width, s // rows),
        in_specs=[seq_spec] * 5 + [par_spec] * 5,
        out_specs=seq_spec,
        scratch_shapes=[pltpu.VMEM((heads, HEAD, HEAD), F32)],
        compiler_params=pltpu.CompilerParams(
            dimension_semantics=("parallel", "parallel", "arbitrary"),
            vmem_limit_bytes=VMEM_LIMIT),
        name="wkv",
    )(r, k, v, lw, a, kk_p, ka_p, rk_p, lnw_p, lnb_p)


def _proj_mlp_kernel(has_gate, final_norm, ff_chunk, *refs):
    refs = list(refs)
    y_ref = refs.pop(0)
    gate_ref = refs.pop(0) if has_gate else None
    x_ref, wo_ref, bo_ref, gm_ref, win_ref, wout_ref = refs[:6]
    refs = refs[6:]
    gf_ref = refs.pop(0) if final_norm else None
    o_ref = refs.pop(0)

    y = y_ref[...].astype(F32)
    if has_gate:
        y = y * gate_ref[...]
    x1 = x_ref[...] + _dot(y, wo_ref[...]) + bo_ref[...]
    hb = _rms(x1, gm_ref[...]).astype(BF16)
    acc = x1
    for c in range(D_FF // ff_chunk):
        cols = slice(c * ff_chunk, (c + 1) * ff_chunk)
        hid = jnp.maximum(jnp.dot(hb, win_ref[:, cols], preferred_element_type=F32), 0.0)
        acc = acc + _dot(hid * hid, wout_ref[cols, :])
    if final_norm:
        acc = _rms(acc, gf_ref[...])
    o_ref[...] = acc


def _proj_mlp(y, gate, x, wo, bo, gm, win, wout, gf, tm=256, ff_chunk=1024):
    t, d = x.shape
    row_spec = pl.BlockSpec((tm, d), lambda i: (i, 0))
    args = [y] + ([gate] if gate is not None else []) + [x, wo, bo, gm, win, wout]
    specs = [row_spec] * (len(args) - 5) + [_const_spec(c.shape) for c in args[-5:]]
    if gf is not None:
        args.append(gf)
        specs.append(_const_spec(gf.shape))
    return pl.pallas_call(
        functools.partial(_proj_mlp_kernel, gate is not None, gf is not None, ff_chunk),
        out_shape=jax.ShapeDtypeStruct((t, d), F32),
        grid=(t // tm,),
        in_specs=specs,
        out_specs=row_spec,
        compiler_params=pltpu.CompilerParams(
            dimension_semantics=("parallel",), vmem_limit_bytes=VMEM_LIMIT),
        name="proj_mlp",
    )(*args)


def _rope_tab_kernel(pos_ref, freq_ref, cos_ref, sin_ref):
    ang = pos_ref[...] * freq_ref[...]
    lane = lax.broadcasted_iota(jnp.int32, ang.shape, 1)
    cos_ref[...] = jnp.cos(ang)
    sin_ref[...] = jnp.where(lane % HEAD < HEAD // 2, -1.0, 1.0) * jnp.sin(ang)


def _rope_tab(pos_col, freq_row, tm=1024):
    t = pos_col.shape[0]
    out = jax.ShapeDtypeStruct((t, LANES), F32)
    spec = pl.BlockSpec((tm, LANES), lambda i: (i, 0))
    return pl.pallas_call(
        _rope_tab_kernel,
        out_shape=(out, out),
        grid=(t // tm,),
        in_specs=[pl.BlockSpec((tm, 1), lambda i: (i, 0)), _const_spec(freq_row.shape)],
        out_specs=(spec, spec),
        compiler_params=pltpu.CompilerParams(dimension_semantics=("parallel",)),
        name="rope_tab",
    )(pos_col, freq_row)


def _attn_qkv_kernel(x_ref, g_ref, w_ref, b_ref, cos_ref, sin_ref, o_ref):
    h = _rms(x_ref[...], g_ref[...])
    qkv = _dot(h, w_ref[...]) + b_ref[...]
    cos = cos_ref[...]
    sin = sin_ref[...]
    lane = lax.broadcasted_iota(jnp.int32, cos.shape, 1)
    first_half = lane % HEAD < HEAD // 2
    n_q = N_HEADS * HEAD // LANES
    n_rot = (N_HEADS + N_KV_HEADS) * HEAD // LANES
    for j in range(QKV_DIM // LANES):
        cols = slice(j * LANES, (j + 1) * LANES)
        blk = qkv[:, cols]
        if j < n_rot:
            rot = jnp.where(first_half, pltpu.roll(blk, LANES - HEAD // 2, axis=1),
                            pltpu.roll(blk, HEAD // 2, axis=1))
            blk = blk * cos + rot * sin
        if j < n_q:
            blk = blk * (HEAD ** -0.5)
        o_ref[:, cols] = blk.astype(o_ref.dtype)


def _attn_qkv(x, g, w, b, cos, sin, tm=256):
    t, d = x.shape
    row = lambda width: pl.BlockSpec((tm, width), lambda i: (i, 0))
    return pl.pallas_call(
        _attn_qkv_kernel,
        out_shape=jax.ShapeDtypeStruct((t, QKV_DIM), BF16),
        grid=(t // tm,),
        in_specs=[row(d), _const_spec(g.shape), _const_spec(w.shape), _const_spec(b.shape),
                  row(LANES), row(LANES)],
        out_specs=row(QKV_DIM),
        compiler_params=pltpu.CompilerParams(
            dimension_semantics=("parallel",), vmem_limit_bytes=VMEM_LIMIT),
        name="attn_qkv",
    )(x, g, w, b, cos, sin)


def _attn_core_kernel(q_ref, kp_ref, kc_ref, vp_ref, vc_ref, sink_ref, o_ref):
    n = pl.program_id(1)
    rows = GROUP * WINDOW
    ri = lax.broadcasted_iota(jnp.int32, (rows, WINDOW), 0) & (WINDOW - 1)
    ci = lax.broadcasted_iota(jnp.int32, (rows, WINDOW), 1)
    prev_ok = (ci > ri) & (n > 0)
    cur_ok = ci <= ri
    neg = -jnp.inf
    kvs = range(N_KV_HEADS)
    kvl = [slice(j * HEAD, (j + 1) * HEAD) for j in kvs]
    heads = [[j * GROUP + g for g in range(GROUP)] for j in kvs]
    q = [jnp.concatenate([q_ref[0, :, h * HEAD:(h + 1) * HEAD] for h in heads[j]], axis=0)
         for j in kvs]
    sink = [jnp.concatenate([jnp.broadcast_to(sink_ref[:, h:h + 1], (WINDOW, 1))
                             for h in heads[j]], axis=0) for j in kvs]
    s_p = [jnp.where(prev_ok, _dot_nt(q[j], kp_ref[0, :, kvl[j]]), neg) for j in kvs]
    s_c = [jnp.where(cur_ok, _dot_nt(q[j], kc_ref[0, :, kvl[j]]), neg) for j in kvs]
    m = [jnp.maximum(jnp.maximum(jnp.max(s_p[j], axis=-1, keepdims=True),
                                 jnp.max(s_c[j], axis=-1, keepdims=True)), sink[j]) for j in kvs]
    p_p = [jnp.exp(s_p[j] - m[j]) for j in kvs]
    p_c = [jnp.exp(s_c[j] - m[j]) for j in kvs]
    den = [jnp.sum(p_p[j], axis=-1, keepdims=True) + jnp.sum(p_c[j], axis=-1, keepdims=True)
           + jnp.exp(sink[j] - m[j]) for j in kvs]
    o = [(_dot(p_p[j], vp_ref[0, :, kvl[j]]) + _dot(p_c[j], vc_ref[0, :, kvl[j]])) / den[j]
         for j in kvs]
    for j in kvs:
        for g, h in enumerate(heads[j]):
            o_ref[0, :, h * HEAD:(h + 1) * HEAD] = (
                o[j][g * WINDOW:(g + 1) * WINDOW].astype(o_ref.dtype))


def _attn_core(qkv, sinks):
    b, s, _ = qkv.shape
    nq = N_HEADS * HEAD
    k_blk = nq // LANES
    v_blk = k_blk + N_KV_HEADS * HEAD // LANES
    q_spec = pl.BlockSpec((1, WINDOW, nq), lambda bi, n: (bi, n, 0))
    cur = lambda blk: pl.BlockSpec((1, WINDOW, LANES), lambda bi, n: (bi, n, blk))
    prev = lambda blk: pl.BlockSpec((1, WINDOW, LANES),
                                    lambda bi, n: (bi, jnp.maximum(n - 1, 0), blk))
    return pl.pallas_call(
        _attn_core_kernel,
        out_shape=jax.ShapeDtypeStruct((b, s, nq), BF16),
        grid=(b, s // WINDOW),
        in_specs=[q_spec, prev(k_blk), cur(k_blk), prev(v_blk), cur(v_blk),
                  _const_spec(sinks.shape)],
        out_specs=q_spec,
        compiler_params=pltpu.CompilerParams(
            dimension_semantics=("parallel", "arbitrary"), vmem_limit_bytes=VMEM_LIMIT),
        name="attn_core",
    )(qkv, qkv, qkv, qkv, qkv, sinks)


def kernel(x, positions, norm_mix_g, norm_mlp_g, norm_final_g, rwkv_mu, rwkv_w_r, rwkv_w_k, rwkv_w_v, rwkv_w_o, rwkv_w0, rwkv_w1, rwkv_w2, rwkv_a0, rwkv_a1, rwkv_a2, rwkv_g1, rwkv_g2, rwkv_k_k, rwkv_k_a, rwkv_r_k, rwkv_ln_w, rwkv_ln_b, attn_w_qkv, attn_b_qkv, attn_sinks, attn_w_o, attn_b_o, mlp_w_in, mlp_w_out):
    b, s, d = x.shape
    t = b * s
    x2d = x.reshape(t, d)
    row = lambda p: p.reshape(1, -1)
    bf = lambda w: w.astype(BF16)
    zero_bias = jnp.zeros((1, d), F32)

    r, k, v, lw, a, gate = _rwkv_pre(
        x2d, s, row(norm_mix_g[0]), rwkv_mu[0], bf(rwkv_w_r[0]), bf(rwkv_w_k[0]),
        bf(rwkv_w_v[0]), row(rwkv_w0[0]), bf(rwkv_w1[0]), bf(rwkv_w2[0]), row(rwkv_a0[0]),
        bf(rwkv_a1[0]), bf(rwkv_a2[0]), bf(rwkv_g1[0]), bf(rwkv_g2[0]))
    seq = lambda z: z.reshape(b, s, d)
    y = _wkv(seq(r), seq(k), seq(v), seq(lw), seq(a), row(rwkv_k_k[0]), row(rwkv_k_a[0]),
             row(rwkv_r_k[0]), row(rwkv_ln_w[0]), row(rwkv_ln_b[0]))
    x2d = _proj_mlp(y.reshape(t, d), gate, x2d, bf(rwkv_w_o[0]), zero_bias,
                    row(norm_mlp_g[0]), bf(mlp_w_in[0]), bf(mlp_w_out[0]), None)

    inv_freq = ROPE_THETA ** (-jnp.arange(0, HEAD, 2, dtype=F32) / HEAD)
    freq_row = jnp.tile(inv_freq, LANES // (HEAD // 2)).reshape(1, LANES)
    cos, sin = _rope_tab(positions.astype(F32).reshape(t, 1), freq_row)
    qkv = _attn_qkv(x2d, row(norm_mix_g[1]), bf(attn_w_qkv[0]), row(attn_b_qkv[0]), cos, sin)
    o = _attn_core(qkv.reshape(b, s, QKV_DIM), row(attn_sinks[0]))
    out = _proj_mlp(o.reshape(t, d), None, x2d, bf(attn_w_o[0]), row(attn_b_o[0]),
                    row(norm_mlp_g[1]), bf(mlp_w_in[1]), bf(mlp_w_out[1]), row(norm_final_g))
    return out.reshape(b, s, d)
```

```python
import functools

import jax
import jax.numpy as jnp
from jax import lax
from jax.experimental import pallas as pl
from jax.experimental.pallas import tpu as pltpu

F32 = jnp.float32
BF16 = jnp.bfloat16

D_MODEL = 1024
HEAD = 64
N_HEADS = D_MODEL // HEAD
N_KV_HEADS = 2
GROUP = N_HEADS // N_KV_HEADS
WINDOW = 128
QKV_DIM = (N_HEADS + 2 * N_KV_HEADS) * HEAD
D_FF = 4 * D_MODEL
ROPE_THETA = 10000.0
RMS_EPS = 1e-5
GN_EPS = 64e-5

LANES = 128
CHUNK = 64
VMEM_LIMIT = 56 * 1024 * 1024


def _rms(x, g):
    return x * lax.rsqrt(jnp.mean(x * x, axis=-1, keepdims=True) + RMS_EPS) * g


def _dot(a, b):
    return jnp.dot(a.astype(BF16), b.astype(BF16), preferred_element_type=F32)


def _dot_nt(a, b):
    return lax.dot_general(a.astype(BF16), b.astype(BF16), (((1,), (1,)), ((), ())),
                           preferred_element_type=F32)


def _dot_tn(a, b):
    return lax.dot_general(a.astype(BF16), b.astype(BF16), (((0,), (0,)), ((), ())),
                           preferred_element_type=F32)


def _const_spec(shape):
    nd = len(shape)
    return pl.BlockSpec(shape, lambda *_: (0,) * nd)


def _rwkv_pre_kernel(seq_tiles, x_ref, xp_ref, g_ref, mu_ref, wr_ref, wk_ref, wv_ref,
                     w0_ref, w1_ref, w2_ref, a0_ref, a1_ref, a2_ref, g1_ref, g2_ref,
                     r_out, k_out, v_out, lw_out, a_out, g_out):
    i = pl.program_id(0)
    g = g_ref[...]
    h = _rms(x_ref[...], g)
    hp = _rms(xp_ref[...], g)
    last = jnp.where(i % seq_tiles == 0, 0.0, hp[7:8, :])
    row = lax.broadcasted_iota(jnp.int32, h.shape, 0)
    h_prev = jnp.where(row == 0, last, pltpu.roll(h, 1, axis=0))
    dx = h_prev - h
    mu = mu_ref[...]
    xr = h + dx * mu[0:1]
    xw = h + dx * mu[1:2]
    xk = h + dx * mu[2:3]
    xv = h + dx * mu[3:4]
    xa = h + dx * mu[4:5]
    xg = h + dx * mu[5:6]
    r_out[...] = _dot(xr, wr_ref[...])
    k_out[...] = _dot(xk, wk_ref[...])
    v_out[...] = _dot(xv, wv_ref[...])
    w_pre = w0_ref[...] + _dot(jnp.tanh(_dot(xw, w1_ref[...])), w2_ref[...])
    w_log = -jax.nn.softplus(-w_pre) - 0.5
    lw_out[...] = -jnp.exp(w_log)
    a_out[...] = jax.nn.sigmoid(a0_ref[...] + _dot(_dot(xa, a1_ref[...]), a2_ref[...]))
    g_out[...] = _dot(jax.nn.sigmoid(_dot(xg, g1_ref[...])), g2_ref[...])


def _rwkv_pre(x2d, seq_len, g, mu, wr, wk, wv, w0, w1, w2, a0, a1, a2, g1, g2, tm=256):
    t, d = x2d.shape
    seq_tiles = seq_len // tm
    row_spec = pl.BlockSpec((tm, d), lambda i: (i, 0))
    prev_spec = pl.BlockSpec((8, d), lambda i: (jnp.maximum(i * (tm // 8) - 1, 0), 0))
    consts = (g, mu, wr, wk, wv, w0, w1, w2, a0, a1, a2, g1, g2)
    out = jax.ShapeDtypeStruct((t, d), F32)
    return pl.pallas_call(
        functools.partial(_rwkv_pre_kernel, seq_tiles),
        out_shape=(out,) * 6,
        grid=(t // tm,),
        in_specs=[row_spec, prev_spec] + [_const_spec(c.shape) for c in consts],
        out_specs=(row_spec,) * 6,
        compiler_params=pltpu.CompilerParams(
            dimension_semantics=("parallel",), vmem_limit_bytes=VMEM_LIMIT),
        name="rwkv_pre",
    )(x2d, x2d, *consts)


def _cumsum_rows(x):
    n = x.shape[0]
    row = lax.broadcasted_iota(jnp.int32, x.shape, 0)
    s = 1
    while s < n:
        x = x + jnp.where(row >= s, pltpu.roll(x, s, axis=0), 0.0)
        s *= 2
    return x


INV_BASE = 8
GROUP_HEADS = 4
GROUP_W = GROUP_HEADS * HEAD


def _head_sum(x):
    low = lax.broadcasted_iota(jnp.int32, (x.shape[0], LANES), 1) < HEAD
    outs = []
    for c in range(x.shape[1] // LANES):
        xc = x[:, c * LANES:(c + 1) * LANES]
        s_lo = jnp.sum(jnp.where(low, xc, 0.0), axis=-1, keepdims=True)
        s_hi = jnp.sum(jnp.where(low, 0.0, xc), axis=-1, keepdims=True)
        outs.append(jnp.where(low, s_lo, s_hi))
    return jnp.concatenate(outs, axis=1)


def _wkv_kernel(chunks, r_ref, k_ref, v_ref, lw_ref, a_ref,
                kk_ref, ka_ref, rk_ref, lnw_ref, lnb_ref, o_ref, state_ref):
    n_groups = D_MODEL // GROUP_W
    gs = range(n_groups)

    @pl.when(pl.program_id(1) == 0)
    def _():
        state_ref[...] = jnp.zeros_like(state_ref)

    ri = lax.broadcasted_iota(jnp.int32, (CHUNK, GROUP_W), 0)
    lane = lax.broadcasted_iota(jnp.int32, (CHUNK, GROUP_W), 1)
    cj = lane & (HEAD - 1)
    lane_head = lax.shift_right_logical(lane, HEAD.bit_length() - 1)
    head_masks = [lane_head == h for h in range(GROUP_HEADS)]
    blk = lambda idx, size: lax.shift_right_logical(idx, size.bit_length() - 1)
    strict = cj < ri
    incl = cj <= ri
    eye = (cj == ri).astype(F32)
    diag = blk(ri, INV_BASE) == blk(cj, INV_BASE)
    bands = []
    size = INV_BASE
    while size < CHUNK:
        bands.append((blk(ri, 2 * size) == blk(cj, 2 * size))
                     & (blk(ri, size) != blk(cj, size)))
        size *= 2
    sr = lax.broadcasted_iota(jnp.int32, (GROUP_W, GROUP_W), 0)
    sc = lax.broadcasted_iota(jnp.int32, (GROUP_W, GROUP_W), 1)
    state_mask = blk(sr, HEAD) == blk(sc, HEAD)

    def bd(y):
        return jnp.concatenate([jnp.where(m, y, 0.0) for m in head_masks], axis=0).astype(BF16)

    def mm(x, w):
        return jnp.dot(x.astype(BF16), w, preferred_element_type=F32)

    def mm_nt(x, w):
        return lax.dot_general(x.astype(BF16), w, (((1,), (1,)), ((), ())),
                               preferred_element_type=F32)

    def chunk_body(c, carry):
        rows = pl.ds(pl.multiple_of(c * CHUNK, CHUNK), CHUNK)
        lw = lw_ref[0, rows, :]
        r = r_ref[0, rows, :]
        k = k_ref[0, rows, :]
        v = v_ref[0, rows, :]
        a = a_ref[0, rows, :]
        linc = _cumsum_rows(lw)
        p_inc = jnp.exp(linc)
        p_exc = jnp.exp(linc - lw)
        p_inv = jnp.exp(-linc)
        kk = k * kk_ref[...]
        kk = kk * lax.rsqrt(jnp.maximum(_head_sum(kk * kk), 1e-24))
        kmod = k * (1.0 + (a - 1.0) * ka_ref[...])
        at_f = -kk * p_exc
        rt_f = r * p_inc
        bt_f = kk * a * p_inv
        kt_f = kmod * p_inv
        p_end = p_inc[CHUNK - 1:CHUNK, :]

        cols = [slice(g * GROUP_W, (g + 1) * GROUP_W) for g in gs]
        bt = [bt_f[:, cl] for cl in cols]
        kt = [kt_f[:, cl] for cl in cols]
        vg = [v[:, cl] for cl in cols]
        lhs = [jnp.concatenate([at_f[:, cl], rt_f[:, cl]], axis=0).astype(BF16) for cl in cols]
        ab = [mm_nt(lhs[g], bd(bt[g])) for g in gs]
        ak = [mm_nt(lhs[g], bd(kt[g])) for g in gs]
        a_ab = [jnp.where(strict, x[:CHUNK], 0.0) for x in ab]
        a_rb = [jnp.where(incl, x[CHUNK:], 0.0) for x in ab]
        a_k = [jnp.concatenate([jnp.where(strict, x[:CHUNK], 0.0),
                                jnp.where(incl, x[CHUNK:], 0.0)], axis=0) for x in ak]

        d1 = [jnp.where(diag, x, 0.0) for x in a_ab]
        d2 = [mm(d1[g], bd(d1[g])) for g in gs]
        t = [eye + x for x in d1]
        td = [mm(jnp.concatenate([t[g], d2[g]], axis=0), bd(d2[g])) for g in gs]
        t = [t[g] + td[g][:CHUNK] for g in gs]
        t = [t[g] + mm(t[g], bd(td[g][CHUNK:])) for g in gs]
        for band in bands:
            te = [mm(t[g], bd(jnp.where(band, a_ab[g], 0.0))) for g in gs]
            t = [t[g] + mm(te[g], bd(t[g])) for g in gs]

        s0 = [state_ref[g] for g in gs]
        zy = [mm_nt(lhs[g], s0[g].astype(BF16)) for g in gs]
        av = [mm(a_k[g], bd(vg[g])) for g in gs]
        u = [mm(t[g], bd(zy[g][:CHUNK] + av[g][:CHUNK])) for g in gs]
        y = [zy[g][CHUNK:] + av[g][CHUNK:] + mm(a_rb[g], bd(u[g])) for g in gs]
        for g in gs:
            uv = jnp.concatenate([u[g], vg[g]], axis=0)
            bk = jnp.concatenate([bt[g], kt[g]], axis=0)
            state_ref[g] = jnp.where(state_mask, s0[g] + _dot_tn(uv, bk), 0.0) * p_end[:, cols[g]]

        y_f = jnp.concatenate(y, axis=1)
        inv_n = 1.0 / HEAD
        yc = y_f - _head_sum(y_f) * inv_n
        var = _head_sum(yc * yc) * inv_n
        yn = yc * lax.rsqrt(var + GN_EPS) * lnw_ref[...] + lnb_ref[...]
        o_ref[0, rows, :] = yn + _head_sum(r * kmod * rk_ref[...]) * v
        return carry

    lax.fori_loop(0, chunks, chunk_body, 0)


def _wkv(r, k, v, lw, a, kk_p, ka_p, rk_p, lnw_p, lnb_p, rows=256):
    b, s, d = r.shape
    seq_spec = pl.BlockSpec((1, rows, d), lambda bi, ci: (bi, ci, 0))
    par_spec = pl.BlockSpec((1, d), lambda bi, ci: (0, 0))
    return pl.pallas_call(
        functools.partial(_wkv_kernel, rows // CHUNK),
        out_shape=jax.ShapeDtypeStruct((b, s, d), F32),
        grid=(b, s // rows),
        in_specs=[seq_spec] * 5 + [par_spec] * 5,
        out_specs=seq_spec,
        scratch_shapes=[pltpu.VMEM((d // GROUP_W, GROUP_W, GROUP_W), F32)],
        compiler_params=pltpu.CompilerParams(
            dimension_semantics=("parallel", "arbitrary"), vmem_limit_bytes=VMEM_LIMIT),
        name="wkv",
    )(r, k, v, lw, a, kk_p, ka_p, rk_p, lnw_p, lnb_p)


def _proj_mlp_kernel(has_gate, final_norm, ff_chunk, *refs):
    refs = list(refs)
    y_ref = refs.pop(0)
    gate_ref = refs.pop(0) if has_gate else None
    x_ref, wo_ref, bo_ref, gm_ref, win_ref, wout_ref = refs[:6]
    refs = refs[6:]
    gf_ref = refs.pop(0) if final_norm else None
    o_ref = refs.pop(0)

    y = y_ref[...].astype(F32)
    if has_gate:
        y = y * gate_ref[...]
    x1 = x_ref[...] + _dot(y, wo_ref[...]) + bo_ref[...]
    hb = _rms(x1, gm_ref[...]).astype(BF16)
    acc = x1
    for c in range(D_FF // ff_chunk):
        cols = slice(c * ff_chunk, (c + 1) * ff_chunk)
        hid = jnp.maximum(jnp.dot(hb, win_ref[:, cols], preferred_element_type=F32), 0.0)
        acc = acc + _dot(hid * hid, wout_ref[cols, :])
    if final_norm:
        acc = _rms(acc, gf_ref[...])
    o_ref[...] = acc


def _proj_mlp(y, gate, x, wo, bo, gm, win, wout, gf, tm=256, ff_chunk=1024):
    t, d = x.shape
    row_spec = pl.BlockSpec((tm, d), lambda i: (i, 0))
    args = [y] + ([gate] if gate is not None else []) + [x, wo, bo, gm, win, wout]
    specs = [row_spec] * (len(args) - 5) + [_const_spec(c.shape) for c in args[-5:]]
    if gf is not None:
        args.append(gf)
        specs.append(_const_spec(gf.shape))
    return pl.pallas_call(
        functools.partial(_proj_mlp_kernel, gate is not None, gf is not None, ff_chunk),
        out_shape=jax.ShapeDtypeStruct((t, d), F32),
        grid=(t // tm,),
        in_specs=specs,
        out_specs=row_spec,
        compiler_params=pltpu.CompilerParams(
            dimension_semantics=("parallel",), vmem_limit_bytes=VMEM_LIMIT),
        name="proj_mlp",
    )(*args)


def _rope_tab_kernel(pos_ref, freq_ref, cos_ref, sin_ref):
    ang = pos_ref[...] * freq_ref[...]
    lane = lax.broadcasted_iota(jnp.int32, ang.shape, 1)
    cos_ref[...] = jnp.cos(ang)
    sin_ref[...] = jnp.where(lane % HEAD < HEAD // 2, -1.0, 1.0) * jnp.sin(ang)


def _rope_tab(pos_col, freq_row, tm=1024):
    t = pos_col.shape[0]
    out = jax.ShapeDtypeStruct((t, LANES), F32)
    spec = pl.BlockSpec((tm, LANES), lambda i: (i, 0))
    return pl.pallas_call(
        _rope_tab_kernel,
        out_shape=(out, out),
        grid=(t // tm,),
        in_specs=[pl.BlockSpec((tm, 1), lambda i: (i, 0)), _const_spec(freq_row.shape)],
        out_specs=(spec, spec),
        compiler_params=pltpu.CompilerParams(dimension_semantics=("parallel",)),
        name="rope_tab",
    )(pos_col, freq_row)


def _attn_qkv_kernel(x_ref, g_ref, w_ref, b_ref, cos_ref, sin_ref, o_ref):
    h = _rms(x_ref[...], g_ref[...])
    qkv = _dot(h, w_ref[...]) + b_ref[...]
    cos = cos_ref[...]
    sin = sin_ref[...]
    lane = lax.broadcasted_iota(jnp.int32, cos.shape, 1)
    first_half = lane % HEAD < HEAD // 2
    n_q = N_HEADS * HEAD // LANES
    n_rot = (N_HEADS + N_KV_HEADS) * HEAD // LANES
    for j in range(QKV_DIM // LANES):
        cols = slice(j * LANES, (j + 1) * LANES)
        blk = qkv[:, cols]
        if j < n_rot:
            rot = jnp.where(first_half, pltpu.roll(blk, LANES - HEAD // 2, axis=1),
                            pltpu.roll(blk, HEAD // 2, axis=1))
            blk = blk * cos + rot * sin
        if j < n_q:
            blk = blk * (HEAD ** -0.5)
        o_ref[:, cols] = blk.astype(o_ref.dtype)


def _attn_qkv(x, g, w, b, cos, sin, tm=256):
    t, d = x.shape
    row = lambda width: pl.BlockSpec((tm, width), lambda i: (i, 0))
    return pl.pallas_call(
        _attn_qkv_kernel,
        out_shape=jax.ShapeDtypeStruct((t, QKV_DIM), BF16),
        grid=(t // tm,),
        in_specs=[row(d), _const_spec(g.shape), _const_spec(w.shape), _const_spec(b.shape),
                  row(LANES), row(LANES)],
        out_specs=row(QKV_DIM),
        compiler_params=pltpu.CompilerParams(
            dimension_semantics=("parallel",), vmem_limit_bytes=VMEM_LIMIT),
        name="attn_qkv",
    )(x, g, w, b, cos, sin)


def _attn_core_kernel(q_ref, kp_ref, kc_ref, vp_ref, vc_ref, sink_ref, o_ref):
    n = pl.program_id(1)
    rows = GROUP * WINDOW
    ri = lax.broadcasted_iota(jnp.int32, (rows, WINDOW), 0) & (WINDOW - 1)
    ci = lax.broadcasted_iota(jnp.int32, (rows, WINDOW), 1)
    prev_ok = (ci > ri) & (n > 0)
    cur_ok = ci <= ri
    neg = -jnp.inf
    kvs = range(N_KV_HEADS)
    kvl = [slice(j * HEAD, (j + 1) * HEAD) for j in kvs]
    heads = [[j * GROUP + g for g in range(GROUP)] for j in kvs]
    q = [jnp.concatenate([q_ref[0, :, h * HEAD:(h + 1) * HEAD] for h in heads[j]], axis=0)
         for j in kvs]
    sink = [jnp.concatenate([jnp.broadcast_to(sink_ref[:, h:h + 1], (WINDOW, 1))
                             for h in heads[j]], axis=0) for j in kvs]
    s_p = [jnp.where(prev_ok, _dot_nt(q[j], kp_ref[0, :, kvl[j]]), neg) for j in kvs]
    s_c = [jnp.where(cur_ok, _dot_nt(q[j], kc_ref[0, :, kvl[j]]), neg) for j in kvs]
    m = [jnp.maximum(jnp.maximum(jnp.max(s_p[j], axis=-1, keepdims=True),
                                 jnp.max(s_c[j], axis=-1, keepdims=True)), sink[j]) for j in kvs]
    p_p = [jnp.exp(s_p[j] - m[j]) for j in kvs]
    p_c = [jnp.exp(s_c[j] - m[j]) for j in kvs]
    den = [jnp.sum(p_p[j], axis=-1, keepdims=True) + jnp.sum(p_c[j], axis=-1, keepdims=True)
           + jnp.exp(sink[j] - m[j]) for j in kvs]
    o = [(_dot(p_p[j], vp_ref[0, :, kvl[j]]) + _dot(p_c[j], vc_ref[0, :, kvl[j]])) / den[j]
         for j in kvs]
    for j in kvs:
        for g, h in enumerate(heads[j]):
            o_ref[0, :, h * HEAD:(h + 1) * HEAD] = (
                o[j][g * WINDOW:(g + 1) * WINDOW].astype(o_ref.dtype))


def _attn_core(qkv, sinks):
    b, s, _ = qkv.shape
    nq = N_HEADS * HEAD
    k_blk = nq // LANES
    v_blk = k_blk + N_KV_HEADS * HEAD // LANES
    q_spec = pl.BlockSpec((1, WINDOW, nq), lambda bi, n: (bi, n, 0))
    cur = lambda blk: pl.BlockSpec((1, WINDOW, LANES), lambda bi, n: (bi, n, blk))
    prev = lambda blk: pl.BlockSpec((1, WINDOW, LANES),
                                    lambda bi, n: (bi, jnp.maximum(n - 1, 0), blk))
    return pl.pallas_call(
        _attn_core_kernel,
        out_shape=jax.ShapeDtypeStruct((b, s, nq), BF16),
        grid=(b, s // WINDOW),
        in_specs=[q_spec, prev(k_blk), cur(k_blk), prev(v_blk), cur(v_blk),
                  _const_spec(sinks.shape)],
        out_specs=q_spec,
        compiler_params=pltpu.CompilerParams(
            dimension_semantics=("parallel", "arbitrary"), vmem_limit_bytes=VMEM_LIMIT),
        name="attn_core",
    )(qkv, qkv, qkv, qkv, qkv, sinks)


def kernel(x, positions, norm_mix_g, norm_mlp_g, norm_final_g, rwkv_mu, rwkv_w_r, rwkv_w_k, rwkv_w_v, rwkv_w_o, rwkv_w0, rwkv_w1, rwkv_w2, rwkv_a0, rwkv_a1, rwkv_a2, rwkv_g1, rwkv_g2, rwkv_k_k, rwkv_k_a, rwkv_r_k, rwkv_ln_w, rwkv_ln_b, attn_w_qkv, attn_b_qkv, attn_sinks, attn_w_o, attn_b_o, mlp_w_in, mlp_w_out):
    b, s, d = x.shape
    t = b * s
    x2d = x.reshape(t, d)
    row = lambda p: p.reshape(1, -1)
    bf = lambda w: w.astype(BF16)
    zero_bias = jnp.zeros((1, d), F32)

    r, k, v, lw, a, gate = _rwkv_pre(
        x2d, s, row(norm_mix_g[0]), rwkv_mu[0], bf(rwkv_w_r[0]), bf(rwkv_w_k[0]),
        bf(rwkv_w_v[0]), row(rwkv_w0[0]), bf(rwkv_w1[0]), bf(rwkv_w2[0]), row(rwkv_a0[0]),
        bf(rwkv_a1[0]), bf(rwkv_a2[0]), bf(rwkv_g1[0]), bf(rwkv_g2[0]))
    seq = lambda z: z.reshape(b, s, d)
    y = _wkv(seq(r), seq(k), seq(v), seq(lw), seq(a), row(rwkv_k_k[0]), row(rwkv_k_a[0]),
             row(rwkv_r_k[0]), row(rwkv_ln_w[0]), row(rwkv_ln_b[0]))
    x2d = _proj_mlp(y.reshape(t, d), gate, x2d, bf(rwkv_w_o[0]), zero_bias,
                    row(norm_mlp_g[0]), bf(mlp_w_in[0]), bf(mlp_w_out[0]), None)

    inv_freq = ROPE_THETA ** (-jnp.arange(0, HEAD, 2, dtype=F32) / HEAD)
    freq_row = jnp.tile(inv_freq, LANES // (HEAD // 2)).reshape(1, LANES)
    cos, sin = _rope_tab(positions.astype(F32).reshape(t, 1), freq_row)
    qkv = _attn_qkv(x2d, row(norm_mix_g[1]), bf(attn_w_qkv[0]), row(attn_b_qkv[0]), cos, sin)
    o = _attn_core(qkv.reshape(b, s, QKV_DIM), row(attn_sinks[0]))
    out = _proj_mlp(o.reshape(t, d), None, x2d, bf(attn_w_o[0]), row(attn_b_o[0]),
                    row(norm_mlp_g[1]), bf(mlp_w_in[1]), bf(mlp_w_out[1]), row(norm_final_g))
    return out.reshape(b, s, d)
```

```python
import functools
import math

import jax
import jax.numpy as jnp
from jax import lax
from jax.experimental import pallas as pl
from jax.experimental.pallas import tpu as pltpu

F32 = jnp.float32
BF16 = jnp.bfloat16

D_MODEL = 1024
HEAD = 64
N_HEADS = D_MODEL // HEAD
N_KV_HEADS = 2
GROUP = N_HEADS // N_KV_HEADS
WINDOW = 128
QKV_DIM = (N_HEADS + 2 * N_KV_HEADS) * HEAD
D_FF = 4 * D_MODEL
ROPE_THETA = 10000.0
RMS_EPS = 1e-5
GN_EPS = 64e-5

LANES = 128
LOG2E = math.log2(math.e)
Q_SCALE = HEAD ** -0.5 * LOG2E
CHUNK = 64
VMEM_LIMIT = 56 * 1024 * 1024


def _rms(x, g):
    return x * lax.rsqrt(jnp.mean(x * x, axis=-1, keepdims=True) + RMS_EPS) * g


def _dot(a, b):
    return jnp.dot(a.astype(BF16), b.astype(BF16), preferred_element_type=F32)


def _dot_nt(a, b):
    return lax.dot_general(a.astype(BF16), b.astype(BF16), (((1,), (1,)), ((), ())),
                           preferred_element_type=F32)


def _dot_tn(a, b):
    return lax.dot_general(a.astype(BF16), b.astype(BF16), (((0,), (0,)), ((), ())),
                           preferred_element_type=F32)


def _const_spec(shape):
    nd = len(shape)
    return pl.BlockSpec(shape, lambda *_: (0,) * nd)


def _rwkv_pre_kernel(seq_tiles, x_ref, xp_ref, g_ref, mu_ref, wr_ref, wk_ref, wv_ref,
                     w0_ref, w1_ref, w2_ref, a0_ref, a1_ref, a2_ref, g1_ref, g2_ref,
                     r_out, k_out, v_out, lw_out, a_out, g_out):
    i = pl.program_id(0)
    g = g_ref[...]
    h = _rms(x_ref[...], g)
    hp = _rms(xp_ref[...], g)
    last = jnp.where(i % seq_tiles == 0, 0.0, hp[7:8, :])
    row = lax.broadcasted_iota(jnp.int32, h.shape, 0)
    h_prev = jnp.where(row == 0, last, pltpu.roll(h, 1, axis=0))
    dx = h_prev - h
    mu = mu_ref[...]
    xr = h + dx * mu[0:1]
    xw = h + dx * mu[1:2]
    xk = h + dx * mu[2:3]
    xv = h + dx * mu[3:4]
    xa = h + dx * mu[4:5]
    xg = h + dx * mu[5:6]
    r_out[...] = _dot(xr, wr_ref[...])
    k_out[...] = _dot(xk, wk_ref[...])
    v_out[...] = _dot(xv, wv_ref[...])
    w_pre = w0_ref[...] + _dot(jnp.tanh(_dot(xw, w1_ref[...])), w2_ref[...])
    lw_out[...] = jax.nn.sigmoid(w_pre) * (-math.exp(-0.5))
    a_out[...] = jax.nn.sigmoid(a0_ref[...] + _dot(_dot(xa, a1_ref[...]), a2_ref[...]))
    g_out[...] = _dot(jax.nn.sigmoid(_dot(xg, g1_ref[...])), g2_ref[...])


def _rwkv_pre(x2d, seq_len, g, mu, wr, wk, wv, w0, w1, w2, a0, a1, a2, g1, g2, tm=256):
    t, d = x2d.shape
    seq_tiles = seq_len // tm
    row_spec = pl.BlockSpec((tm, d), lambda i: (i, 0))
    prev_spec = pl.BlockSpec((8, d), lambda i: (jnp.maximum(i * (tm // 8) - 1, 0), 0))
    consts = (g, mu, wr, wk, wv, w0, w1, w2, a0, a1, a2, g1, g2)
    out = jax.ShapeDtypeStruct((t, d), F32)
    return pl.pallas_call(
        functools.partial(_rwkv_pre_kernel, seq_tiles),
        out_shape=(out,) * 6,
        grid=(t // tm,),
        in_specs=[row_spec, prev_spec] + [_const_spec(c.shape) for c in consts],
        out_specs=(row_spec,) * 6,
        compiler_params=pltpu.CompilerParams(
            dimension_semantics=("parallel",), vmem_limit_bytes=VMEM_LIMIT),
        name="rwkv_pre",
    )(x2d, x2d, *consts)


def _cumsum_rows(x):
    n = x.shape[0]
    row = lax.broadcasted_iota(jnp.int32, x.shape, 0)
    s = 1
    while s < n:
        x = x + jnp.where(row >= s, pltpu.roll(x, s, axis=0), 0.0)
        s *= 2
    return x


INV_BASE = 8
GROUP_HEADS = 4
GROUP_W = GROUP_HEADS * HEAD


def _head_sum(x):
    low = lax.broadcasted_iota(jnp.int32, (x.shape[0], LANES), 1) < HEAD
    outs = []
    for c in range(x.shape[1] // LANES):
        xc = x[:, c * LANES:(c + 1) * LANES]
        s_lo = jnp.sum(jnp.where(low, xc, 0.0), axis=-1, keepdims=True)
        s_hi = jnp.sum(jnp.where(low, 0.0, xc), axis=-1, keepdims=True)
        outs.append(jnp.where(low, s_lo, s_hi))
    return jnp.concatenate(outs, axis=1)


def _wkv_kernel(chunks, r_ref, k_ref, v_ref, lw_ref, a_ref,
                kk_ref, ka_ref, rk_ref, lnw_ref, lnb_ref, o_ref, state_ref):
    n_groups = D_MODEL // GROUP_W
    gs = range(n_groups)

    @pl.when(pl.program_id(1) == 0)
    def _():
        state_ref[...] = jnp.zeros_like(state_ref)

    ri = lax.broadcasted_iota(jnp.int32, (CHUNK, GROUP_W), 0)
    lane = lax.broadcasted_iota(jnp.int32, (CHUNK, GROUP_W), 1)
    cj = lane & (HEAD - 1)
    lane_head = lax.shift_right_logical(lane, HEAD.bit_length() - 1)
    head_masks = [lane_head == h for h in range(GROUP_HEADS)]
    blk = lambda idx, size: lax.shift_right_logical(idx, size.bit_length() - 1)
    strict = cj < ri
    incl = cj <= ri
    eye = (cj == ri).astype(F32)
    diag = blk(ri, INV_BASE) == blk(cj, INV_BASE)
    bands = []
    size = INV_BASE
    while size < CHUNK:
        bands.append((blk(ri, 2 * size) == blk(cj, 2 * size))
                     & (blk(ri, size) != blk(cj, size)))
        size *= 2
    sr = lax.broadcasted_iota(jnp.int32, (GROUP_W, GROUP_W), 0)
    sc = lax.broadcasted_iota(jnp.int32, (GROUP_W, GROUP_W), 1)
    state_mask = blk(sr, HEAD) == blk(sc, HEAD)

    def bd(y):
        return jnp.concatenate([jnp.where(m, y, 0.0) for m in head_masks], axis=0).astype(BF16)

    def mm(x, w):
        return jnp.dot(x.astype(BF16), w, preferred_element_type=F32)

    def mm_nt(x, w):
        return lax.dot_general(x.astype(BF16), w, (((1,), (1,)), ((), ())),
                               preferred_element_type=F32)

    def chunk_body(c, carry):
        rows = pl.ds(pl.multiple_of(c * CHUNK, CHUNK), CHUNK)
        lw = lw_ref[0, rows, :]
        r = r_ref[0, rows, :]
        k = k_ref[0, rows, :]
        v = v_ref[0, rows, :]
        a = a_ref[0, rows, :]
        linc = _cumsum_rows(lw)
        p_inc = jnp.exp(linc)
        p_exc = jnp.exp(linc - lw)
        p_inv = jnp.exp(-linc)
        kk = k * kk_ref[...]
        kk = kk * lax.rsqrt(jnp.maximum(_head_sum(kk * kk), 1e-24))
        kmod = k * (1.0 + (a - 1.0) * ka_ref[...])
        at_f = -kk * p_exc
        rt_f = r * p_inc
        bt_f = kk * a * p_inv
        kt_f = kmod * p_inv
        p_end = p_inc[CHUNK - 1:CHUNK, :]

        cols = [slice(g * GROUP_W, (g + 1) * GROUP_W) for g in gs]
        bt = [bt_f[:, cl] for cl in cols]
        kt = [kt_f[:, cl] for cl in cols]
        vg = [v[:, cl] for cl in cols]
        lhs = [jnp.concatenate([at_f[:, cl], rt_f[:, cl]], axis=0).astype(BF16) for cl in cols]
        ab = [mm_nt(lhs[g], bd(bt[g])) for g in gs]
        ak = [mm_nt(lhs[g], bd(kt[g])) for g in gs]
        a_ab = [jnp.where(strict, x[:CHUNK], 0.0) for x in ab]
        a_rb = [jnp.where(incl, x[CHUNK:], 0.0) for x in ab]
        a_k = [jnp.concatenate([jnp.where(strict, x[:CHUNK], 0.0),
                                jnp.where(incl, x[CHUNK:], 0.0)], axis=0) for x in ak]

        d1 = [jnp.where(diag, x, 0.0) for x in a_ab]
        d2 = [mm(d1[g], bd(d1[g])) for g in gs]
        t = [eye + x for x in d1]
        td = [mm(jnp.concatenate([t[g], d2[g]], axis=0), bd(d2[g])) for g in gs]
        t = [t[g] + td[g][:CHUNK] for g in gs]
        t = [t[g] + mm(t[g], bd(td[g][CHUNK:])) for g in gs]
        for band in bands:
            te = [mm(t[g], bd(jnp.where(band, a_ab[g], 0.0))) for g in gs]
            t = [t[g] + mm(te[g], bd(t[g])) for g in gs]

        s0 = [state_ref[g] for g in gs]
        zy = [mm_nt(lhs[g], s0[g].astype(BF16)) for g in gs]
        av = [mm(a_k[g], bd(vg[g])) for g in gs]
        u = [mm(t[g], bd(zy[g][:CHUNK] + av[g][:CHUNK])) for g in gs]
        y = [zy[g][CHUNK:] + av[g][CHUNK:] + mm(a_rb[g], bd(u[g])) for g in gs]
        for g in gs:
            uv = jnp.concatenate([u[g], vg[g]], axis=0)
            bk = jnp.concatenate([bt[g], kt[g]], axis=0)
            state_ref[g] = jnp.where(state_mask, s0[g] + _dot_tn(uv, bk), 0.0) * p_end[:, cols[g]]

        y_f = jnp.concatenate(y, axis=1)
        inv_n = 1.0 / HEAD
        yc = y_f - _head_sum(y_f) * inv_n
        var = _head_sum(yc * yc) * inv_n
        yn = yc * lax.rsqrt(var + GN_EPS) * lnw_ref[...] + lnb_ref[...]
        o_ref[0, rows, :] = yn + _head_sum(r * kmod * rk_ref[...]) * v
        return carry

    lax.fori_loop(0, chunks, chunk_body, 0)


def _wkv(r, k, v, lw, a, kk_p, ka_p, rk_p, lnw_p, lnb_p, rows=256):
    b, s, d = r.shape
    seq_spec = pl.BlockSpec((1, rows, d), lambda bi, ci: (bi, ci, 0))
    par_spec = pl.BlockSpec((1, d), lambda bi, ci: (0, 0))
    return pl.pallas_call(
        functools.partial(_wkv_kernel, rows // CHUNK),
        out_shape=jax.ShapeDtypeStruct((b, s, d), F32),
        grid=(b, s // rows),
        in_specs=[seq_spec] * 5 + [par_spec] * 5,
        out_specs=seq_spec,
        scratch_shapes=[pltpu.VMEM((d // GROUP_W, GROUP_W, GROUP_W), F32)],
        compiler_params=pltpu.CompilerParams(
            dimension_semantics=("parallel", "arbitrary"), vmem_limit_bytes=VMEM_LIMIT),
        name="wkv",
    )(r, k, v, lw, a, kk_p, ka_p, rk_p, lnw_p, lnb_p)


def _proj_mlp_kernel(has_gate, final_norm, ff_chunk, *refs):
    refs = list(refs)
    y_ref = refs.pop(0)
    gate_ref = refs.pop(0) if has_gate else None
    x_ref, wo_ref, bo_ref, gm_ref, win_ref, wout_ref = refs[:6]
    refs = refs[6:]
    gf_ref = refs.pop(0) if final_norm else None
    o_ref = refs.pop(0)

    y = y_ref[...].astype(F32)
    if has_gate:
        y = y * gate_ref[...]
    x1 = x_ref[...] + _dot(y, wo_ref[...]) + bo_ref[...]
    hb = _rms(x1, gm_ref[...]).astype(BF16)
    acc = x1
    for c in range(D_FF // ff_chunk):
        cols = slice(c * ff_chunk, (c + 1) * ff_chunk)
        hid = jnp.maximum(jnp.dot(hb, win_ref[:, cols], preferred_element_type=F32), 0.0)
        acc = acc + _dot(hid * hid, wout_ref[cols, :])
    if final_norm:
        acc = _rms(acc, gf_ref[...])
    o_ref[...] = acc


def _proj_mlp(y, gate, x, wo, bo, gm, win, wout, gf, tm=256, ff_chunk=1024):
    t, d = x.shape
    row_spec = pl.BlockSpec((tm, d), lambda i: (i, 0))
    args = [y] + ([gate] if gate is not None else []) + [x, wo, bo, gm, win, wout]
    specs = [row_spec] * (len(args) - 5) + [_const_spec(c.shape) for c in args[-5:]]
    if gf is not None:
        args.append(gf)
        specs.append(_const_spec(gf.shape))
    return pl.pallas_call(
        functools.partial(_proj_mlp_kernel, gate is not None, gf is not None, ff_chunk),
        out_shape=jax.ShapeDtypeStruct((t, d), F32),
        grid=(t // tm,),
        in_specs=specs,
        out_specs=row_spec,
        compiler_params=pltpu.CompilerParams(
            dimension_semantics=("parallel",), vmem_limit_bytes=VMEM_LIMIT),
        name="proj_mlp",
    )(*args)


def _rope_tab_kernel(pos_ref, freq_ref, cos_ref, sin_ref):
    ang = pos_ref[...] * freq_ref[...]
    lane = lax.broadcasted_iota(jnp.int32, ang.shape, 1)
    cos_ref[...] = jnp.cos(ang)
    sin_ref[...] = jnp.where(lane % HEAD < HEAD // 2, -1.0, 1.0) * jnp.sin(ang)


def _rope_tab(pos_col, freq_row, tm=1024):
    t = pos_col.shape[0]
    out = jax.ShapeDtypeStruct((t, LANES), F32)
    spec = pl.BlockSpec((tm, LANES), lambda i: (i, 0))
    return pl.pallas_call(
        _rope_tab_kernel,
        out_shape=(out, out),
        grid=(t // tm,),
        in_specs=[pl.BlockSpec((tm, 1), lambda i: (i, 0)), _const_spec(freq_row.shape)],
        out_specs=(spec, spec),
        compiler_params=pltpu.CompilerParams(dimension_semantics=("parallel",)),
        name="rope_tab",
    )(pos_col, freq_row)


def _attn_qkv_kernel(x_ref, g_ref, w_ref, b_ref, cos_ref, sin_ref, o_ref):
    h = _rms(x_ref[...], g_ref[...])
    qkv = _dot(h, w_ref[...]) + b_ref[...]
    cos = cos_ref[...]
    sin = sin_ref[...]
    lane = lax.broadcasted_iota(jnp.int32, cos.shape, 1)
    first_half = lane % HEAD < HEAD // 2
    n_q = N_HEADS * HEAD // LANES
    n_rot = (N_HEADS + N_KV_HEADS) * HEAD // LANES
    for j in range(QKV_DIM // LANES):
        cols = slice(j * LANES, (j + 1) * LANES)
        blk = qkv[:, cols]
        if j < n_rot:
            rot = jnp.where(first_half, pltpu.roll(blk, LANES - HEAD // 2, axis=1),
                            pltpu.roll(blk, HEAD // 2, axis=1))
            blk = blk * cos + rot * sin
        if j < n_q:
            blk = blk * Q_SCALE
        o_ref[:, cols] = blk.astype(o_ref.dtype)


def _attn_qkv(x, g, w, b, cos, sin, tm=256):
    t, d = x.shape
    row = lambda width: pl.BlockSpec((tm, width), lambda i: (i, 0))
    return pl.pallas_call(
        _attn_qkv_kernel,
        out_shape=jax.ShapeDtypeStruct((t, QKV_DIM), BF16),
        grid=(t // tm,),
        in_specs=[row(d), _const_spec(g.shape), _const_spec(w.shape), _const_spec(b.shape),
                  row(LANES), row(LANES)],
        out_specs=row(QKV_DIM),
        compiler_params=pltpu.CompilerParams(
            dimension_semantics=("parallel",), vmem_limit_bytes=VMEM_LIMIT),
        name="attn_qkv",
    )(x, g, w, b, cos, sin)


def _attn_core_kernel(q_blocks, q_ref, kp_ref, kc_ref, vp_ref, vc_ref, sink_ref, o_ref):
    n = pl.program_id(1)
    kr = lax.broadcasted_iota(jnp.int32, (2 * WINDOW, WINDOW), 0)
    qc = lax.broadcasted_iota(jnp.int32, (2 * WINDOW, WINDOW), 1)
    own = kr >= WINDOW
    band = (own & (kr - WINDOW <= qc)) | (jnp.logical_not(own) & (kr > qc))
    first_band = band & (own | (n > 0))
    neg = -jnp.inf
    probs = [(i, j) for i in range(q_blocks) for j in range(N_KV_HEADS)]
    blk = lambda i: slice(i * WINDOW, (i + 1) * WINDOW)
    kvl = lambda j: slice(j * HEAD, (j + 1) * HEAD)
    head = lambda j, g: j * GROUP + g

    def keys(prev_ref, cur_ref, i, j):
        prev = prev_ref[0, :, kvl(j)] if i == 0 else cur_ref[0, blk(i - 1), kvl(j)]
        return jnp.concatenate([prev, cur_ref[0, blk(i), kvl(j)]], axis=0)

    q = [jnp.concatenate([q_ref[0, blk(i), head(j, g) * HEAD:(head(j, g) + 1) * HEAD]
                          for g in range(GROUP)], axis=0) for i, j in probs]
    s_t = [_dot_nt(keys(kp_ref, kc_ref, i, j), q[p]) for p, (i, j) in enumerate(probs)]
    p_t = [[] for _ in probs]
    for g in range(GROUP):
        sink = [sink_ref[:, head(j, g):head(j, g) + 1] * LOG2E for _, j in probs]
        x = [jnp.where(first_band if i == 0 else band, s_t[p][:, blk(g)], neg)
             for p, (i, j) in enumerate(probs)]
        m = [jnp.maximum(jnp.max(x[p], axis=0, keepdims=True), sink[p])
             for p in range(len(probs))]
        e = [jnp.exp2(x[p] - m[p]) for p in range(len(probs))]
        den = [jnp.sum(e[p], axis=0, keepdims=True) + jnp.exp2(sink[p] - m[p])
               for p in range(len(probs))]
        for p in range(len(probs)):
            p_t[p].append((e[p] * (1.0 / den[p])).astype(BF16))
    for p, (i, j) in enumerate(probs):
        o = lax.dot_general(jnp.concatenate(p_t[p], axis=1), keys(vp_ref, vc_ref, i, j),
                            (((0,), (0,)), ((), ())), preferred_element_type=F32)
        for g in range(GROUP):
            o_ref[0, blk(i), head(j, g) * HEAD:(head(j, g) + 1) * HEAD] = (
                o[blk(g)].astype(o_ref.dtype))


def _attn_core(qkv, sinks, q_blocks=4):
    b, s, _ = qkv.shape
    nq = N_HEADS * HEAD
    tq = q_blocks * WINDOW
    k_blk = nq // LANES
    v_blk = k_blk + N_KV_HEADS * HEAD // LANES
    q_spec = pl.BlockSpec((1, tq, nq), lambda bi, n: (bi, n, 0))
    cur = lambda blk: pl.BlockSpec((1, tq, LANES), lambda bi, n: (bi, n, blk))
    prev = lambda blk: pl.BlockSpec((1, WINDOW, LANES),
                                    lambda bi, n: (bi, jnp.maximum(n * q_blocks - 1, 0), blk))
    return pl.pallas_call(
        functools.partial(_attn_core_kernel, q_blocks),
        out_shape=jax.ShapeDtypeStruct((b, s, nq), BF16),
        grid=(b, s // tq),
        in_specs=[q_spec, prev(k_blk), cur(k_blk), prev(v_blk), cur(v_blk),
                  _const_spec(sinks.shape)],
        out_specs=q_spec,
        compiler_params=pltpu.CompilerParams(
            dimension_semantics=("parallel", "arbitrary"), vmem_limit_bytes=VMEM_LIMIT),
        name="attn_core",
    )(qkv, qkv, qkv, qkv, qkv, sinks)


def kernel(x, positions, norm_mix_g, norm_mlp_g, norm_final_g, rwkv_mu, rwkv_w_r, rwkv_w_k, rwkv_w_v, rwkv_w_o, rwkv_w0, rwkv_w1, rwkv_w2, rwkv_a0, rwkv_a1, rwkv_a2, rwkv_g1, rwkv_g2, rwkv_k_k, rwkv_k_a, rwkv_r_k, rwkv_ln_w, rwkv_ln_b, attn_w_qkv, attn_b_qkv, attn_sinks, attn_w_o, attn_b_o, mlp_w_in, mlp_w_out):
    b, s, d = x.shape
    t = b * s
    x2d = x.reshape(t, d)
    row = lambda p: p.reshape(1, -1)
    bf = lambda w: w.astype(BF16)
    zero_bias = jnp.zeros((1, d), F32)

    r, k, v, lw, a, gate = _rwkv_pre(
        x2d, s, row(norm_mix_g[0]), rwkv_mu[0], bf(rwkv_w_r[0]), bf(rwkv_w_k[0]),
        bf(rwkv_w_v[0]), row(rwkv_w0[0]), bf(rwkv_w1[0]), bf(rwkv_w2[0]), row(rwkv_a0[0]),
        bf(rwkv_a1[0]), bf(rwkv_a2[0]), bf(rwkv_g1[0]), bf(rwkv_g2[0]))
    seq = lambda z: z.reshape(b, s, d)
    y = _wkv(seq(r), seq(k), seq(v), seq(lw), seq(a), row(rwkv_k_k[0]), row(rwkv_k_a[0]),
             row(rwkv_r_k[0]), row(rwkv_ln_w[0]), row(rwkv_ln_b[0]))
    x2d = _proj_mlp(y.reshape(t, d), gate, x2d, bf(rwkv_w_o[0]), zero_bias,
                    row(norm_mlp_g[0]), bf(mlp_w_in[0]), bf(mlp_w_out[0]), None)

    inv_freq = ROPE_THETA ** (-jnp.arange(0, HEAD, 2, dtype=F32) / HEAD)
    freq_row = jnp.tile(inv_freq, LANES // (HEAD // 2)).reshape(1, LANES)
    cos, sin = _rope_tab(positions.astype(F32).reshape(t, 1), freq_row)
    qkv = _attn_qkv(x2d, row(norm_mix_g[1]), bf(attn_w_qkv[0]), row(attn_b_qkv[0]), cos, sin)
    o = _attn_core(qkv.reshape(b, s, QKV_DIM), row(attn_sinks[0]))
    out = _proj_mlp(o.reshape(t, d), None, x2d, bf(attn_w_o[0]), row(attn_b_o[0]),
                    row(norm_mlp_g[1]), bf(mlp_w_in[1]), bf(mlp_w_out[1]), row(norm_final_g))
    return out.reshape(b, s, d)
```

```python
import functools
import math

import jax
import jax.numpy as jnp
from jax import lax
from jax.experimental import pallas as pl
from jax.experimental.pallas import tpu as pltpu

F32 = jnp.float32
BF16 = jnp.bfloat16

D_MODEL = 1024
HEAD = 64
N_HEADS = D_MODEL // HEAD
N_KV_HEADS = 2
GROUP = N_HEADS // N_KV_HEADS
WINDOW = 128
QKV_DIM = (N_HEADS + 2 * N_KV_HEADS) * HEAD
D_FF = 4 * D_MODEL
ROPE_THETA = 10000.0
RMS_EPS = 1e-5
GN_EPS = 64e-5

LANES = 128
LOG2E = math.log2(math.e)
Q_SCALE = HEAD ** -0.5 * LOG2E
CHUNK = 64
VMEM_LIMIT = 56 * 1024 * 1024


def _rms(x, g):
    return x * lax.rsqrt(jnp.mean(x * x, axis=-1, keepdims=True) + RMS_EPS) * g


def _dot(a, b):
    return jnp.dot(a.astype(BF16), b.astype(BF16), preferred_element_type=F32)


def _dot_nt(a, b):
    return lax.dot_general(a.astype(BF16), b.astype(BF16), (((1,), (1,)), ((), ())),
                           preferred_element_type=F32)


def _dot_tn(a, b):
    return lax.dot_general(a.astype(BF16), b.astype(BF16), (((0,), (0,)), ((), ())),
                           preferred_element_type=F32)


def _const_spec(shape):
    nd = len(shape)
    return pl.BlockSpec(shape, lambda *_: (0,) * nd)


def _rwkv_pre_kernel(seq_tiles, x_ref, xp_ref, g_ref, mu_ref, wr_ref, wk_ref, wv_ref,
                     w0_ref, w1_ref, w2_ref, a0_ref, a1_ref, a2_ref, g1_ref, g2_ref,
                     r_out, k_out, v_out, lw_out, a_out, g_out):
    i = pl.program_id(0)
    g = g_ref[...]
    h = _rms(x_ref[...], g)
    hp = _rms(xp_ref[...], g)
    last = jnp.where(i % seq_tiles == 0, 0.0, hp[7:8, :])
    row = lax.broadcasted_iota(jnp.int32, h.shape, 0)
    h_prev = jnp.where(row == 0, last, pltpu.roll(h, 1, axis=0))
    dx = h_prev - h
    mu = mu_ref[...]
    xr = h + dx * mu[0:1]
    xw = h + dx * mu[1:2]
    xk = h + dx * mu[2:3]
    xv = h + dx * mu[3:4]
    xa = h + dx * mu[4:5]
    xg = h + dx * mu[5:6]
    r_out[...] = _dot(xr, wr_ref[...])
    k_out[...] = _dot(xk, wk_ref[...])
    v_out[...] = _dot(xv, wv_ref[...])
    w_pre = w0_ref[...] + _dot(jnp.tanh(_dot(xw, w1_ref[...])), w2_ref[...])
    lw_out[...] = jax.nn.sigmoid(w_pre) * (-math.exp(-0.5))
    a_out[...] = jax.nn.sigmoid(a0_ref[...] + _dot(_dot(xa, a1_ref[...]), a2_ref[...]))
    g_out[...] = _dot(jax.nn.sigmoid(_dot(xg, g1_ref[...])), g2_ref[...])


def _rwkv_pre(x2d, seq_len, g, mu, wr, wk, wv, w0, w1, w2, a0, a1, a2, g1, g2, tm=256):
    t, d = x2d.shape
    seq_tiles = seq_len // tm
    row_spec = pl.BlockSpec((tm, d), lambda i: (i, 0))
    prev_spec = pl.BlockSpec((8, d), lambda i: (jnp.maximum(i * (tm // 8) - 1, 0), 0))
    consts = (g, mu, wr, wk, wv, w0, w1, w2, a0, a1, a2, g1, g2)
    out = jax.ShapeDtypeStruct((t, d), F32)
    return pl.pallas_call(
        functools.partial(_rwkv_pre_kernel, seq_tiles),
        out_shape=(out,) * 6,
        grid=(t // tm,),
        in_specs=[row_spec, prev_spec] + [_const_spec(c.shape) for c in consts],
        out_specs=(row_spec,) * 6,
        compiler_params=pltpu.CompilerParams(
            dimension_semantics=("parallel",), vmem_limit_bytes=VMEM_LIMIT),
        name="rwkv_pre",
    )(x2d, x2d, *consts)


def _cumsum_rows(x):
    n = x.shape[0]
    row = lax.broadcasted_iota(jnp.int32, x.shape, 0)
    s = 1
    while s < n:
        x = x + jnp.where(row >= s, pltpu.roll(x, s, axis=0), 0.0)
        s *= 2
    return x


INV_BASE = 8
GROUP_HEADS = 4
GROUP_W = GROUP_HEADS * HEAD


def _head_sum(x):
    low = lax.broadcasted_iota(jnp.int32, (x.shape[0], LANES), 1) < HEAD
    outs = []
    for c in range(x.shape[1] // LANES):
        xc = x[:, c * LANES:(c + 1) * LANES]
        s_lo = jnp.sum(jnp.where(low, xc, 0.0), axis=-1, keepdims=True)
        s_hi = jnp.sum(jnp.where(low, 0.0, xc), axis=-1, keepdims=True)
        outs.append(jnp.where(low, s_lo, s_hi))
    return jnp.concatenate(outs, axis=1)


def _wkv_kernel(chunks, r_ref, k_ref, v_ref, lw_ref, a_ref,
                kk_ref, ka_ref, rk_ref, lnw_ref, lnb_ref, o_ref, state_ref):
    n_groups = D_MODEL // GROUP_W
    gs = range(n_groups)

    @pl.when(pl.program_id(1) == 0)
    def _():
        state_ref[...] = jnp.zeros_like(state_ref)

    ri = lax.broadcasted_iota(jnp.int32, (CHUNK, GROUP_W), 0)
    lane = lax.broadcasted_iota(jnp.int32, (CHUNK, GROUP_W), 1)
    cj = lane & (HEAD - 1)
    lane_head = lax.shift_right_logical(lane, HEAD.bit_length() - 1)
    head_masks = [lane_head == h for h in range(GROUP_HEADS)]
    blk = lambda idx, size: lax.shift_right_logical(idx, size.bit_length() - 1)
    strict = cj < ri
    incl = cj <= ri
    eye = (cj == ri).astype(F32)
    diag = blk(ri, INV_BASE) == blk(cj, INV_BASE)
    bands = []
    size = INV_BASE
    while size < CHUNK:
        bands.append((blk(ri, 2 * size) == blk(cj, 2 * size))
                     & (blk(ri, size) != blk(cj, size)))
        size *= 2
    sr = lax.broadcasted_iota(jnp.int32, (GROUP_W, GROUP_W), 0)
    sc = lax.broadcasted_iota(jnp.int32, (GROUP_W, GROUP_W), 1)
    state_mask = blk(sr, HEAD) == blk(sc, HEAD)

    def bd(y):
        return jnp.concatenate([jnp.where(m, y, 0.0) for m in head_masks], axis=0).astype(BF16)

    def mm(x, w):
        return jnp.dot(x.astype(BF16), w, preferred_element_type=F32)

    def mm_nt(x, w):
        return lax.dot_general(x.astype(BF16), w, (((1,), (1,)), ((), ())),
                               preferred_element_type=F32)

    cols = [slice(g * GROUP_W, (g + 1) * GROUP_W) for g in gs]
    rows_of = lambda c: slice(c * CHUNK, (c + 1) * CHUNK)


    def prepare(c, out):
        rows = rows_of(c)
        lw = lw_ref[0, rows, :]
        linc = _cumsum_rows(lw)
        yield
        p_inc = jnp.exp(linc)
        p_exc = jnp.exp(linc - lw)
        p_inv = jnp.exp(-linc)
        yield
        k = k_ref[0, rows, :]
        a = a_ref[0, rows, :]
        kk = k * kk_ref[...]
        kk = kk * lax.rsqrt(jnp.maximum(_head_sum(kk * kk), 1e-24))
        yield
        kmod = k * (1.0 + (a - 1.0) * ka_ref[...])
        at_f = (-kk * p_exc).astype(BF16)
        rt_f = (r_ref[0, rows, :] * p_inc).astype(BF16)
        yield
        out["lhs"] = [jnp.concatenate([at_f[:, cl], rt_f[:, cl]], axis=0) for cl in cols]
        out["bt"] = kk * a * p_inv
        out["kt"] = kmod * p_inv
        out["kmod"] = kmod
        out["p_end"] = p_inc[CHUNK - 1:CHUNK, :]
        yield

    def matmuls(c, ops, out):
        v = v_ref[0, rows_of(c), :]
        lhs = ops["lhs"]
        bt = [ops["bt"][:, cl] for cl in cols]
        kt = [ops["kt"][:, cl] for cl in cols]
        vg = [v[:, cl] for cl in cols]
        ab = [mm_nt(lhs[g], bd(bt[g])) for g in gs]
        ak = [mm_nt(lhs[g], bd(kt[g])) for g in gs]
        yield
        a_ab = [jnp.where(strict, x[:CHUNK], 0.0) for x in ab]
        a_rb = [jnp.where(incl, x[CHUNK:], 0.0) for x in ab]
        a_k = [jnp.concatenate([jnp.where(strict, x[:CHUNK], 0.0),
                                jnp.where(incl, x[CHUNK:], 0.0)], axis=0) for x in ak]
        d1 = [jnp.where(diag, x, 0.0) for x in a_ab]
        d2 = [mm(d1[g], bd(d1[g])) for g in gs]
        yield
        t = [eye + x for x in d1]
        td = [mm(jnp.concatenate([t[g], d2[g]], axis=0), bd(d2[g])) for g in gs]
        yield
        t = [t[g] + td[g][:CHUNK] for g in gs]
        t = [t[g] + mm(t[g], bd(td[g][CHUNK:])) for g in gs]
        yield
        for band in bands:
            te = [mm(t[g], bd(jnp.where(band, a_ab[g], 0.0))) for g in gs]
            yield
            t = [t[g] + mm(te[g], bd(t[g])) for g in gs]
            yield
        s0 = [state_ref[g] for g in gs]
        zy = [mm_nt(lhs[g], s0[g].astype(BF16)) for g in gs]
        av = [mm(a_k[g], bd(vg[g])) for g in gs]
        yield
        u = [mm(t[g], bd(zy[g][:CHUNK] + av[g][:CHUNK])) for g in gs]
        yield
        y = [zy[g][CHUNK:] + av[g][CHUNK:] + mm(a_rb[g], bd(u[g])) for g in gs]
        for g in gs:
            uv = jnp.concatenate([u[g], vg[g]], axis=0)
            bk = jnp.concatenate([bt[g], kt[g]], axis=0)
            state_ref[g] = (jnp.where(state_mask, s0[g] + _dot_tn(uv, bk), 0.0)
                            * ops["p_end"][:, cols[g]])
        out["y"] = jnp.concatenate(y, axis=1)
        yield

    def finish(c, ops, y_f):
        rows = rows_of(c)
        inv_n = 1.0 / HEAD
        yc = y_f - _head_sum(y_f) * inv_n
        yield
        var = _head_sum(yc * yc) * inv_n
        yn = yc * lax.rsqrt(var + GN_EPS) * lnw_ref[...] + lnb_ref[...]
        yield
        bonus = _head_sum(r_ref[0, rows, :] * ops["kmod"] * rk_ref[...])
        o_ref[0, rows, :] = yn + bonus * v_ref[0, rows, :]
        yield

    def drain(gen):
        for _ in gen:
            pass

    ops = [dict() for _ in range(chunks)]
    res = [dict() for _ in range(chunks)]
    drain(prepare(0, ops[0]))
    for c in range(chunks):
        fillers = []
        if c + 1 < chunks:
            fillers.append(prepare(c + 1, ops[c + 1]))
        if c > 0:
            fillers.append(finish(c - 1, ops[c - 1], res[c - 1]["y"]))
        for _ in matmuls(c, ops[c], res[c]):
            for f in fillers:
                next(f, None)
        for f in fillers:
            drain(f)
    drain(finish(chunks - 1, ops[chunks - 1], res[chunks - 1]["y"]))


def _wkv(r, k, v, lw, a, kk_p, ka_p, rk_p, lnw_p, lnb_p, rows=512):
    b, s, d = r.shape
    seq_spec = pl.BlockSpec((1, rows, d), lambda bi, ci: (bi, ci, 0))
    par_spec = pl.BlockSpec((1, d), lambda bi, ci: (0, 0))
    return pl.pallas_call(
        functools.partial(_wkv_kernel, rows // CHUNK),
        out_shape=jax.ShapeDtypeStruct((b, s, d), F32),
        grid=(b, s // rows),
        in_specs=[seq_spec] * 5 + [par_spec] * 5,
        out_specs=seq_spec,
        scratch_shapes=[pltpu.VMEM((d // GROUP_W, GROUP_W, GROUP_W), F32)],
        compiler_params=pltpu.CompilerParams(
            dimension_semantics=("parallel", "arbitrary"), vmem_limit_bytes=VMEM_LIMIT),
        name="wkv",
    )(r, k, v, lw, a, kk_p, ka_p, rk_p, lnw_p, lnb_p)


def _proj_mlp_kernel(has_gate, final_norm, ff_chunk, *refs):
    refs = list(refs)
    y_ref = refs.pop(0)
    gate_ref = refs.pop(0) if has_gate else None
    x_ref, wo_ref, bo_ref, gm_ref, win_ref, wout_ref = refs[:6]
    refs = refs[6:]
    gf_ref = refs.pop(0) if final_norm else None
    o_ref = refs.pop(0)

    y = y_ref[...].astype(F32)
    if has_gate:
        y = y * gate_ref[...]
    x1 = x_ref[...] + _dot(y, wo_ref[...]) + bo_ref[...]
    hb = _rms(x1, gm_ref[...]).astype(BF16)
    acc = x1
    for c in range(D_FF // ff_chunk):
        cols = slice(c * ff_chunk, (c + 1) * ff_chunk)
        hid = jnp.maximum(jnp.dot(hb, win_ref[:, cols], preferred_element_type=F32), 0.0)
        acc = acc + _dot(hid * hid, wout_ref[cols, :])
    if final_norm:
        acc = _rms(acc, gf_ref[...])
    o_ref[...] = acc


def _proj_mlp(y, gate, x, wo, bo, gm, win, wout, gf, tm=256, ff_chunk=1024):
    t, d = x.shape
    row_spec = pl.BlockSpec((tm, d), lambda i: (i, 0))
    args = [y] + ([gate] if gate is not None else []) + [x, wo, bo, gm, win, wout]
    specs = [row_spec] * (len(args) - 5) + [_const_spec(c.shape) for c in args[-5:]]
    if gf is not None:
        args.append(gf)
        specs.append(_const_spec(gf.shape))
    return pl.pallas_call(
        functools.partial(_proj_mlp_kernel, gate is not None, gf is not None, ff_chunk),
        out_shape=jax.ShapeDtypeStruct((t, d), F32),
        grid=(t // tm,),
        in_specs=specs,
        out_specs=row_spec,
        compiler_params=pltpu.CompilerParams(
            dimension_semantics=("parallel",), vmem_limit_bytes=VMEM_LIMIT),
        name="proj_mlp",
    )(*args)


def _rope_tab_kernel(pos_ref, freq_ref, cos_ref, sin_ref):
    ang = pos_ref[...] * freq_ref[...]
    lane = lax.broadcasted_iota(jnp.int32, ang.shape, 1)
    cos_ref[...] = jnp.cos(ang)
    sin_ref[...] = jnp.where(lane % HEAD < HEAD // 2, -1.0, 1.0) * jnp.sin(ang)


def _rope_tab(pos_col, freq_row, tm=1024):
    t = pos_col.shape[0]
    out = jax.ShapeDtypeStruct((t, LANES), F32)
    spec = pl.BlockSpec((tm, LANES), lambda i: (i, 0))
    return pl.pallas_call(
        _rope_tab_kernel,
        out_shape=(out, out),
        grid=(t // tm,),
        in_specs=[pl.BlockSpec((tm, 1), lambda i: (i, 0)), _const_spec(freq_row.shape)],
        out_specs=(spec, spec),
        compiler_params=pltpu.CompilerParams(dimension_semantics=("parallel",)),
        name="rope_tab",
    )(pos_col, freq_row)


def _attn_qkv_kernel(x_ref, g_ref, w_ref, b_ref, cos_ref, sin_ref, o_ref):
    h = _rms(x_ref[...], g_ref[...])
    qkv = _dot(h, w_ref[...]) + b_ref[...]
    cos = cos_ref[...]
    sin = sin_ref[...]
    lane = lax.broadcasted_iota(jnp.int32, cos.shape, 1)
    first_half = lane % HEAD < HEAD // 2
    n_q = N_HEADS * HEAD // LANES
    n_rot = (N_HEADS + N_KV_HEADS) * HEAD // LANES
    for j in range(QKV_DIM // LANES):
        cols = slice(j * LANES, (j + 1) * LANES)
        blk = qkv[:, cols]
        if j < n_rot:
            rot = jnp.where(first_half, pltpu.roll(blk, LANES - HEAD // 2, axis=1),
                            pltpu.roll(blk, HEAD // 2, axis=1))
            blk = blk * cos + rot * sin
        if j < n_q:
            blk = blk * Q_SCALE
        o_ref[:, cols] = blk.astype(o_ref.dtype)


def _attn_qkv(x, g, w, b, cos, sin, tm=256):
    t, d = x.shape
    row = lambda width: pl.BlockSpec((tm, width), lambda i: (i, 0))
    return pl.pallas_call(
        _attn_qkv_kernel,
        out_shape=jax.ShapeDtypeStruct((t, QKV_DIM), BF16),
        grid=(t // tm,),
        in_specs=[row(d), _const_spec(g.shape), _const_spec(w.shape), _const_spec(b.shape),
                  row(LANES), row(LANES)],
        out_specs=row(QKV_DIM),
        compiler_params=pltpu.CompilerParams(
            dimension_semantics=("parallel",), vmem_limit_bytes=VMEM_LIMIT),
        name="attn_qkv",
    )(x, g, w, b, cos, sin)


def _attn_core_kernel(q_blocks, q_ref, kp_ref, kc_ref, vp_ref, vc_ref, sink_ref, o_ref):
    n = pl.program_id(1)
    kr = lax.broadcasted_iota(jnp.int32, (2 * WINDOW, WINDOW), 0)
    qc = lax.broadcasted_iota(jnp.int32, (2 * WINDOW, WINDOW), 1)
    own = kr >= WINDOW
    band = (own & (kr - WINDOW <= qc)) | (jnp.logical_not(own) & (kr > qc))
    first_band = band & (own | (n > 0))
    neg = -jnp.inf
    probs = [(i, j) for i in range(q_blocks) for j in range(N_KV_HEADS)]
    blk = lambda i: slice(i * WINDOW, (i + 1) * WINDOW)
    kvl = lambda j: slice(j * HEAD, (j + 1) * HEAD)
    head = lambda j, g: j * GROUP + g

    def keys(prev_ref, cur_ref, i, j):
        prev = prev_ref[0, :, kvl(j)] if i == 0 else cur_ref[0, blk(i - 1), kvl(j)]
        return jnp.concatenate([prev, cur_ref[0, blk(i), kvl(j)]], axis=0)

    q = [jnp.concatenate([q_ref[0, blk(i), head(j, g) * HEAD:(head(j, g) + 1) * HEAD]
                          for g in range(GROUP)], axis=0) for i, j in probs]
    s_t = [_dot_nt(keys(kp_ref, kc_ref, i, j), q[p]) for p, (i, j) in enumerate(probs)]
    p_t = [[] for _ in probs]
    for g in range(GROUP):
        sink = [sink_ref[:, head(j, g):head(j, g) + 1] * LOG2E for _, j in probs]
        x = [jnp.where(first_band if i == 0 else band, s_t[p][:, blk(g)], neg)
             for p, (i, j) in enumerate(probs)]
        m = [jnp.maximum(jnp.max(x[p], axis=0, keepdims=True), sink[p])
             for p in range(len(probs))]
        e = [jnp.exp2(x[p] - m[p]) for p in range(len(probs))]
        den = [jnp.sum(e[p], axis=0, keepdims=True) + jnp.exp2(sink[p] - m[p])
               for p in range(len(probs))]
        for p in range(len(probs)):
            p_t[p].append((e[p] * (1.0 / den[p])).astype(BF16))
    for p, (i, j) in enumerate(probs):
        o = lax.dot_general(jnp.concatenate(p_t[p], axis=1), keys(vp_ref, vc_ref, i, j),
                            (((0,), (0,)), ((), ())), preferred_element_type=F32)
        for g in range(GROUP):
            o_ref[0, blk(i), head(j, g) * HEAD:(head(j, g) + 1) * HEAD] = (
                o[blk(g)].astype(o_ref.dtype))


def _attn_core(qkv, sinks, q_blocks=4):
    b, s, _ = qkv.shape
    nq = N_HEADS * HEAD
    tq = q_blocks * WINDOW
    k_blk = nq // LANES
    v_blk = k_blk + N_KV_HEADS * HEAD // LANES
    q_spec = pl.BlockSpec((1, tq, nq), lambda bi, n: (bi, n, 0))
    cur = lambda blk: pl.BlockSpec((1, tq, LANES), lambda bi, n: (bi, n, blk))
    prev = lambda blk: pl.BlockSpec((1, WINDOW, LANES),
                                    lambda bi, n: (bi, jnp.maximum(n * q_blocks - 1, 0), blk))
    return pl.pallas_call(
        functools.partial(_attn_core_kernel, q_blocks),
        out_shape=jax.ShapeDtypeStruct((b, s, nq), BF16),
        grid=(b, s // tq),
        in_specs=[q_spec, prev(k_blk), cur(k_blk), prev(v_blk), cur(v_blk),
                  _const_spec(sinks.shape)],
        out_specs=q_spec,
        compiler_params=pltpu.CompilerParams(
            dimension_semantics=("parallel", "arbitrary"), vmem_limit_bytes=VMEM_LIMIT),
        name="attn_core",
    )(qkv, qkv, qkv, qkv, qkv, sinks)


def kernel(x, positions, norm_mix_g, norm_mlp_g, norm_final_g, rwkv_mu, rwkv_w_r, rwkv_w_k, rwkv_w_v, rwkv_w_o, rwkv_w0, rwkv_w1, rwkv_w2, rwkv_a0, rwkv_a1, rwkv_a2, rwkv_g1, rwkv_g2, rwkv_k_k, rwkv_k_a, rwkv_r_k, rwkv_ln_w, rwkv_ln_b, attn_w_qkv, attn_b_qkv, attn_sinks, attn_w_o, attn_b_o, mlp_w_in, mlp_w_out):
    b, s, d = x.shape
    t = b * s
    x2d = x.reshape(t, d)
    row = lambda p: p.reshape(1, -1)
    bf = lambda w: w.astype(BF16)
    zero_bias = jnp.zeros((1, d), F32)

    r, k, v, lw, a, gate = _rwkv_pre(
        x2d, s, row(norm_mix_g[0]), rwkv_mu[0], bf(rwkv_w_r[0]), bf(rwkv_w_k[0]),
        bf(rwkv_w_v[0]), row(rwkv_w0[0]), bf(rwkv_w1[0]), bf(rwkv_w2[0]), row(rwkv_a0[0]),
        bf(rwkv_a1[0]), bf(rwkv_a2[0]), bf(rwkv_g1[0]), bf(rwkv_g2[0]))
    seq = lambda z: z.reshape(b, s, d)
    y = _wkv(seq(r), seq(k), seq(v), seq(lw), seq(a), row(rwkv_k_k[0]), row(rwkv_k_a[0]),
             row(rwkv_r_k[0]), row(rwkv_ln_w[0]), row(rwkv_ln_b[0]))
    x2d = _proj_mlp(y.reshape(t, d), gate, x2d, bf(rwkv_w_o[0]), zero_bias,
                    row(norm_mlp_g[0]), bf(mlp_w_in[0]), bf(mlp_w_out[0]), None)

    inv_freq = ROPE_THETA ** (-jnp.arange(0, HEAD, 2, dtype=F32) / HEAD)
    freq_row = jnp.tile(inv_freq, LANES // (HEAD // 2)).reshape(1, LANES)
    cos, sin = _rope_tab(positions.astype(F32).reshape(t, 1), freq_row)
    qkv = _attn_qkv(x2d, row(norm_mix_g[1]), bf(attn_w_qkv[0]), row(attn_b_qkv[0]), cos, sin)
    o = _attn_core(qkv.reshape(b, s, QKV_DIM), row(attn_sinks[0]))
    out = _proj_mlp(o.reshape(t, d), None, x2d, bf(attn_w_o[0]), row(attn_b_o[0]),
                    row(norm_mlp_g[1]), bf(mlp_w_in[1]), bf(mlp_w_out[1]), row(norm_final_g))
    return out.reshape(b, s, d)
```

```python
import functools
import math

import jax
import jax.numpy as jnp
from jax import lax
from jax.experimental import pallas as pl
from jax.experimental.pallas import tpu as pltpu

F32 = jnp.float32
BF16 = jnp.bfloat16

D_MODEL = 1024
HEAD = 64
N_HEADS = D_MODEL // HEAD
N_KV_HEADS = 2
GROUP = N_HEADS // N_KV_HEADS
WINDOW = 128
QKV_DIM = (N_HEADS + 2 * N_KV_HEADS) * HEAD
D_FF = 4 * D_MODEL
ROPE_THETA = 10000.0
RMS_EPS = 1e-5
GN_EPS = 64e-5

LANES = 128
LOG2E = math.log2(math.e)
Q_SCALE = HEAD ** -0.5 * LOG2E
CHUNK = 64
VMEM_LIMIT = 56 * 1024 * 1024


def _rms(x, g):
    return x * lax.rsqrt(jnp.mean(x * x, axis=-1, keepdims=True) + RMS_EPS) * g


def _dot(a, b):
    return jnp.dot(a.astype(BF16), b.astype(BF16), preferred_element_type=F32)


def _dot_nt(a, b):
    return lax.dot_general(a.astype(BF16), b.astype(BF16), (((1,), (1,)), ((), ())),
                           preferred_element_type=F32)


def _dot_tn(a, b):
    return lax.dot_general(a.astype(BF16), b.astype(BF16), (((0,), (0,)), ((), ())),
                           preferred_element_type=F32)


def _const_spec(shape):
    nd = len(shape)
    return pl.BlockSpec(shape, lambda *_: (0,) * nd)


def _drain(gen):
    for _ in gen:
        pass


def _interleave(main, fillers):
    for _ in main:
        for f in fillers:
            next(f, None)
    for f in fillers:
        _drain(f)


def _rwkv_pre_kernel(seq_tiles, sub, x_ref, xp_ref, g_ref, mu_ref, wr_ref, wk_ref, wv_ref,
                     w0_ref, w1_ref, w2_ref, a0_ref, a1_ref, a2_ref, g1_ref, g2_ref,
                     r_out, k_out, v_out, lw_out, a_out, g_out):
    i = pl.program_id(0)
    n_sub = x_ref.shape[0] // sub
    g = g_ref[...]
    mu = mu_ref[...]
    rows_of = lambda j: slice(j * sub, (j + 1) * sub)
    hp = _rms(xp_ref[...], g)
    carry_row = {0: jnp.where(i % seq_tiles == 0, 0.0, hp[7:8, :])}

    def mix(j, out):
        h = _rms(x_ref[rows_of(j), :], g)
        carry_row[j + 1] = h[sub - 1:sub, :]
        yield
        row = lax.broadcasted_iota(jnp.int32, h.shape, 0)
        dx = jnp.where(row == 0, carry_row[j], pltpu.roll(h, 1, axis=0)) - h
        yield
        for n, name in enumerate(("r", "w", "k", "v", "a", "g")):
            out[name] = (h + dx * mu[n:n + 1]).astype(BF16)
            if n % 2 == 1:
                yield

    def project(j, xs):
        rows = rows_of(j)
        dw = _dot(xs["w"], w1_ref[...])
        da = _dot(xs["a"], a1_ref[...])
        dg = _dot(xs["g"], g1_ref[...])
        yield
        r_out[rows, :] = jnp.dot(xs["r"], wr_ref[...], preferred_element_type=F32)
        yield
        w_pre = w0_ref[...] + _dot(jnp.tanh(dw), w2_ref[...])
        a_pre = a0_ref[...] + _dot(da, a2_ref[...])
        g_out[rows, :] = _dot(jax.nn.sigmoid(dg), g2_ref[...])
        yield
        k_out[rows, :] = jnp.dot(xs["k"], wk_ref[...], preferred_element_type=F32)
        yield
        lw_out[rows, :] = jax.nn.sigmoid(w_pre) * (-math.exp(-0.5))
        a_out[rows, :] = jax.nn.sigmoid(a_pre)
        yield
        v_out[rows, :] = jnp.dot(xs["v"], wv_ref[...], preferred_element_type=F32)
        yield

    xs = [dict() for _ in range(n_sub)]
    _drain(mix(0, xs[0]))
    for j in range(n_sub):
        fillers = [mix(j + 1, xs[j + 1])] if j + 1 < n_sub else []
        _interleave(project(j, xs[j]), fillers)


def _rwkv_pre(x2d, seq_len, g, mu, wr, wk, wv, w0, w1, w2, a0, a1, a2, g1, g2, tm=512, sub=256):
    t, d = x2d.shape
    seq_tiles = seq_len // tm
    row_spec = pl.BlockSpec((tm, d), lambda i: (i, 0))
    prev_spec = pl.BlockSpec((8, d), lambda i: (jnp.maximum(i * (tm // 8) - 1, 0), 0))
    consts = (g, mu, wr, wk, wv, w0, w1, w2, a0, a1, a2, g1, g2)
    out = jax.ShapeDtypeStruct((t, d), F32)
    return pl.pallas_call(
        functools.partial(_rwkv_pre_kernel, seq_tiles, sub),
        out_shape=(out,) * 6,
        grid=(t // tm,),
        in_specs=[row_spec, prev_spec] + [_const_spec(c.shape) for c in consts],
        out_specs=(row_spec,) * 6,
        compiler_params=pltpu.CompilerParams(
            dimension_semantics=("parallel",), vmem_limit_bytes=VMEM_LIMIT),
        name="rwkv_pre",
    )(x2d, x2d, *consts)


def _cumsum_rows(x):
    n = x.shape[0]
    row = lax.broadcasted_iota(jnp.int32, x.shape, 0)
    s = 1
    while s < n:
        x = x + jnp.where(row >= s, pltpu.roll(x, s, axis=0), 0.0)
        s *= 2
    return x


INV_BASE = 8
GROUP_HEADS = 4
GROUP_W = GROUP_HEADS * HEAD


def _head_sum(x):
    low = lax.broadcasted_iota(jnp.int32, (x.shape[0], LANES), 1) < HEAD
    outs = []
    for c in range(x.shape[1] // LANES):
        xc = x[:, c * LANES:(c + 1) * LANES]
        s_lo = jnp.sum(jnp.where(low, xc, 0.0), axis=-1, keepdims=True)
        s_hi = jnp.sum(jnp.where(low, 0.0, xc), axis=-1, keepdims=True)
        outs.append(jnp.where(low, s_lo, s_hi))
    return jnp.concatenate(outs, axis=1)


def _wkv_kernel(chunks, r_ref, k_ref, v_ref, lw_ref, a_ref,
                kk_ref, ka_ref, rk_ref, lnw_ref, lnb_ref, o_ref, state_ref):
    n_groups = D_MODEL // GROUP_W
    gs = range(n_groups)

    @pl.when(pl.program_id(1) == 0)
    def _():
        state_ref[...] = jnp.zeros_like(state_ref)

    ri = lax.broadcasted_iota(jnp.int32, (CHUNK, GROUP_W), 0)
    lane = lax.broadcasted_iota(jnp.int32, (CHUNK, GROUP_W), 1)
    cj = lane & (HEAD - 1)
    lane_head = lax.shift_right_logical(lane, HEAD.bit_length() - 1)
    head_masks = [lane_head == h for h in range(GROUP_HEADS)]
    blk = lambda idx, size: lax.shift_right_logical(idx, size.bit_length() - 1)
    strict = cj < ri
    incl = cj <= ri
    eye = (cj == ri).astype(F32)
    diag = blk(ri, INV_BASE) == blk(cj, INV_BASE)
    bands = []
    size = INV_BASE
    while size < CHUNK:
        bands.append((blk(ri, 2 * size) == blk(cj, 2 * size))
                     & (blk(ri, size) != blk(cj, size)))
        size *= 2
    sr = lax.broadcasted_iota(jnp.int32, (GROUP_W, GROUP_W), 0)
    sc = lax.broadcasted_iota(jnp.int32, (GROUP_W, GROUP_W), 1)
    state_mask = blk(sr, HEAD) == blk(sc, HEAD)

    def bd(y):
        return jnp.concatenate([jnp.where(m, y, 0.0) for m in head_masks], axis=0).astype(BF16)

    def mm(x, w):
        return jnp.dot(x.astype(BF16), w, preferred_element_type=F32)

    def mm_nt(x, w):
        return lax.dot_general(x.astype(BF16), w, (((1,), (1,)), ((), ())),
                               preferred_element_type=F32)

    cols = [slice(g * GROUP_W, (g + 1) * GROUP_W) for g in gs]
    rows_of = lambda c: slice(c * CHUNK, (c + 1) * CHUNK)


    def prepare(c, out):
        rows = rows_of(c)
        lw = lw_ref[0, rows, :]
        linc = _cumsum_rows(lw)
        yield
        p_inc = jnp.exp(linc)
        p_exc = jnp.exp(linc - lw)
        p_inv = jnp.exp(-linc)
        yield
        k = k_ref[0, rows, :]
        a = a_ref[0, rows, :]
        kk = k * kk_ref[...]
        kk = kk * lax.rsqrt(jnp.maximum(_head_sum(kk * kk), 1e-24))
        yield
        kmod = k * (1.0 + (a - 1.0) * ka_ref[...])
        at_f = (-kk * p_exc).astype(BF16)
        rt_f = (r_ref[0, rows, :] * p_inc).astype(BF16)
        yield
        out["lhs"] = [jnp.concatenate([at_f[:, cl], rt_f[:, cl]], axis=0) for cl in cols]
        out["bt"] = kk * a * p_inv
        out["kt"] = kmod * p_inv
        out["kmod"] = kmod
        out["p_end"] = p_inc[CHUNK - 1:CHUNK, :]
        yield

    def matmuls(c, ops, out):
        v = v_ref[0, rows_of(c), :]
        lhs = ops["lhs"]
        bt = [ops["bt"][:, cl] for cl in cols]
        kt = [ops["kt"][:, cl] for cl in cols]
        vg = [v[:, cl] for cl in cols]
        ab = [mm_nt(lhs[g], bd(bt[g])) for g in gs]
        ak = [mm_nt(lhs[g], bd(kt[g])) for g in gs]
        yield
        a_ab = [jnp.where(strict, x[:CHUNK], 0.0) for x in ab]
        a_rb = [jnp.where(incl, x[CHUNK:], 0.0) for x in ab]
        a_k = [jnp.concatenate([jnp.where(strict, x[:CHUNK], 0.0),
                                jnp.where(incl, x[CHUNK:], 0.0)], axis=0) for x in ak]
        d1 = [jnp.where(diag, x, 0.0) for x in a_ab]
        d2 = [mm(d1[g], bd(d1[g])) for g in gs]
        yield
        t = [eye + x for x in d1]
        td = [mm(jnp.concatenate([t[g], d2[g]], axis=0), bd(d2[g])) for g in gs]
        yield
        t = [t[g] + td[g][:CHUNK] for g in gs]
        t = [t[g] + mm(t[g], bd(td[g][CHUNK:])) for g in gs]
        yield
        for band in bands:
            te = [mm(t[g], bd(jnp.where(band, a_ab[g], 0.0))) for g in gs]
            yield
            t = [t[g] + mm(te[g], bd(t[g])) for g in gs]
            yield
        s0 = [state_ref[g] for g in gs]
        zy = [mm_nt(lhs[g], s0[g].astype(BF16)) for g in gs]
        av = [mm(a_k[g], bd(vg[g])) for g in gs]
        yield
        u = [mm(t[g], bd(zy[g][:CHUNK] + av[g][:CHUNK])) for g in gs]
        yield
        y = [zy[g][CHUNK:] + av[g][CHUNK:] + mm(a_rb[g], bd(u[g])) for g in gs]
        for g in gs:
            uv = jnp.concatenate([u[g], vg[g]], axis=0)
            bk = jnp.concatenate([bt[g], kt[g]], axis=0)
            state_ref[g] = (jnp.where(state_mask, s0[g] + _dot_tn(uv, bk), 0.0)
                            * ops["p_end"][:, cols[g]])
        out["y"] = jnp.concatenate(y, axis=1)
        yield

    def finish(c, ops, y_f):
        rows = rows_of(c)
        inv_n = 1.0 / HEAD
        yc = y_f - _head_sum(y_f) * inv_n
        yield
        var = _head_sum(yc * yc) * inv_n
        yn = yc * lax.rsqrt(var + GN_EPS) * lnw_ref[...] + lnb_ref[...]
        yield
        bonus = _head_sum(r_ref[0, rows, :] * ops["kmod"] * rk_ref[...])
        o_ref[0, rows, :] = yn + bonus * v_ref[0, rows, :]
        yield

    ops = [dict() for _ in range(chunks)]
    res = [dict() for _ in range(chunks)]
    _drain(prepare(0, ops[0]))
    for c in range(chunks):
        fillers = []
        if c + 1 < chunks:
            fillers.append(prepare(c + 1, ops[c + 1]))
        if c > 0:
            fillers.append(finish(c - 1, ops[c - 1], res[c - 1]["y"]))
        _interleave(matmuls(c, ops[c], res[c]), fillers)
    _drain(finish(chunks - 1, ops[chunks - 1], res[chunks - 1]["y"]))


def _wkv(r, k, v, lw, a, kk_p, ka_p, rk_p, lnw_p, lnb_p, rows=512):
    b, s, d = r.shape
    seq_spec = pl.BlockSpec((1, rows, d), lambda bi, ci: (bi, ci, 0))
    par_spec = pl.BlockSpec((1, d), lambda bi, ci: (0, 0))
    return pl.pallas_call(
        functools.partial(_wkv_kernel, rows // CHUNK),
        out_shape=jax.ShapeDtypeStruct((b, s, d), F32),
        grid=(b, s // rows),
        in_specs=[seq_spec] * 5 + [par_spec] * 5,
        out_specs=seq_spec,
        scratch_shapes=[pltpu.VMEM((d // GROUP_W, GROUP_W, GROUP_W), F32)],
        compiler_params=pltpu.CompilerParams(
            dimension_semantics=("parallel", "arbitrary"), vmem_limit_bytes=VMEM_LIMIT),
        name="wkv",
    )(r, k, v, lw, a, kk_p, ka_p, rk_p, lnw_p, lnb_p)


def _proj_mlp_kernel(has_gate, final_norm, ff_chunk, sub, *refs):
    refs = list(refs)
    y_ref = refs.pop(0)
    gate_ref = refs.pop(0) if has_gate else None
    x_ref, wo_ref, bo_ref, gm_ref, win_ref, wout_ref = refs[:6]
    refs = refs[6:]
    gf_ref = refs.pop(0) if final_norm else None
    o_ref = refs.pop(0)

    n_sub = x_ref.shape[0] // sub
    rows_of = lambda j: slice(j * sub, (j + 1) * sub)

    def head(j, out):
        rows = rows_of(j)
        y = y_ref[rows, :].astype(F32)
        if has_gate:
            y = y * gate_ref[rows, :]
        yield
        x1 = x_ref[rows, :] + _dot(y, wo_ref[...]) + bo_ref[...]
        yield
        out["x1"] = x1
        out["hb"] = _rms(x1, gm_ref[...]).astype(BF16)
        yield

    def mlp(j, ins, out):
        acc = ins["x1"]
        for c in range(D_FF // ff_chunk):
            cols = slice(c * ff_chunk, (c + 1) * ff_chunk)
            hid = jnp.maximum(jnp.dot(ins["hb"], win_ref[:, cols], preferred_element_type=F32), 0.0)
            yield
            acc = acc + _dot(hid * hid, wout_ref[cols, :])
            yield
        out["acc"] = acc

    def tail(j, acc):
        if final_norm:
            acc = _rms(acc, gf_ref[...])
        o_ref[rows_of(j), :] = acc
        yield

    ins = [dict() for _ in range(n_sub)]
    outs = [dict() for _ in range(n_sub)]
    _drain(head(0, ins[0]))
    for j in range(n_sub):
        fillers = []
        if j + 1 < n_sub:
            fillers.append(head(j + 1, ins[j + 1]))
        if j > 0:
            fillers.append(tail(j - 1, outs[j - 1]["acc"]))
        _interleave(mlp(j, ins[j], outs[j]), fillers)
    _drain(tail(n_sub - 1, outs[n_sub - 1]["acc"]))


def _proj_mlp(y, gate, x, wo, bo, gm, win, wout, gf, tm=512, sub=256, ff_chunk=1024):
    t, d = x.shape
    row_spec = pl.BlockSpec((tm, d), lambda i: (i, 0))
    args = [y] + ([gate] if gate is not None else []) + [x, wo, bo, gm, win, wout]
    resident = lambda c: pl.BlockSpec(c.shape, lambda i: (0,) * c.ndim,
                                      pipeline_mode=pl.Buffered(1))
    specs = [row_spec] * (len(args) - 5) + [resident(c) for c in args[-5:]]
    if gf is not None:
        args.append(gf)
        specs.append(_const_spec(gf.shape))
    return pl.pallas_call(
        functools.partial(_proj_mlp_kernel, gate is not None, gf is not None, ff_chunk, sub),
        out_shape=jax.ShapeDtypeStruct((t, d), F32),
        grid=(t // tm,),
        in_specs=specs,
        out_specs=row_spec,
        compiler_params=pltpu.CompilerParams(
            dimension_semantics=("parallel",), vmem_limit_bytes=VMEM_LIMIT),
        name="proj_mlp",
    )(*args)


def _rope_tab_kernel(pos_ref, freq_ref, cos_ref, sin_ref):
    ang = pos_ref[...] * freq_ref[...]
    lane = lax.broadcasted_iota(jnp.int32, ang.shape, 1)
    cos_ref[...] = jnp.cos(ang)
    sin_ref[...] = jnp.where(lane % HEAD < HEAD // 2, -1.0, 1.0) * jnp.sin(ang)


def _rope_tab(pos_col, freq_row, tm=1024):
    t = pos_col.shape[0]
    out = jax.ShapeDtypeStruct((t, LANES), F32)
    spec = pl.BlockSpec((tm, LANES), lambda i: (i, 0))
    return pl.pallas_call(
        _rope_tab_kernel,
        out_shape=(out, out),
        grid=(t // tm,),
        in_specs=[pl.BlockSpec((tm, 1), lambda i: (i, 0)), _const_spec(freq_row.shape)],
        out_specs=(spec, spec),
        compiler_params=pltpu.CompilerParams(dimension_semantics=("parallel",)),
        name="rope_tab",
    )(pos_col, freq_row)


def _attn_qkv_kernel(sub, x_ref, g_ref, w_ref, b_ref, cos_ref, sin_ref, o_ref):
    n_sub = x_ref.shape[0] // sub
    rows_of = lambda j: slice(j * sub, (j + 1) * sub)
    lane = lax.broadcasted_iota(jnp.int32, (sub, LANES), 1)
    first_half = lane % HEAD < HEAD // 2
    n_q = N_HEADS * HEAD // LANES
    n_rot = (N_HEADS + N_KV_HEADS) * HEAD // LANES
    n_blk = 2 * LANES

    def norm(j, out):
        out["h"] = _rms(x_ref[rows_of(j), :], g_ref[...]).astype(BF16)
        yield

    def project(j, ins, out):
        out["qkv"] = []
        for c in range(QKV_DIM // n_blk):
            cols = slice(c * n_blk, (c + 1) * n_blk)
            out["qkv"].append(jnp.dot(ins["h"], w_ref[:, cols], preferred_element_type=F32)
                              + b_ref[:, cols])
            yield

    def rotary(j, qkv):
        rows = rows_of(j)
        cos = cos_ref[rows, :]
        sin = sin_ref[rows, :]
        for s in range(QKV_DIM // LANES):
            c, half = divmod(s, n_blk // LANES)
            blk = qkv[c][:, half * LANES:(half + 1) * LANES]
            if s < n_rot:
                rot = jnp.where(first_half, pltpu.roll(blk, LANES - HEAD // 2, axis=1),
                                pltpu.roll(blk, HEAD // 2, axis=1))
                blk = blk * cos + rot * sin
            if s < n_q:
                blk = blk * Q_SCALE
            o_ref[rows, s * LANES:(s + 1) * LANES] = blk.astype(o_ref.dtype)
            if s % 2 == 1:
                yield

    ins = [dict() for _ in range(n_sub)]
    outs = [dict() for _ in range(n_sub)]
    _drain(norm(0, ins[0]))
    for j in range(n_sub):
        fillers = []
        if j + 1 < n_sub:
            fillers.append(norm(j + 1, ins[j + 1]))
        if j > 0:
            fillers.append(rotary(j - 1, outs[j - 1]["qkv"]))
        _interleave(project(j, ins[j], outs[j]), fillers)
    _drain(rotary(n_sub - 1, outs[n_sub - 1]["qkv"]))


def _attn_qkv(x, g, w, b, cos, sin, tm=512, sub=256):
    t, d = x.shape
    row = lambda width: pl.BlockSpec((tm, width), lambda i: (i, 0))
    return pl.pallas_call(
        functools.partial(_attn_qkv_kernel, sub),
        out_shape=jax.ShapeDtypeStruct((t, QKV_DIM), BF16),
        grid=(t // tm,),
        in_specs=[row(d), _const_spec(g.shape), _const_spec(w.shape), _const_spec(b.shape),
                  row(LANES), row(LANES)],
        out_specs=row(QKV_DIM),
        compiler_params=pltpu.CompilerParams(
            dimension_semantics=("parallel",), vmem_limit_bytes=VMEM_LIMIT),
        name="attn_qkv",
    )(x, g, w, b, cos, sin)


def _attn_core_kernel(q_blocks, q_ref, kp_ref, kc_ref, vp_ref, vc_ref, sink_ref, o_ref):
    n = pl.program_id(1)
    kr = lax.broadcasted_iota(jnp.int32, (2 * WINDOW, WINDOW), 0)
    qc = lax.broadcasted_iota(jnp.int32, (2 * WINDOW, WINDOW), 1)
    own = kr >= WINDOW
    band = (own & (kr - WINDOW <= qc)) | (jnp.logical_not(own) & (kr > qc))
    first_band = band & (own | (n > 0))
    neg = -jnp.inf
    probs = [(i, j) for i in range(q_blocks) for j in range(N_KV_HEADS)]
    blk = lambda i: slice(i * WINDOW, (i + 1) * WINDOW)
    kvl = lambda j: slice(j * HEAD, (j + 1) * HEAD)
    head = lambda j, g: j * GROUP + g

    def keys(prev_ref, cur_ref, i, j):
        prev = prev_ref[0, :, kvl(j)] if i == 0 else cur_ref[0, blk(i - 1), kvl(j)]
        return jnp.concatenate([prev, cur_ref[0, blk(i), kvl(j)]], axis=0)

    q = [jnp.concatenate([q_ref[0, blk(i), head(j, g) * HEAD:(head(j, g) + 1) * HEAD]
                          for g in range(GROUP)], axis=0) for i, j in probs]
    s_t = [_dot_nt(keys(kp_ref, kc_ref, i, j), q[p]) for p, (i, j) in enumerate(probs)]
    p_t = [[] for _ in probs]
    for g in range(GROUP):
        sink = [sink_ref[:, head(j, g):head(j, g) + 1] * LOG2E for _, j in probs]
        x = [jnp.where(first_band if i == 0 else band, s_t[p][:, blk(g)], neg)
             for p, (i, j) in enumerate(probs)]
        m = [jnp.maximum(jnp.max(x[p], axis=0, keepdims=True), sink[p])
             for p in range(len(probs))]
        e = [jnp.exp2(x[p] - m[p]) for p in range(len(probs))]
        den = [jnp.sum(e[p], axis=0, keepdims=True) + jnp.exp2(sink[p] - m[p])
               for p in range(len(probs))]
        for p in range(len(probs)):
            p_t[p].append((e[p] * (1.0 / den[p])).astype(BF16))
    for p, (i, j) in enumerate(probs):
        o = lax.dot_general(jnp.concatenate(p_t[p], axis=1), keys(vp_ref, vc_ref, i, j),
                            (((0,), (0,)), ((), ())), preferred_element_type=F32)
        for g in range(GROUP):
            o_ref[0, blk(i), head(j, g) * HEAD:(head(j, g) + 1) * HEAD] = (
                o[blk(g)].astype(o_ref.dtype))


def _attn_core(qkv, sinks, q_blocks=4):
    b, s, _ = qkv.shape
    nq = N_HEADS * HEAD
    tq = q_blocks * WINDOW
    k_blk = nq // LANES
    v_blk = k_blk + N_KV_HEADS * HEAD // LANES
    q_spec = pl.BlockSpec((1, tq, nq), lambda bi, n: (bi, n, 0))
    cur = lambda blk: pl.BlockSpec((1, tq, LANES), lambda bi, n: (bi, n, blk))
    prev = lambda blk: pl.BlockSpec((1, WINDOW, LANES),
                                    lambda bi, n: (bi, jnp.maximum(n * q_blocks - 1, 0), blk))
    return pl.pallas_call(
        functools.partial(_attn_core_kernel, q_blocks),
        out_shape=jax.ShapeDtypeStruct((b, s, nq), BF16),
        grid=(b, s // tq),
        in_specs=[q_spec, prev(k_blk), cur(k_blk), prev(v_blk), cur(v_blk),
                  _const_spec(sinks.shape)],
        out_specs=q_spec,
        compiler_params=pltpu.CompilerParams(
            dimension_semantics=("parallel", "arbitrary"), vmem_limit_bytes=VMEM_LIMIT),
        name="attn_core",
    )(qkv, qkv, qkv, qkv, qkv, sinks)


def kernel(x, positions, norm_mix_g, norm_mlp_g, norm_final_g, rwkv_mu, rwkv_w_r, rwkv_w_k, rwkv_w_v, rwkv_w_o, rwkv_w0, rwkv_w1, rwkv_w2, rwkv_a0, rwkv_a1, rwkv_a2, rwkv_g1, rwkv_g2, rwkv_k_k, rwkv_k_a, rwkv_r_k, rwkv_ln_w, rwkv_ln_b, attn_w_qkv, attn_b_qkv, attn_sinks, attn_w_o, attn_b_o, mlp_w_in, mlp_w_out):
    b, s, d = x.shape
    t = b * s
    x2d = x.reshape(t, d)
    row = lambda p: p.reshape(1, -1)
    bf = lambda w: w.astype(BF16)
    zero_bias = jnp.zeros((1, d), F32)

    r, k, v, lw, a, gate = _rwkv_pre(
        x2d, s, row(norm_mix_g[0]), rwkv_mu[0], bf(rwkv_w_r[0]), bf(rwkv_w_k[0]),
        bf(rwkv_w_v[0]), row(rwkv_w0[0]), bf(rwkv_w1[0]), bf(rwkv_w2[0]), row(rwkv_a0[0]),
        bf(rwkv_a1[0]), bf(rwkv_a2[0]), bf(rwkv_g1[0]), bf(rwkv_g2[0]))
    seq = lambda z: z.reshape(b, s, d)
    y = _wkv(seq(r), seq(k), seq(v), seq(lw), seq(a), row(rwkv_k_k[0]), row(rwkv_k_a[0]),
             row(rwkv_r_k[0]), row(rwkv_ln_w[0]), row(rwkv_ln_b[0]))
    x2d = _proj_mlp(y.reshape(t, d), gate, x2d, bf(rwkv_w_o[0]), zero_bias,
                    row(norm_mlp_g[0]), bf(mlp_w_in[0]), bf(mlp_w_out[0]), None)

    inv_freq = ROPE_THETA ** (-jnp.arange(0, HEAD, 2, dtype=F32) / HEAD)
    freq_row = jnp.tile(inv_freq, LANES // (HEAD // 2)).reshape(1, LANES)
    cos, sin = _rope_tab(positions.astype(F32).reshape(t, 1), freq_row)
    qkv = _attn_qkv(x2d, row(norm_mix_g[1]), bf(attn_w_qkv[0]), row(attn_b_qkv[0]), cos, sin)
    o = _attn_core(qkv.reshape(b, s, QKV_DIM), row(attn_sinks[0]))
    out = _proj_mlp(o.reshape(t, d), None, x2d, bf(attn_w_o[0]), row(attn_b_o[0]),
                    row(norm_mlp_g[1]), bf(mlp_w_in[1]), bf(mlp_w_out[1]), row(norm_final_g))
    return out.reshape(b, s, d)
```

```python
import functools
import math

import jax
import jax.numpy as jnp
from jax import lax
from jax.experimental import pallas as pl
from jax.experimental.pallas import tpu as pltpu

F32 = jnp.float32
BF16 = jnp.bfloat16

D_MODEL = 1024
HEAD = 64
N_HEADS = D_MODEL // HEAD
N_KV_HEADS = 2
GROUP = N_HEADS // N_KV_HEADS
WINDOW = 128
QKV_DIM = (N_HEADS + 2 * N_KV_HEADS) * HEAD
D_FF = 4 * D_MODEL
ROPE_THETA = 10000.0
RMS_EPS = 1e-5
GN_EPS = 64e-5

LANES = 128
LOG2E = math.log2(math.e)
Q_SCALE = HEAD ** -0.5 * LOG2E
CHUNK = 64
VMEM_LIMIT = 56 * 1024 * 1024


def _rms(x, g):
    return x * lax.rsqrt(jnp.mean(x * x, axis=-1, keepdims=True) + RMS_EPS) * g


def _dot(a, b):
    return jnp.dot(a.astype(BF16), b.astype(BF16), preferred_element_type=F32)


def _dot_nt(a, b):
    return lax.dot_general(a.astype(BF16), b.astype(BF16), (((1,), (1,)), ((), ())),
                           preferred_element_type=F32)


def _dot_tn(a, b):
    return lax.dot_general(a.astype(BF16), b.astype(BF16), (((0,), (0,)), ((), ())),
                           preferred_element_type=F32)


def _const_spec(shape):
    nd = len(shape)
    return pl.BlockSpec(shape, lambda *_: (0,) * nd)


def _drain(gen):
    for _ in gen:
        pass


def _interleave(main, fillers):
    for _ in main:
        for f in fillers:
            next(f, None)
    for f in fillers:
        _drain(f)


def _rwkv_pre_kernel(seq_tiles, sub, x_ref, xp_ref, g_ref, mu_ref, wr_ref, wk_ref, wv_ref,
                     w0_ref, w1_ref, w2_ref, a0_ref, a1_ref, a2_ref, g1_ref, g2_ref,
                     r_out, k_out, v_out, lw_out, a_out, g_out):
    i = pl.program_id(0)
    n_sub = x_ref.shape[0] // sub
    g = g_ref[...]
    mu = mu_ref[...]
    rows_of = lambda j: slice(j * sub, (j + 1) * sub)
    hp = _rms(xp_ref[...], g)
    carry_row = {0: jnp.where(i % seq_tiles == 0, 0.0, hp[7:8, :])}

    def mix(j, out):
        h = _rms(x_ref[rows_of(j), :], g)
        carry_row[j + 1] = h[sub - 1:sub, :]
        yield
        row = lax.broadcasted_iota(jnp.int32, h.shape, 0)
        dx = jnp.where(row == 0, carry_row[j], pltpu.roll(h, 1, axis=0)) - h
        yield
        for n, name in enumerate(("r", "w", "k", "v", "a", "g")):
            out[name] = (h + dx * mu[n:n + 1]).astype(BF16)
            if n % 2 == 1:
                yield

    def project(j, xs):
        rows = rows_of(j)
        dw = _dot(xs["w"], w1_ref[...])
        da = _dot(xs["a"], a1_ref[...])
        dg = _dot(xs["g"], g1_ref[...])
        yield
        r_out[rows, :] = jnp.dot(xs["r"], wr_ref[...], preferred_element_type=F32)
        yield
        w_pre = w0_ref[...] + _dot(jnp.tanh(dw), w2_ref[...])
        a_pre = a0_ref[...] + _dot(da, a2_ref[...])
        g_out[rows, :] = _dot(jax.nn.sigmoid(dg), g2_ref[...])
        yield
        k_out[rows, :] = jnp.dot(xs["k"], wk_ref[...], preferred_element_type=F32)
        yield
        lw_out[rows, :] = jax.nn.sigmoid(w_pre) * (-math.exp(-0.5))
        a_out[rows, :] = jax.nn.sigmoid(a_pre)
        yield
        v_out[rows, :] = jnp.dot(xs["v"], wv_ref[...], preferred_element_type=F32)
        yield

    xs = [dict() for _ in range(n_sub)]
    _drain(mix(0, xs[0]))
    for j in range(n_sub):
        fillers = [mix(j + 1, xs[j + 1])] if j + 1 < n_sub else []
        _interleave(project(j, xs[j]), fillers)


def _rwkv_pre(x2d, seq_len, g, mu, wr, wk, wv, w0, w1, w2, a0, a1, a2, g1, g2, tm=512, sub=256):
    t, d = x2d.shape
    seq_tiles = seq_len // tm
    row_spec = pl.BlockSpec((tm, d), lambda i: (i, 0))
    prev_spec = pl.BlockSpec((8, d), lambda i: (jnp.maximum(i * (tm // 8) - 1, 0), 0))
    consts = (g, mu, wr, wk, wv, w0, w1, w2, a0, a1, a2, g1, g2)
    out = jax.ShapeDtypeStruct((t, d), F32)
    return pl.pallas_call(
        functools.partial(_rwkv_pre_kernel, seq_tiles, sub),
        out_shape=(out,) * 6,
        grid=(t // tm,),
        in_specs=[row_spec, prev_spec] + [_const_spec(c.shape) for c in consts],
        out_specs=(row_spec,) * 6,
        compiler_params=pltpu.CompilerParams(
            dimension_semantics=("parallel",), vmem_limit_bytes=VMEM_LIMIT),
        name="rwkv_pre",
    )(x2d, x2d, *consts)


def _cumsum_rows(x):
    n = x.shape[0]
    row = lax.broadcasted_iota(jnp.int32, x.shape, 0)
    s = 1
    while s < n:
        x = x + jnp.where(row >= s, pltpu.roll(x, s, axis=0), 0.0)
        s *= 2
    return x


INV_BASE = 8
GROUP_HEADS = 4
GROUP_W = GROUP_HEADS * HEAD


def _head_sum(x):
    low = lax.broadcasted_iota(jnp.int32, (x.shape[0], LANES), 1) < HEAD
    outs = []
    for c in range(x.shape[1] // LANES):
        xc = x[:, c * LANES:(c + 1) * LANES]
        s_lo = jnp.sum(jnp.where(low, xc, 0.0), axis=-1, keepdims=True)
        s_hi = jnp.sum(jnp.where(low, 0.0, xc), axis=-1, keepdims=True)
        outs.append(jnp.where(low, s_lo, s_hi))
    return jnp.concatenate(outs, axis=1)


def _wkv_kernel(chunks, r_ref, k_ref, v_ref, lw_ref, a_ref,
                kk_ref, ka_ref, rk_ref, lnw_ref, lnb_ref, o_ref, state_ref):
    n_batch = r_ref.shape[0]
    n_groups = D_MODEL // GROUP_W
    probs = [(b, g) for b in range(n_batch) for g in range(n_groups)]
    ps = range(len(probs))

    @pl.when(pl.program_id(0) == 0)
    def _():
        state_ref[...] = jnp.zeros_like(state_ref)

    ri = lax.broadcasted_iota(jnp.int32, (CHUNK, GROUP_W), 0)
    lane = lax.broadcasted_iota(jnp.int32, (CHUNK, GROUP_W), 1)
    cj = lane & (HEAD - 1)
    lane_head = lax.shift_right_logical(lane, HEAD.bit_length() - 1)
    head_masks = [lane_head == h for h in range(GROUP_HEADS)]
    blk = lambda idx, size: lax.shift_right_logical(idx, size.bit_length() - 1)
    strict = cj < ri
    incl = cj <= ri
    eye = (cj == ri).astype(F32)
    diag = blk(ri, INV_BASE) == blk(cj, INV_BASE)
    bands = []
    size = INV_BASE
    while size < CHUNK:
        bands.append((blk(ri, 2 * size) == blk(cj, 2 * size))
                     & (blk(ri, size) != blk(cj, size)))
        size *= 2
    sr = lax.broadcasted_iota(jnp.int32, (GROUP_W, GROUP_W), 0)
    sc = lax.broadcasted_iota(jnp.int32, (GROUP_W, GROUP_W), 1)
    state_mask = blk(sr, HEAD) == blk(sc, HEAD)

    def bd(y):
        return jnp.concatenate([jnp.where(m, y, 0.0) for m in head_masks], axis=0).astype(BF16)

    def mm(x, w):
        return jnp.dot(x.astype(BF16), w, preferred_element_type=F32)

    def mm_nt(x, w):
        return lax.dot_general(x.astype(BF16), w, (((1,), (1,)), ((), ())),
                               preferred_element_type=F32)

    cols = [slice(g * GROUP_W, (g + 1) * GROUP_W) for g in range(n_groups)]
    rows_of = lambda c: slice(c * CHUNK, (c + 1) * CHUNK)


    def prepare(c, b, out):
        rows = rows_of(c)
        lw = lw_ref[b, rows, :]
        linc = _cumsum_rows(lw)
        yield
        p_inc = jnp.exp(linc)
        p_exc = jnp.exp(linc - lw)
        p_inv = jnp.exp(-linc)
        yield
        k = k_ref[b, rows, :]
        a = a_ref[b, rows, :]
        kk = k * kk_ref[...]
        kk = kk * lax.rsqrt(jnp.maximum(_head_sum(kk * kk), 1e-24))
        yield
        kmod = k * (1.0 + (a - 1.0) * ka_ref[...])
        at_f = (-kk * p_exc).astype(BF16)
        rt_f = (r_ref[b, rows, :] * p_inc).astype(BF16)
        yield
        out["lhs"] = [jnp.concatenate([at_f[:, cl], rt_f[:, cl]], axis=0) for cl in cols]
        out["bt"] = kk * a * p_inv
        out["kt"] = kmod * p_inv
        out["kmod"] = kmod
        out["p_end"] = p_inc[CHUNK - 1:CHUNK, :]
        yield

    def matmuls(c, ops, out):
        v = [v_ref[b, rows_of(c), :] for b in range(n_batch)]
        lhs = [ops[b]["lhs"][g] for b, g in probs]
        bt = [ops[b]["bt"][:, cols[g]] for b, g in probs]
        kt = [ops[b]["kt"][:, cols[g]] for b, g in probs]
        vg = [v[b][:, cols[g]] for b, g in probs]
        ab = [mm_nt(lhs[p], bd(bt[p])) for p in ps]
        ak = [mm_nt(lhs[p], bd(kt[p])) for p in ps]
        yield
        a_ab = [jnp.where(strict, x[:CHUNK], 0.0) for x in ab]
        a_rb = [jnp.where(incl, x[CHUNK:], 0.0) for x in ab]
        a_k = [jnp.concatenate([jnp.where(strict, x[:CHUNK], 0.0),
                                jnp.where(incl, x[CHUNK:], 0.0)], axis=0) for x in ak]
        d1 = [jnp.where(diag, x, 0.0) for x in a_ab]
        d2 = [mm(d1[p], bd(d1[p])) for p in ps]
        yield
        t = [eye + x for x in d1]
        td = [mm(jnp.concatenate([t[p], d2[p]], axis=0), bd(d2[p])) for p in ps]
        yield
        t = [t[p] + td[p][:CHUNK] for p in ps]
        t = [t[p] + mm(t[p], bd(td[p][CHUNK:])) for p in ps]
        yield
        for band in bands:
            te = [mm(t[p], bd(jnp.where(band, a_ab[p], 0.0))) for p in ps]
            yield
            t = [t[p] + mm(te[p], bd(t[p])) for p in ps]
            yield
        s0 = [state_ref[p] for p in ps]
        zy = [mm_nt(lhs[p], s0[p].astype(BF16)) for p in ps]
        av = [mm(a_k[p], bd(vg[p])) for p in ps]
        yield
        u = [mm(t[p], bd(zy[p][:CHUNK] + av[p][:CHUNK])) for p in ps]
        yield
        y = [zy[p][CHUNK:] + av[p][CHUNK:] + mm(a_rb[p], bd(u[p])) for p in ps]
        for p, (b, g) in enumerate(probs):
            uv = jnp.concatenate([u[p], vg[p]], axis=0)
            bk = jnp.concatenate([bt[p], kt[p]], axis=0)
            state_ref[p] = (jnp.where(state_mask, s0[p] + _dot_tn(uv, bk), 0.0)
                            * ops[b]["p_end"][:, cols[g]])
        out["y"] = [jnp.concatenate(y[b * n_groups:(b + 1) * n_groups], axis=1)
                    for b in range(n_batch)]
        yield

    def finish(c, b, ops, y_f):
        rows = rows_of(c)
        inv_n = 1.0 / HEAD
        yc = y_f - _head_sum(y_f) * inv_n
        yield
        var = _head_sum(yc * yc) * inv_n
        yn = yc * lax.rsqrt(var + GN_EPS) * lnw_ref[...] + lnb_ref[...]
        yield
        bonus = _head_sum(r_ref[b, rows, :] * ops["kmod"] * rk_ref[...])
        o_ref[b, rows, :] = yn + bonus * v_ref[b, rows, :]
        yield

    bs = range(n_batch)
    ops = [[dict() for _ in bs] for _ in range(chunks)]
    res = [dict() for _ in range(chunks)]
    for b in bs:
        _drain(prepare(0, b, ops[0][b]))
    for c in range(chunks):
        fillers = []
        if c + 1 < chunks:
            fillers += [prepare(c + 1, b, ops[c + 1][b]) for b in bs]
        if c > 0:
            fillers += [finish(c - 1, b, ops[c - 1][b], res[c - 1]["y"][b]) for b in bs]
        _interleave(matmuls(c, ops[c], res[c]), fillers)
    for b in bs:
        _drain(finish(chunks - 1, b, ops[chunks - 1][b], res[chunks - 1]["y"][b]))


def _wkv(r, k, v, lw, a, kk_p, ka_p, rk_p, lnw_p, lnb_p, rows=256):
    b, s, d = r.shape
    seq_spec = pl.BlockSpec((b, rows, d), lambda ci: (0, ci, 0))
    par_spec = pl.BlockSpec((1, d), lambda ci: (0, 0))
    return pl.pallas_call(
        functools.partial(_wkv_kernel, rows // CHUNK),
        out_shape=jax.ShapeDtypeStruct((b, s, d), F32),
        grid=(s // rows,),
        in_specs=[seq_spec] * 5 + [par_spec] * 5,
        out_specs=seq_spec,
        scratch_shapes=[pltpu.VMEM((b * d // GROUP_W, GROUP_W, GROUP_W), F32)],
        compiler_params=pltpu.CompilerParams(
            dimension_semantics=("arbitrary",), vmem_limit_bytes=VMEM_LIMIT),
        name="wkv",
    )(r, k, v, lw, a, kk_p, ka_p, rk_p, lnw_p, lnb_p)


def _proj_mlp_kernel(has_gate, final_norm, ff_chunk, sub, *refs):
    refs = list(refs)
    y_ref = refs.pop(0)
    gate_ref = refs.pop(0) if has_gate else None
    x_ref, wo_ref, bo_ref, gm_ref, win_ref, wout_ref = refs[:6]
    refs = refs[6:]
    gf_ref = refs.pop(0) if final_norm else None
    o_ref = refs.pop(0)

    n_sub = x_ref.shape[0] // sub
    rows_of = lambda j: slice(j * sub, (j + 1) * sub)

    def head(j, out):
        rows = rows_of(j)
        y = y_ref[rows, :].astype(F32)
        if has_gate:
            y = y * gate_ref[rows, :]
        yield
        x1 = x_ref[rows, :] + _dot(y, wo_ref[...]) + bo_ref[...]
        yield
        out["x1"] = x1
        out["hb"] = _rms(x1, gm_ref[...]).astype(BF16)
        yield

    def mlp(j, ins, out):
        acc = ins["x1"]
        for c in range(D_FF // ff_chunk):
            cols = slice(c * ff_chunk, (c + 1) * ff_chunk)
            hid = jnp.maximum(jnp.dot(ins["hb"], win_ref[:, cols], preferred_element_type=F32), 0.0)
            yield
            acc = acc + _dot(hid * hid, wout_ref[cols, :])
            yield
        out["acc"] = acc

    def tail(j, acc):
        if final_norm:
            acc = _rms(acc, gf_ref[...])
        o_ref[rows_of(j), :] = acc
        yield

    ins = [dict() for _ in range(n_sub)]
    outs = [dict() for _ in range(n_sub)]
    _drain(head(0, ins[0]))
    for j in range(n_sub):
        fillers = []
        if j + 1 < n_sub:
            fillers.append(head(j + 1, ins[j + 1]))
        if j > 0:
            fillers.append(tail(j - 1, outs[j - 1]["acc"]))
        _interleave(mlp(j, ins[j], outs[j]), fillers)
    _drain(tail(n_sub - 1, outs[n_sub - 1]["acc"]))


def _proj_mlp(y, gate, x, wo, bo, gm, win, wout, gf, tm=512, sub=256, ff_chunk=1024):
    t, d = x.shape
    row_spec = pl.BlockSpec((tm, d), lambda i: (i, 0))
    args = [y] + ([gate] if gate is not None else []) + [x, wo, bo, gm, win, wout]
    resident = lambda c: pl.BlockSpec(c.shape, lambda i: (0,) * c.ndim,
                                      pipeline_mode=pl.Buffered(1))
    specs = [row_spec] * (len(args) - 5) + [resident(c) for c in args[-5:]]
    if gf is not None:
        args.append(gf)
        specs.append(_const_spec(gf.shape))
    return pl.pallas_call(
        functools.partial(_proj_mlp_kernel, gate is not None, gf is not None, ff_chunk, sub),
        out_shape=jax.ShapeDtypeStruct((t, d), F32),
        grid=(t // tm,),
        in_specs=specs,
        out_specs=row_spec,
        compiler_params=pltpu.CompilerParams(
            dimension_semantics=("parallel",), vmem_limit_bytes=VMEM_LIMIT),
        name="proj_mlp",
    )(*args)


def _rope_tab_kernel(pos_ref, freq_ref, cos_ref, sin_ref):
    ang = pos_ref[...] * freq_ref[...]
    lane = lax.broadcasted_iota(jnp.int32, ang.shape, 1)
    cos_ref[...] = jnp.cos(ang)
    sin_ref[...] = jnp.where(lane % HEAD < HEAD // 2, -1.0, 1.0) * jnp.sin(ang)


def _rope_tab(pos_col, freq_row, tm=1024):
    t = pos_col.shape[0]
    out = jax.ShapeDtypeStruct((t, LANES), F32)
    spec = pl.BlockSpec((tm, LANES), lambda i: (i, 0))
    return pl.pallas_call(
        _rope_tab_kernel,
        out_shape=(out, out),
        grid=(t // tm,),
        in_specs=[pl.BlockSpec((tm, 1), lambda i: (i, 0)), _const_spec(freq_row.shape)],
        out_specs=(spec, spec),
        compiler_params=pltpu.CompilerParams(dimension_semantics=("parallel",)),
        name="rope_tab",
    )(pos_col, freq_row)


def _attn_qkv_kernel(sub, x_ref, g_ref, w_ref, b_ref, cos_ref, sin_ref, o_ref):
    n_sub = x_ref.shape[0] // sub
    rows_of = lambda j: slice(j * sub, (j + 1) * sub)
    lane = lax.broadcasted_iota(jnp.int32, (sub, LANES), 1)
    first_half = lane % HEAD < HEAD // 2
    n_q = N_HEADS * HEAD // LANES
    n_rot = (N_HEADS + N_KV_HEADS) * HEAD // LANES
    n_blk = 2 * LANES

    def norm(j, out):
        out["h"] = _rms(x_ref[rows_of(j), :], g_ref[...]).astype(BF16)
        yield

    def project(j, ins, out):
        out["qkv"] = []
        for c in range(QKV_DIM // n_blk):
            cols = slice(c * n_blk, (c + 1) * n_blk)
            out["qkv"].append(jnp.dot(ins["h"], w_ref[:, cols], preferred_element_type=F32)
                              + b_ref[:, cols])
            yield

    def rotary(j, qkv):
        rows = rows_of(j)
        cos = cos_ref[rows, :]
        sin = sin_ref[rows, :]
        for s in range(QKV_DIM // LANES):
            c, half = divmod(s, n_blk // LANES)
            blk = qkv[c][:, half * LANES:(half + 1) * LANES]
            if s < n_rot:
                rot = jnp.where(first_half, pltpu.roll(blk, LANES - HEAD // 2, axis=1),
                                pltpu.roll(blk, HEAD // 2, axis=1))
                blk = blk * cos + rot * sin
            if s < n_q:
                blk = blk * Q_SCALE
            o_ref[rows, s * LANES:(s + 1) * LANES] = blk.astype(o_ref.dtype)
            if s % 2 == 1:
                yield

    ins = [dict() for _ in range(n_sub)]
    outs = [dict() for _ in range(n_sub)]
    _drain(norm(0, ins[0]))
    for j in range(n_sub):
        fillers = []
        if j + 1 < n_sub:
            fillers.append(norm(j + 1, ins[j + 1]))
        if j > 0:
            fillers.append(rotary(j - 1, outs[j - 1]["qkv"]))
        _interleave(project(j, ins[j], outs[j]), fillers)
    _drain(rotary(n_sub - 1, outs[n_sub - 1]["qkv"]))


def _attn_qkv(x, g, w, b, cos, sin, tm=512, sub=256):
    t, d = x.shape
    row = lambda width: pl.BlockSpec((tm, width), lambda i: (i, 0))
    return pl.pallas_call(
        functools.partial(_attn_qkv_kernel, sub),
        out_shape=jax.ShapeDtypeStruct((t, QKV_DIM), BF16),
        grid=(t // tm,),
        in_specs=[row(d), _const_spec(g.shape), _const_spec(w.shape), _const_spec(b.shape),
                  row(LANES), row(LANES)],
        out_specs=row(QKV_DIM),
        compiler_params=pltpu.CompilerParams(
            dimension_semantics=("parallel",), vmem_limit_bytes=VMEM_LIMIT),
        name="attn_qkv",
    )(x, g, w, b, cos, sin)


def _attn_core_kernel(q_blocks, q_ref, kp_ref, kc_ref, vp_ref, vc_ref, sink_ref, o_ref):
    n = pl.program_id(1)
    kr = lax.broadcasted_iota(jnp.int32, (2 * WINDOW, WINDOW), 0)
    qc = lax.broadcasted_iota(jnp.int32, (2 * WINDOW, WINDOW), 1)
    own = kr >= WINDOW
    band = (own & (kr - WINDOW <= qc)) | (jnp.logical_not(own) & (kr > qc))
    first_band = band & (own | (n > 0))
    neg = -jnp.inf
    probs = [(i, j) for i in range(q_blocks) for j in range(N_KV_HEADS)]
    blk = lambda i: slice(i * WINDOW, (i + 1) * WINDOW)
    kvl = lambda j: slice(j * HEAD, (j + 1) * HEAD)
    head = lambda j, g: j * GROUP + g

    def keys(prev_ref, cur_ref, i, j):
        prev = prev_ref[0, :, kvl(j)] if i == 0 else cur_ref[0, blk(i - 1), kvl(j)]
        return jnp.concatenate([prev, cur_ref[0, blk(i), kvl(j)]], axis=0)

    q = [jnp.concatenate([q_ref[0, blk(i), head(j, g) * HEAD:(head(j, g) + 1) * HEAD]
                          for g in range(GROUP)], axis=0) for i, j in probs]
    s_t = [_dot_nt(keys(kp_ref, kc_ref, i, j), q[p]) for p, (i, j) in enumerate(probs)]
    p_t = [[] for _ in probs]
    for g in range(GROUP):
        sink = [sink_ref[:, head(j, g):head(j, g) + 1] * LOG2E for _, j in probs]
        x = [jnp.where(first_band if i == 0 else band, s_t[p][:, blk(g)], neg)
             for p, (i, j) in enumerate(probs)]
        m = [jnp.maximum(jnp.max(x[p], axis=0, keepdims=True), sink[p])
             for p in range(len(probs))]
        e = [jnp.exp2(x[p] - m[p]) for p in range(len(probs))]
        den = [jnp.sum(e[p], axis=0, keepdims=True) + jnp.exp2(sink[p] - m[p])
               for p in range(len(probs))]
        for p in range(len(probs)):
            p_t[p].append((e[p] * (1.0 / den[p])).astype(BF16))
    for p, (i, j) in enumerate(probs):
        o = lax.dot_general(jnp.concatenate(p_t[p], axis=1), keys(vp_ref, vc_ref, i, j),
                            (((0,), (0,)), ((), ())), preferred_element_type=F32)
        for g in range(GROUP):
            o_ref[0, blk(i), head(j, g) * HEAD:(head(j, g) + 1) * HEAD] = (
                o[blk(g)].astype(o_ref.dtype))


def _attn_core(qkv, sinks, q_blocks=8):
    b, s, _ = qkv.shape
    nq = N_HEADS * HEAD
    tq = q_blocks * WINDOW
    k_blk = nq // LANES
    v_blk = k_blk + N_KV_HEADS * HEAD // LANES
    q_spec = pl.BlockSpec((1, tq, nq), lambda bi, n: (bi, n, 0))
    cur = lambda blk: pl.BlockSpec((1, tq, LANES), lambda bi, n: (bi, n, blk))
    prev = lambda blk: pl.BlockSpec((1, WINDOW, LANES),
                                    lambda bi, n: (bi, jnp.maximum(n * q_blocks - 1, 0), blk))
    return pl.pallas_call(
        functools.partial(_attn_core_kernel, q_blocks),
        out_shape=jax.ShapeDtypeStruct((b, s, nq), BF16),
        grid=(b, s // tq),
        in_specs=[q_spec, prev(k_blk), cur(k_blk), prev(v_blk), cur(v_blk),
                  _const_spec(sinks.shape)],
        out_specs=q_spec,
        compiler_params=pltpu.CompilerParams(
            dimension_semantics=("parallel", "arbitrary"), vmem_limit_bytes=VMEM_LIMIT),
        name="attn_core",
    )(qkv, qkv, qkv, qkv, qkv, sinks)


def kernel(x, positions, norm_mix_g, norm_mlp_g, norm_final_g, rwkv_mu, rwkv_w_r, rwkv_w_k, rwkv_w_v, rwkv_w_o, rwkv_w0, rwkv_w1, rwkv_w2, rwkv_a0, rwkv_a1, rwkv_a2, rwkv_g1, rwkv_g2, rwkv_k_k, rwkv_k_a, rwkv_r_k, rwkv_ln_w, rwkv_ln_b, attn_w_qkv, attn_b_qkv, attn_sinks, attn_w_o, attn_b_o, mlp_w_in, mlp_w_out):
    b, s, d = x.shape
    t = b * s
    x2d = x.reshape(t, d)
    row = lambda p: p.reshape(1, -1)
    bf = lambda w: w.astype(BF16)
    zero_bias = jnp.zeros((1, d), F32)

    r, k, v, lw, a, gate = _rwkv_pre(
        x2d, s, row(norm_mix_g[0]), rwkv_mu[0], bf(rwkv_w_r[0]), bf(rwkv_w_k[0]),
        bf(rwkv_w_v[0]), row(rwkv_w0[0]), bf(rwkv_w1[0]), bf(rwkv_w2[0]), row(rwkv_a0[0]),
        bf(rwkv_a1[0]), bf(rwkv_a2[0]), bf(rwkv_g1[0]), bf(rwkv_g2[0]))
    seq = lambda z: z.reshape(b, s, d)
    y = _wkv(seq(r), seq(k), seq(v), seq(lw), seq(a), row(rwkv_k_k[0]), row(rwkv_k_a[0]),
             row(rwkv_r_k[0]), row(rwkv_ln_w[0]), row(rwkv_ln_b[0]))
    x2d = _proj_mlp(y.reshape(t, d), gate, x2d, bf(rwkv_w_o[0]), zero_bias,
                    row(norm_mlp_g[0]), bf(mlp_w_in[0]), bf(mlp_w_out[0]), None)

    inv_freq = ROPE_THETA ** (-jnp.arange(0, HEAD, 2, dtype=F32) / HEAD)
    freq_row = jnp.tile(inv_freq, LANES // (HEAD // 2)).reshape(1, LANES)
    cos, sin = _rope_tab(positions.astype(F32).reshape(t, 1), freq_row)
    qkv = _attn_qkv(x2d, row(norm_mix_g[1]), bf(attn_w_qkv[0]), row(attn_b_qkv[0]), cos, sin)
    o = _attn_core(qkv.reshape(b, s, QKV_DIM), row(attn_sinks[0]))
    out = _proj_mlp(o.reshape(t, d), None, x2d, bf(attn_w_o[0]), row(attn_b_o[0]),
                    row(norm_mlp_g[1]), bf(mlp_w_in[1]), bf(mlp_w_out[1]), row(norm_final_g))
    return out.reshape(b, s, d)
```

```python
import functools
import math

import jax
import jax.numpy as jnp
from jax import lax
from jax.experimental import pallas as pl
from jax.experimental.pallas import tpu as pltpu

F32 = jnp.float32
BF16 = jnp.bfloat16

D_MODEL = 1024
HEAD = 64
N_HEADS = D_MODEL // HEAD
N_KV_HEADS = 2
GROUP = N_HEADS // N_KV_HEADS
WINDOW = 128
QKV_DIM = (N_HEADS + 2 * N_KV_HEADS) * HEAD
D_FF = 4 * D_MODEL
ROPE_THETA = 10000.0
RMS_EPS = 1e-5
GN_EPS = 64e-5

LANES = 128
LOG2E = math.log2(math.e)
Q_SCALE = HEAD ** -0.5 * LOG2E
CHUNK = 64
VMEM_LIMIT = 56 * 1024 * 1024


def _rms(x, g):
    return x * lax.rsqrt(jnp.mean(x * x, axis=-1, keepdims=True) + RMS_EPS) * g


def _dot(a, b):
    return jnp.dot(a.astype(BF16), b.astype(BF16), preferred_element_type=F32)


def _dot_nt(a, b):
    return lax.dot_general(a.astype(BF16), b.astype(BF16), (((1,), (1,)), ((), ())),
                           preferred_element_type=F32)


def _dot_tn(a, b):
    return lax.dot_general(a.astype(BF16), b.astype(BF16), (((0,), (0,)), ((), ())),
                           preferred_element_type=F32)


def _const_spec(shape):
    nd = len(shape)
    return pl.BlockSpec(shape, lambda *_: (0,) * nd)


def _drain(gen):
    for _ in gen:
        pass


def _interleave(main, fillers):
    for _ in main:
        for f in fillers:
            next(f, None)
    for f in fillers:
        _drain(f)


def _rwkv_pre_kernel(seq_tiles, sub, x_ref, xp_ref, g_ref, mu_ref, wr_ref, wk_ref, wv_ref,
                     w0_ref, w1_ref, w2_ref, a0_ref, a1_ref, a2_ref, g1_ref, g2_ref,
                     r_out, k_out, v_out, lw_out, a_out, g_out):
    i = pl.program_id(0)
    n_sub = x_ref.shape[0] // sub
    g = g_ref[...]
    mu = mu_ref[...]
    rows_of = lambda j: slice(j * sub, (j + 1) * sub)
    hp = _rms(xp_ref[...], g)
    carry_row = {0: jnp.where(i % seq_tiles == 0, 0.0, hp[7:8, :])}

    def mix(j, out):
        h = _rms(x_ref[rows_of(j), :], g)
        carry_row[j + 1] = h[sub - 1:sub, :]
        yield
        row = lax.broadcasted_iota(jnp.int32, h.shape, 0)
        dx = jnp.where(row == 0, carry_row[j], pltpu.roll(h, 1, axis=0)) - h
        yield
        for n, name in enumerate(("r", "w", "k", "v", "a", "g")):
            out[name] = (h + dx * mu[n:n + 1]).astype(BF16)
            if n % 2 == 1:
                yield

    def project(j, xs):
        rows = rows_of(j)
        dw = _dot(xs["w"], w1_ref[...])
        da = _dot(xs["a"], a1_ref[...])
        dg = _dot(xs["g"], g1_ref[...])
        yield
        r_out[rows, :] = jnp.dot(xs["r"], wr_ref[...], preferred_element_type=F32)
        yield
        w_pre = w0_ref[...] + _dot(jnp.tanh(dw), w2_ref[...])
        a_pre = a0_ref[...] + _dot(da, a2_ref[...])
        g_out[rows, :] = _dot(jax.nn.sigmoid(dg), g2_ref[...])
        yield
        k_out[rows, :] = jnp.dot(xs["k"], wk_ref[...], preferred_element_type=F32)
        yield
        lw_out[rows, :] = jax.nn.sigmoid(w_pre) * (-math.exp(-0.5))
        a_out[rows, :] = jax.nn.sigmoid(a_pre)
        yield
        v_out[rows, :] = jnp.dot(xs["v"], wv_ref[...], preferred_element_type=F32)
        yield

    xs = [dict() for _ in range(n_sub)]
    _drain(mix(0, xs[0]))
    for j in range(n_sub):
        fillers = [mix(j + 1, xs[j + 1])] if j + 1 < n_sub else []
        _interleave(project(j, xs[j]), fillers)


def _rwkv_pre(x2d, seq_len, g, mu, wr, wk, wv, w0, w1, w2, a0, a1, a2, g1, g2, tm=512, sub=256):
    t, d = x2d.shape
    seq_tiles = seq_len // tm
    row_spec = pl.BlockSpec((tm, d), lambda i: (i, 0))
    prev_spec = pl.BlockSpec((8, d), lambda i: (jnp.maximum(i * (tm // 8) - 1, 0), 0))
    consts = (g, mu, wr, wk, wv, w0, w1, w2, a0, a1, a2, g1, g2)
    out = jax.ShapeDtypeStruct((t, d), F32)
    return pl.pallas_call(
        functools.partial(_rwkv_pre_kernel, seq_tiles, sub),
        out_shape=(out,) * 6,
        grid=(t // tm,),
        in_specs=[row_spec, prev_spec] + [_const_spec(c.shape) for c in consts],
        out_specs=(row_spec,) * 6,
        compiler_params=pltpu.CompilerParams(
            dimension_semantics=("parallel",), vmem_limit_bytes=VMEM_LIMIT),
        name="rwkv_pre",
    )(x2d, x2d, *consts)


def _cumsum_rows(x):
    n = x.shape[0]
    row = lax.broadcasted_iota(jnp.int32, x.shape, 0)
    s = 1
    while s < n:
        x = x + jnp.where(row >= s, pltpu.roll(x, s, axis=0), 0.0)
        s *= 2
    return x


INV_BASE = 8
GROUP_HEADS = 4
GROUP_W = GROUP_HEADS * HEAD


def _head_sum(x):
    low = lax.broadcasted_iota(jnp.int32, (x.shape[0], LANES), 1) < HEAD
    outs = []
    for c in range(x.shape[1] // LANES):
        xc = x[:, c * LANES:(c + 1) * LANES]
        s_lo = jnp.sum(jnp.where(low, xc, 0.0), axis=-1, keepdims=True)
        s_hi = jnp.sum(jnp.where(low, 0.0, xc), axis=-1, keepdims=True)
        outs.append(jnp.where(low, s_lo, s_hi))
    return jnp.concatenate(outs, axis=1)


def _wkv_kernel(chunks, n_w, r_ref, k_ref, v_ref, lw_ref, a_ref,
                kk_ref, ka_ref, rk_ref, lnw_ref, lnb_ref, *rest):
    o_ref, state_ref = rest[n_w], rest[-1]
    w_refs = rest[:n_w] + rest[n_w + 1:-1]
    n_batch = r_ref.shape[0]
    n_groups = D_MODEL // GROUP_W
    probs = [(b, g) for b in range(n_batch) for g in range(n_groups)]
    ps = range(len(probs))

    @pl.when(pl.program_id(0) == 0)
    def _():
        state_ref[...] = jnp.zeros_like(state_ref)

    ri = lax.broadcasted_iota(jnp.int32, (CHUNK, GROUP_W), 0)
    lane = lax.broadcasted_iota(jnp.int32, (CHUNK, GROUP_W), 1)
    cj = lane & (HEAD - 1)
    lane_head = lax.shift_right_logical(lane, HEAD.bit_length() - 1)
    head_masks = [lane_head == h for h in range(GROUP_HEADS)]
    blk = lambda idx, size: lax.shift_right_logical(idx, size.bit_length() - 1)
    strict = cj < ri
    incl = cj <= ri
    eye = (cj == ri).astype(F32)
    diag = blk(ri, INV_BASE) == blk(cj, INV_BASE)
    bands = []
    size = INV_BASE
    while size < CHUNK:
        bands.append((blk(ri, 2 * size) == blk(cj, 2 * size))
                     & (blk(ri, size) != blk(cj, size)))
        size *= 2
    sr = lax.broadcasted_iota(jnp.int32, (GROUP_W, GROUP_W), 0)
    sc = lax.broadcasted_iota(jnp.int32, (GROUP_W, GROUP_W), 1)
    state_mask = blk(sr, HEAD) == blk(sc, HEAD)

    def bd(y):
        return jnp.concatenate([jnp.where(m, y, 0.0) for m in head_masks], axis=0).astype(BF16)

    def mm(x, w):
        return jnp.dot(x.astype(BF16), w, preferred_element_type=F32)

    def mm_nt(x, w):
        return lax.dot_general(x.astype(BF16), w, (((1,), (1,)), ((), ())),
                               preferred_element_type=F32)

    cols = [slice(g * GROUP_W, (g + 1) * GROUP_W) for g in range(n_groups)]
    rows_of = lambda c: slice(c * CHUNK, (c + 1) * CHUNK)


    def prepare(c, b, out):
        rows = rows_of(c)
        lw = lw_ref[b, rows, :]
        linc = _cumsum_rows(lw)
        yield
        p_inc = jnp.exp(linc)
        p_exc = jnp.exp(linc - lw)
        p_inv = jnp.exp(-linc)
        yield
        k = k_ref[b, rows, :]
        a = a_ref[b, rows, :]
        kk = k * kk_ref[...]
        kk = kk * lax.rsqrt(jnp.maximum(_head_sum(kk * kk), 1e-24))
        yield
        kmod = k * (1.0 + (a - 1.0) * ka_ref[...])
        at_f = (-kk * p_exc).astype(BF16)
        rt_f = (r_ref[b, rows, :] * p_inc).astype(BF16)
        yield
        out["lhs"] = [jnp.concatenate([at_f[:, cl], rt_f[:, cl]], axis=0) for cl in cols]
        out["bt"] = kk * a * p_inv
        out["kt"] = kmod * p_inv
        out["kmod"] = kmod
        out["p_end"] = p_inc[CHUNK - 1:CHUNK, :]
        yield

    def matmuls(c, ops, out):
        v = [v_ref[b, rows_of(c), :] for b in range(n_batch)]
        lhs = [ops[b]["lhs"][g] for b, g in probs]
        bt = [ops[b]["bt"][:, cols[g]] for b, g in probs]
        kt = [ops[b]["kt"][:, cols[g]] for b, g in probs]
        vg = [v[b][:, cols[g]] for b, g in probs]
        ab = [mm_nt(lhs[p], bd(bt[p])) for p in ps]
        ak = [mm_nt(lhs[p], bd(kt[p])) for p in ps]
        yield
        a_ab = [jnp.where(strict, x[:CHUNK], 0.0) for x in ab]
        a_rb = [jnp.where(incl, x[CHUNK:], 0.0) for x in ab]
        a_k = [jnp.concatenate([jnp.where(strict, x[:CHUNK], 0.0),
                                jnp.where(incl, x[CHUNK:], 0.0)], axis=0) for x in ak]
        d1 = [jnp.where(diag, x, 0.0) for x in a_ab]
        d2 = [mm(d1[p], bd(d1[p])) for p in ps]
        yield
        t = [eye + x for x in d1]
        td = [mm(jnp.concatenate([t[p], d2[p]], axis=0), bd(d2[p])) for p in ps]
        yield
        t = [t[p] + td[p][:CHUNK] for p in ps]
        t = [t[p] + mm(t[p], bd(td[p][CHUNK:])) for p in ps]
        yield
        for band in bands:
            te = [mm(t[p], bd(jnp.where(band, a_ab[p], 0.0))) for p in ps]
            yield
            t = [t[p] + mm(te[p], bd(t[p])) for p in ps]
            yield
        s0 = [state_ref[p] for p in ps]
        zy = [mm_nt(lhs[p], s0[p].astype(BF16)) for p in ps]
        av = [mm(a_k[p], bd(vg[p])) for p in ps]
        yield
        u = [mm(t[p], bd(zy[p][:CHUNK] + av[p][:CHUNK])) for p in ps]
        yield
        y = [zy[p][CHUNK:] + av[p][CHUNK:] + mm(a_rb[p], bd(u[p])) for p in ps]
        for p, (b, g) in enumerate(probs):
            uv = jnp.concatenate([u[p], vg[p]], axis=0)
            bk = jnp.concatenate([bt[p], kt[p]], axis=0)
            state_ref[p] = (jnp.where(state_mask, s0[p] + _dot_tn(uv, bk), 0.0)
                            * ops[b]["p_end"][:, cols[g]])
        out["y"] = [jnp.concatenate(y[b * n_groups:(b + 1) * n_groups], axis=1)
                    for b in range(n_batch)]
        yield

    def finish(c, b, ops, y_f):
        rows = rows_of(c)
        inv_n = 1.0 / HEAD
        yc = y_f - _head_sum(y_f) * inv_n
        yield
        var = _head_sum(yc * yc) * inv_n
        yn = yc * lax.rsqrt(var + GN_EPS) * lnw_ref[...] + lnb_ref[...]
        yield
        bonus = _head_sum(r_ref[b, rows, :] * ops["kmod"] * rk_ref[...])
        o_ref[b, rows, :] = yn + bonus * v_ref[b, rows, :]
        yield

    def cast_weights():
        for src, dst in zip(w_refs[:len(w_refs) // 2], w_refs[len(w_refs) // 2:]):
            dst[...] = src[...].astype(dst.dtype)
            yield

    bs = range(n_batch)
    ops = [[dict() for _ in bs] for _ in range(chunks)]
    res = [dict() for _ in range(chunks)]
    for b in bs:
        _drain(prepare(0, b, ops[0][b]))
    for c in range(chunks):
        fillers = [cast_weights()] if c == 0 else []
        if c + 1 < chunks:
            fillers += [prepare(c + 1, b, ops[c + 1][b]) for b in bs]
        if c > 0:
            fillers += [finish(c - 1, b, ops[c - 1][b], res[c - 1]["y"][b]) for b in bs]
        _interleave(matmuls(c, ops[c], res[c]), fillers)
    for b in bs:
        _drain(finish(chunks - 1, b, ops[chunks - 1][b], res[chunks - 1]["y"][b]))


def _wkv(r, k, v, lw, a, kk_p, ka_p, rk_p, lnw_p, lnb_p, weights, rows=256):
    b, s, d = r.shape
    steps = s // rows
    seq_spec = pl.BlockSpec((b, rows, d), lambda ci: (0, ci, 0))
    par_spec = pl.BlockSpec((1, d), lambda ci: (0, 0))
    w_specs = [pl.BlockSpec((w.shape[0] // steps, w.shape[1]), lambda ci: (ci, 0))
               for w in weights]
    w_out = [jax.ShapeDtypeStruct(w.shape, BF16) for w in weights]
    outs = pl.pallas_call(
        functools.partial(_wkv_kernel, rows // CHUNK, len(weights)),
        out_shape=[jax.ShapeDtypeStruct((b, s, d), F32)] + w_out,
        grid=(steps,),
        in_specs=[seq_spec] * 5 + [par_spec] * 5 + w_specs,
        out_specs=[seq_spec] + w_specs,
        scratch_shapes=[pltpu.VMEM((b * d // GROUP_W, GROUP_W, GROUP_W), F32)],
        compiler_params=pltpu.CompilerParams(
            dimension_semantics=("arbitrary",), vmem_limit_bytes=VMEM_LIMIT),
        name="wkv",
    )(r, k, v, lw, a, kk_p, ka_p, rk_p, lnw_p, lnb_p, *weights)
    return outs[0], outs[1:]


def _proj_mlp_kernel(has_gate, final_norm, ff_chunk, sub, *refs):
    refs = list(refs)
    y_ref = refs.pop(0)
    gate_ref = refs.pop(0) if has_gate else None
    x_ref, wo_ref, bo_ref, gm_ref, win_ref, wout_ref = refs[:6]
    refs = refs[6:]
    gf_ref = refs.pop(0) if final_norm else None
    o_ref = refs.pop(0)

    n_sub = x_ref.shape[0] // sub
    rows_of = lambda j: slice(j * sub, (j + 1) * sub)

    def head(j, out):
        rows = rows_of(j)
        y = y_ref[rows, :].astype(F32)
        if has_gate:
            y = y * gate_ref[rows, :]
        yield
        x1 = x_ref[rows, :] + _dot(y, wo_ref[...]) + bo_ref[...]
        yield
        out["x1"] = x1
        out["hb"] = _rms(x1, gm_ref[...]).astype(BF16)
        yield

    def mlp(j, ins, out):
        acc = ins["x1"]
        for c in range(D_FF // ff_chunk):
            cols = slice(c * ff_chunk, (c + 1) * ff_chunk)
            hid = jnp.maximum(jnp.dot(ins["hb"], win_ref[:, cols], preferred_element_type=F32), 0.0)
            yield
            acc = acc + _dot(hid * hid, wout_ref[cols, :])
            yield
        out["acc"] = acc

    def tail(j, acc):
        if final_norm:
            acc = _rms(acc, gf_ref[...])
        o_ref[rows_of(j), :] = acc
        yield

    ins = [dict() for _ in range(n_sub)]
    outs = [dict() for _ in range(n_sub)]
    _drain(head(0, ins[0]))
    for j in range(n_sub):
        fillers = []
        if j + 1 < n_sub:
            fillers.append(head(j + 1, ins[j + 1]))
        if j > 0:
            fillers.append(tail(j - 1, outs[j - 1]["acc"]))
        _interleave(mlp(j, ins[j], outs[j]), fillers)
    _drain(tail(n_sub - 1, outs[n_sub - 1]["acc"]))


def _proj_mlp(y, gate, x, wo, bo, gm, win_all, wout_all, layer, gf, tm=512, sub=256,
              ff_chunk=1024):
    t, d = x.shape
    row_spec = pl.BlockSpec((tm, d), lambda i: (i, 0))
    args = [y] + ([gate] if gate is not None else []) + [x, wo, bo, gm, win_all, wout_all]
    resident = lambda shape, blk: pl.BlockSpec(shape, lambda i: (blk, 0),
                                               pipeline_mode=pl.Buffered(1))
    specs = [row_spec] * (len(args) - 5) + [
        resident(wo.shape, 0), resident(bo.shape, 0), resident(gm.shape, 0),
        resident((d, D_FF), layer), resident((D_FF, d), layer)]
    if gf is not None:
        args.append(gf)
        specs.append(_const_spec(gf.shape))
    return pl.pallas_call(
        functools.partial(_proj_mlp_kernel, gate is not None, gf is not None, ff_chunk, sub),
        out_shape=jax.ShapeDtypeStruct((t, d), F32),
        grid=(t // tm,),
        in_specs=specs,
        out_specs=row_spec,
        compiler_params=pltpu.CompilerParams(
            dimension_semantics=("parallel",), vmem_limit_bytes=VMEM_LIMIT),
        name="proj_mlp",
    )(*args)


def _rope_tab_kernel(pos_ref, freq_ref, cos_ref, sin_ref):
    ang = pos_ref[...] * freq_ref[...]
    lane = lax.broadcasted_iota(jnp.int32, ang.shape, 1)
    cos_ref[...] = jnp.cos(ang)
    sin_ref[...] = jnp.where(lane % HEAD < HEAD // 2, -1.0, 1.0) * jnp.sin(ang)


def _rope_tab(pos_col, freq_row, tm=1024):
    t = pos_col.shape[0]
    out = jax.ShapeDtypeStruct((t, LANES), F32)
    spec = pl.BlockSpec((tm, LANES), lambda i: (i, 0))
    return pl.pallas_call(
        _rope_tab_kernel,
        out_shape=(out, out),
        grid=(t // tm,),
        in_specs=[pl.BlockSpec((tm, 1), lambda i: (i, 0)), _const_spec(freq_row.shape)],
        out_specs=(spec, spec),
        compiler_params=pltpu.CompilerParams(dimension_semantics=("parallel",)),
        name="rope_tab",
    )(pos_col, freq_row)


def _attn_qkv_kernel(sub, x_ref, g_ref, w_ref, b_ref, cos_ref, sin_ref, o_ref):
    n_sub = x_ref.shape[0] // sub
    rows_of = lambda j: slice(j * sub, (j + 1) * sub)
    lane = lax.broadcasted_iota(jnp.int32, (sub, LANES), 1)
    first_half = lane % HEAD < HEAD // 2
    n_q = N_HEADS * HEAD // LANES
    n_rot = (N_HEADS + N_KV_HEADS) * HEAD // LANES
    n_blk = 2 * LANES

    def norm(j, out):
        out["h"] = _rms(x_ref[rows_of(j), :], g_ref[...]).astype(BF16)
        yield

    def project(j, ins, out):
        out["qkv"] = []
        for c in range(QKV_DIM // n_blk):
            cols = slice(c * n_blk, (c + 1) * n_blk)
            out["qkv"].append(jnp.dot(ins["h"], w_ref[:, cols], preferred_element_type=F32)
                              + b_ref[:, cols])
            yield

    def rotary(j, qkv):
        rows = rows_of(j)
        cos = cos_ref[rows, :]
        sin = sin_ref[rows, :]
        for s in range(QKV_DIM // LANES):
            c, half = divmod(s, n_blk // LANES)
            blk = qkv[c][:, half * LANES:(half + 1) * LANES]
            if s < n_rot:
                rot = jnp.where(first_half, pltpu.roll(blk, LANES - HEAD // 2, axis=1),
                                pltpu.roll(blk, HEAD // 2, axis=1))
                blk = blk * cos + rot * sin
            if s < n_q:
                blk = blk * Q_SCALE
            o_ref[rows, s * LANES:(s + 1) * LANES] = blk.astype(o_ref.dtype)
            if s % 2 == 1:
                yield

    ins = [dict() for _ in range(n_sub)]
    outs = [dict() for _ in range(n_sub)]
    _drain(norm(0, ins[0]))
    for j in range(n_sub):
        fillers = []
        if j + 1 < n_sub:
            fillers.append(norm(j + 1, ins[j + 1]))
        if j > 0:
            fillers.append(rotary(j - 1, outs[j - 1]["qkv"]))
        _interleave(project(j, ins[j], outs[j]), fillers)
    _drain(rotary(n_sub - 1, outs[n_sub - 1]["qkv"]))


def _attn_qkv(x, g, w, b, cos, sin, tm=512, sub=256):
    t, d = x.shape
    row = lambda width: pl.BlockSpec((tm, width), lambda i: (i, 0))
    return pl.pallas_call(
        functools.partial(_attn_qkv_kernel, sub),
        out_shape=jax.ShapeDtypeStruct((t, QKV_DIM), BF16),
        grid=(t // tm,),
        in_specs=[row(d), _const_spec(g.shape), _const_spec(w.shape), _const_spec(b.shape),
                  row(LANES), row(LANES)],
        out_specs=row(QKV_DIM),
        compiler_params=pltpu.CompilerParams(
            dimension_semantics=("parallel",), vmem_limit_bytes=VMEM_LIMIT),
        name="attn_qkv",
    )(x, g, w, b, cos, sin)


def _attn_core_kernel(q_blocks, q_ref, kp_ref, kc_ref, vp_ref, vc_ref, sink_ref, o_ref):
    n = pl.program_id(1)
    kr = lax.broadcasted_iota(jnp.int32, (2 * WINDOW, WINDOW), 0)
    qc = lax.broadcasted_iota(jnp.int32, (2 * WINDOW, WINDOW), 1)
    own = kr >= WINDOW
    band = (own & (kr - WINDOW <= qc)) | (jnp.logical_not(own) & (kr > qc))
    first_band = band & (own | (n > 0))
    neg = -jnp.inf
    probs = [(i, j) for i in range(q_blocks) for j in range(N_KV_HEADS)]
    blk = lambda i: slice(i * WINDOW, (i + 1) * WINDOW)
    kvl = lambda j: slice(j * HEAD, (j + 1) * HEAD)
    head = lambda j, g: j * GROUP + g

    def keys(prev_ref, cur_ref, i, j):
        prev = prev_ref[0, :, kvl(j)] if i == 0 else cur_ref[0, blk(i - 1), kvl(j)]
        return jnp.concatenate([prev, cur_ref[0, blk(i), kvl(j)]], axis=0)

    q = [jnp.concatenate([q_ref[0, blk(i), head(j, g) * HEAD:(head(j, g) + 1) * HEAD]
                          for g in range(GROUP)], axis=0) for i, j in probs]
    s_t = [_dot_nt(keys(kp_ref, kc_ref, i, j), q[p]) for p, (i, j) in enumerate(probs)]
    p_t = [[] for _ in probs]
    for g in range(GROUP):
        sink = [sink_ref[:, head(j, g):head(j, g) + 1] * LOG2E for _, j in probs]
        x = [jnp.where(first_band if i == 0 else band, s_t[p][:, blk(g)], neg)
             for p, (i, j) in enumerate(probs)]
        m = [jnp.maximum(jnp.max(x[p], axis=0, keepdims=True), sink[p])
             for p in range(len(probs))]
        e = [jnp.exp2(x[p] - m[p]) for p in range(len(probs))]
        den = [jnp.sum(e[p], axis=0, keepdims=True) + jnp.exp2(sink[p] - m[p])
               for p in range(len(probs))]
        for p in range(len(probs)):
            p_t[p].append((e[p] * (1.0 / den[p])).astype(BF16))
    for p, (i, j) in enumerate(probs):
        o = lax.dot_general(jnp.concatenate(p_t[p], axis=1), keys(vp_ref, vc_ref, i, j),
                            (((0,), (0,)), ((), ())), preferred_element_type=F32)
        for g in range(GROUP):
            o_ref[0, blk(i), head(j, g) * HEAD:(head(j, g) + 1) * HEAD] = (
                o[blk(g)].astype(o_ref.dtype))


def _attn_core(qkv, sinks, q_blocks=8):
    b, s, _ = qkv.shape
    nq = N_HEADS * HEAD
    tq = q_blocks * WINDOW
    k_blk = nq // LANES
    v_blk = k_blk + N_KV_HEADS * HEAD // LANES
    q_spec = pl.BlockSpec((1, tq, nq), lambda bi, n: (bi, n, 0))
    cur = lambda blk: pl.BlockSpec((1, tq, LANES), lambda bi, n: (bi, n, blk))
    prev = lambda blk: pl.BlockSpec((1, WINDOW, LANES),
                                    lambda bi, n: (bi, jnp.maximum(n * q_blocks - 1, 0), blk))
    return pl.pallas_call(
        functools.partial(_attn_core_kernel, q_blocks),
        out_shape=jax.ShapeDtypeStruct((b, s, nq), BF16),
        grid=(b, s // tq),
        in_specs=[q_spec, prev(k_blk), cur(k_blk), prev(v_blk), cur(v_blk),
                  _const_spec(sinks.shape)],
        out_specs=q_spec,
        compiler_params=pltpu.CompilerParams(
            dimension_semantics=("parallel", "arbitrary"), vmem_limit_bytes=VMEM_LIMIT),
        name="attn_core",
    )(qkv, qkv, qkv, qkv, qkv, sinks)


def kernel(x, positions, norm_mix_g, norm_mlp_g, norm_final_g, rwkv_mu, rwkv_w_r, rwkv_w_k, rwkv_w_v, rwkv_w_o, rwkv_w0, rwkv_w1, rwkv_w2, rwkv_a0, rwkv_a1, rwkv_a2, rwkv_g1, rwkv_g2, rwkv_k_k, rwkv_k_a, rwkv_r_k, rwkv_ln_w, rwkv_ln_b, attn_w_qkv, attn_b_qkv, attn_sinks, attn_w_o, attn_b_o, mlp_w_in, mlp_w_out):
    b, s, d = x.shape
    t = b * s
    x2d = x.reshape(t, d)
    row = lambda p: p.reshape(1, -1)
    bf = lambda w: w.astype(BF16)
    zero_bias = jnp.zeros((1, d), F32)

    r, k, v, lw, a, gate = _rwkv_pre(
        x2d, s, row(norm_mix_g[0]), rwkv_mu[0], bf(rwkv_w_r[0]), bf(rwkv_w_k[0]),
        bf(rwkv_w_v[0]), row(rwkv_w0[0]), bf(rwkv_w1[0]), bf(rwkv_w2[0]), row(rwkv_a0[0]),
        bf(rwkv_a1[0]), bf(rwkv_a2[0]), bf(rwkv_g1[0]), bf(rwkv_g2[0]))
    seq = lambda z: z.reshape(b, s, d)
    later_weights = [rwkv_w_o[0], attn_w_qkv[0], attn_w_o[0],
                     mlp_w_in.reshape(-1, D_FF), mlp_w_out.reshape(-1, d)]
    y, (w_o0, w_qkv, w_o1, w_in_all, w_out_all) = _wkv(
        seq(r), seq(k), seq(v), seq(lw), seq(a), row(rwkv_k_k[0]), row(rwkv_k_a[0]),
        row(rwkv_r_k[0]), row(rwkv_ln_w[0]), row(rwkv_ln_b[0]), later_weights)
    x2d = _proj_mlp(y.reshape(t, d), gate, x2d, w_o0, zero_bias,
                    row(norm_mlp_g[0]), w_in_all, w_out_all, 0, None)

    inv_freq = ROPE_THETA ** (-jnp.arange(0, HEAD, 2, dtype=F32) / HEAD)
    freq_row = jnp.tile(inv_freq, LANES // (HEAD // 2)).reshape(1, LANES)
    cos, sin = _rope_tab(positions.astype(F32).reshape(t, 1), freq_row)
    qkv = _attn_qkv(x2d, row(norm_mix_g[1]), w_qkv, row(attn_b_qkv[0]), cos, sin)
    o = _attn_core(qkv.reshape(b, s, QKV_DIM), row(attn_sinks[0]))
    out = _proj_mlp(o.reshape(t, d), None, x2d, w_o1, row(attn_b_o[0]),
                    row(norm_mlp_g[1]), w_in_all, w_out_all, 1, row(norm_final_g))
    return out.reshape(b, s, d)
```

```python
import functools
import math

import jax
import jax.numpy as jnp
from jax import lax
from jax.experimental import pallas as pl
from jax.experimental.pallas import tpu as pltpu

F32 = jnp.float32
BF16 = jnp.bfloat16

D_MODEL = 1024
HEAD = 64
N_HEADS = D_MODEL // HEAD
N_KV_HEADS = 2
GROUP = N_HEADS // N_KV_HEADS
WINDOW = 128
QKV_DIM = (N_HEADS + 2 * N_KV_HEADS) * HEAD
D_FF = 4 * D_MODEL
ROPE_THETA = 10000.0
RMS_EPS = 1e-5
GN_EPS = 64e-5

LANES = 128
ROPE_PACK = LANES // (HEAD // 2)
ROPE_SUB = 256
LOG2E = math.log2(math.e)
Q_SCALE = HEAD ** -0.5 * LOG2E
CHUNK = 64
VMEM_LIMIT = 56 * 1024 * 1024


def _rms(x, g):
    return x * lax.rsqrt(jnp.mean(x * x, axis=-1, keepdims=True) + RMS_EPS) * g


def _dot(a, b):
    return jnp.dot(a.astype(BF16), b.astype(BF16), preferred_element_type=F32)


def _dot_nt(a, b):
    return lax.dot_general(a.astype(BF16), b.astype(BF16), (((1,), (1,)), ((), ())),
                           preferred_element_type=F32)


def _dot_tn(a, b):
    return lax.dot_general(a.astype(BF16), b.astype(BF16), (((0,), (0,)), ((), ())),
                           preferred_element_type=F32)


def _const_spec(shape):
    nd = len(shape)
    return pl.BlockSpec(shape, lambda *_: (0,) * nd)


def _drain(gen):
    for _ in gen:
        pass


def _interleave(main, fillers):
    for _ in main:
        for f in fillers:
            next(f, None)
    for f in fillers:
        _drain(f)


def _rwkv_pre_kernel(seq_tiles, sub, x_ref, xp_ref, g_ref, mu_ref, wr_ref, wk_ref, wv_ref,
                     w0_ref, w1_ref, w2_ref, a0_ref, a1_ref, a2_ref, g1_ref, g2_ref,
                     r_out, k_out, v_out, lw_out, a_out, g_out):
    i = pl.program_id(0)
    n_sub = x_ref.shape[0] // sub
    g = g_ref[...]
    mu = mu_ref[...]
    rows_of = lambda j: slice(j * sub, (j + 1) * sub)
    hp = _rms(xp_ref[...], g)
    carry_row = {0: jnp.where(i % seq_tiles == 0, 0.0, hp[7:8, :])}

    def mix(j, out):
        h = _rms(x_ref[rows_of(j), :], g)
        carry_row[j + 1] = h[sub - 1:sub, :]
        yield
        row = lax.broadcasted_iota(jnp.int32, h.shape, 0)
        dx = jnp.where(row == 0, carry_row[j], pltpu.roll(h, 1, axis=0)) - h
        yield
        for n, name in enumerate(("r", "w", "k", "v", "a", "g")):
            out[name] = (h + dx * mu[n:n + 1]).astype(BF16)
            if n % 2 == 1:
                yield

    def project(j, xs):
        rows = rows_of(j)
        dw = _dot(xs["w"], w1_ref[...])
        da = _dot(xs["a"], a1_ref[...])
        dg = _dot(xs["g"], g1_ref[...])
        yield
        r_out[rows, :] = jnp.dot(xs["r"], wr_ref[...], preferred_element_type=F32)
        yield
        w_pre = w0_ref[...] + _dot(jnp.tanh(dw), w2_ref[...])
        a_pre = a0_ref[...] + _dot(da, a2_ref[...])
        g_out[rows, :] = _dot(jax.nn.sigmoid(dg), g2_ref[...])
        yield
        k_out[rows, :] = jnp.dot(xs["k"], wk_ref[...], preferred_element_type=F32)
        yield
        lw_out[rows, :] = jax.nn.sigmoid(w_pre) * (-math.exp(-0.5))
        a_out[rows, :] = jax.nn.sigmoid(a_pre)
        yield
        v_out[rows, :] = jnp.dot(xs["v"], wv_ref[...], preferred_element_type=F32)
        yield

    xs = [dict() for _ in range(n_sub)]
    _drain(mix(0, xs[0]))
    for j in range(n_sub):
        fillers = [mix(j + 1, xs[j + 1])] if j + 1 < n_sub else []
        _interleave(project(j, xs[j]), fillers)


def _rwkv_pre(x2d, seq_len, g, mu, wr, wk, wv, w0, w1, w2, a0, a1, a2, g1, g2, tm=512, sub=256):
    t, d = x2d.shape
    seq_tiles = seq_len // tm
    row_spec = pl.BlockSpec((tm, d), lambda i: (i, 0))
    prev_spec = pl.BlockSpec((8, d), lambda i: (jnp.maximum(i * (tm // 8) - 1, 0), 0))
    consts = (g, mu, wr, wk, wv, w0, w1, w2, a0, a1, a2, g1, g2)
    out = jax.ShapeDtypeStruct((t, d), F32)
    return pl.pallas_call(
        functools.partial(_rwkv_pre_kernel, seq_tiles, sub),
        out_shape=(out,) * 6,
        grid=(t // tm,),
        in_specs=[row_spec, prev_spec] + [_const_spec(c.shape) for c in consts],
        out_specs=(row_spec,) * 6,
        compiler_params=pltpu.CompilerParams(
            dimension_semantics=("parallel",), vmem_limit_bytes=VMEM_LIMIT),
        name="rwkv_pre",
    )(x2d, x2d, *consts)


def _cumsum_rows(x):
    n = x.shape[0]
    row = lax.broadcasted_iota(jnp.int32, x.shape, 0)
    s = 1
    while s < n:
        x = x + jnp.where(row >= s, pltpu.roll(x, s, axis=0), 0.0)
        s *= 2
    return x


INV_BASE = 8
GROUP_HEADS = 4
GROUP_W = GROUP_HEADS * HEAD


def _head_sum(x):
    low = lax.broadcasted_iota(jnp.int32, (x.shape[0], LANES), 1) < HEAD
    outs = []
    for c in range(x.shape[1] // LANES):
        xc = x[:, c * LANES:(c + 1) * LANES]
        s_lo = jnp.sum(jnp.where(low, xc, 0.0), axis=-1, keepdims=True)
        s_hi = jnp.sum(jnp.where(low, 0.0, xc), axis=-1, keepdims=True)
        outs.append(jnp.where(low, s_lo, s_hi))
    return jnp.concatenate(outs, axis=1)


def _wkv_kernel(chunks, n_w, r_ref, k_ref, v_ref, lw_ref, a_ref,
                kk_ref, ka_ref, rk_ref, lnw_ref, lnb_ref, *rest):
    o_ref, state_ref = rest[n_w], rest[-1]
    w_refs = rest[:n_w] + rest[n_w + 1:-1]
    n_batch = r_ref.shape[0]
    n_groups = D_MODEL // GROUP_W
    probs = [(b, g) for b in range(n_batch) for g in range(n_groups)]
    ps = range(len(probs))

    @pl.when(pl.program_id(0) == 0)
    def _():
        state_ref[...] = jnp.zeros_like(state_ref)

    ri = lax.broadcasted_iota(jnp.int32, (CHUNK, GROUP_W), 0)
    lane = lax.broadcasted_iota(jnp.int32, (CHUNK, GROUP_W), 1)
    cj = lane & (HEAD - 1)
    lane_head = lax.shift_right_logical(lane, HEAD.bit_length() - 1)
    head_masks = [lane_head == h for h in range(GROUP_HEADS)]
    blk = lambda idx, size: lax.shift_right_logical(idx, size.bit_length() - 1)
    strict = cj < ri
    incl = cj <= ri
    eye = (cj == ri).astype(F32)
    diag = blk(ri, INV_BASE) == blk(cj, INV_BASE)
    bands = []
    size = INV_BASE
    while size < CHUNK:
        bands.append((blk(ri, 2 * size) == blk(cj, 2 * size))
                     & (blk(ri, size) != blk(cj, size)))
        size *= 2
    sr = lax.broadcasted_iota(jnp.int32, (GROUP_W, GROUP_W), 0)
    sc = lax.broadcasted_iota(jnp.int32, (GROUP_W, GROUP_W), 1)
    state_mask = blk(sr, HEAD) == blk(sc, HEAD)

    def bd(y):
        return jnp.concatenate([jnp.where(m, y, 0.0) for m in head_masks], axis=0).astype(BF16)

    def mm(x, w):
        return jnp.dot(x.astype(BF16), w, preferred_element_type=F32)

    def mm_nt(x, w):
        return lax.dot_general(x.astype(BF16), w, (((1,), (1,)), ((), ())),
                               preferred_element_type=F32)

    cols = [slice(g * GROUP_W, (g + 1) * GROUP_W) for g in range(n_groups)]
    rows_of = lambda c: slice(c * CHUNK, (c + 1) * CHUNK)


    def prepare(c, b, out):
        rows = rows_of(c)
        lw = lw_ref[b, rows, :]
        linc = _cumsum_rows(lw)
        yield
        p_inc = jnp.exp(linc)
        p_exc = jnp.exp(linc - lw)
        p_inv = jnp.exp(-linc)
        yield
        k = k_ref[b, rows, :]
        a = a_ref[b, rows, :]
        kk = k * kk_ref[...]
        kk = kk * lax.rsqrt(jnp.maximum(_head_sum(kk * kk), 1e-24))
        yield
        kmod = k * (1.0 + (a - 1.0) * ka_ref[...])
        at_f = (-kk * p_exc).astype(BF16)
        rt_f = (r_ref[b, rows, :] * p_inc).astype(BF16)
        yield
        out["lhs"] = [jnp.concatenate([at_f[:, cl], rt_f[:, cl]], axis=0) for cl in cols]
        out["bt"] = kk * a * p_inv
        out["kt"] = kmod * p_inv
        out["kmod"] = kmod
        out["p_end"] = p_inc[CHUNK - 1:CHUNK, :]
        yield

    def matmuls(c, ops, out):
        v = [v_ref[b, rows_of(c), :] for b in range(n_batch)]
        lhs = [ops[b]["lhs"][g] for b, g in probs]
        bt = [ops[b]["bt"][:, cols[g]] for b, g in probs]
        kt = [ops[b]["kt"][:, cols[g]] for b, g in probs]
        vg = [v[b][:, cols[g]] for b, g in probs]
        ab = [mm_nt(lhs[p], bd(bt[p])) for p in ps]
        ak = [mm_nt(lhs[p], bd(kt[p])) for p in ps]
        yield
        a_ab = [jnp.where(strict, x[:CHUNK], 0.0) for x in ab]
        a_rb = [jnp.where(incl, x[CHUNK:], 0.0) for x in ab]
        a_k = [jnp.concatenate([jnp.where(strict, x[:CHUNK], 0.0),
                                jnp.where(incl, x[CHUNK:], 0.0)], axis=0) for x in ak]
        d1 = [jnp.where(diag, x, 0.0) for x in a_ab]
        d2 = [mm(d1[p], bd(d1[p])) for p in ps]
        yield
        t = [eye + x for x in d1]
        td = [mm(jnp.concatenate([t[p], d2[p]], axis=0), bd(d2[p])) for p in ps]
        yield
        t = [t[p] + td[p][:CHUNK] for p in ps]
        t = [t[p] + mm(t[p], bd(td[p][CHUNK:])) for p in ps]
        yield
        for band in bands:
            te = [mm(t[p], bd(jnp.where(band, a_ab[p], 0.0))) for p in ps]
            yield
            t = [t[p] + mm(te[p], bd(t[p])) for p in ps]
            yield
        s0 = [state_ref[p] for p in ps]
        zy = [mm_nt(lhs[p], s0[p].astype(BF16)) for p in ps]
        av = [mm(a_k[p], bd(vg[p])) for p in ps]
        yield
        u = [mm(t[p], bd(zy[p][:CHUNK] + av[p][:CHUNK])) for p in ps]
        yield
        y = [zy[p][CHUNK:] + av[p][CHUNK:] + mm(a_rb[p], bd(u[p])) for p in ps]
        for p, (b, g) in enumerate(probs):
            uv = jnp.concatenate([u[p], vg[p]], axis=0)
            bk = jnp.concatenate([bt[p], kt[p]], axis=0)
            state_ref[p] = (jnp.where(state_mask, s0[p] + _dot_tn(uv, bk), 0.0)
                            * ops[b]["p_end"][:, cols[g]])
        out["y"] = [jnp.concatenate(y[b * n_groups:(b + 1) * n_groups], axis=1)
                    for b in range(n_batch)]
        yield

    def finish(c, b, ops, y_f):
        rows = rows_of(c)
        inv_n = 1.0 / HEAD
        yc = y_f - _head_sum(y_f) * inv_n
        yield
        var = _head_sum(yc * yc) * inv_n
        yn = yc * lax.rsqrt(var + GN_EPS) * lnw_ref[...] + lnb_ref[...]
        yield
        bonus = _head_sum(r_ref[b, rows, :] * ops["kmod"] * rk_ref[...])
        o_ref[b, rows, :] = yn + bonus * v_ref[b, rows, :]
        yield

    def cast_weights():
        for src, dst in zip(w_refs[:len(w_refs) // 2], w_refs[len(w_refs) // 2:]):
            dst[...] = src[...].astype(dst.dtype)
            yield

    bs = range(n_batch)
    ops = [[dict() for _ in bs] for _ in range(chunks)]
    res = [dict() for _ in range(chunks)]
    for b in bs:
        _drain(prepare(0, b, ops[0][b]))
    for c in range(chunks):
        fillers = [cast_weights()] if c == 0 else []
        if c + 1 < chunks:
            fillers += [prepare(c + 1, b, ops[c + 1][b]) for b in bs]
        if c > 0:
            fillers += [finish(c - 1, b, ops[c - 1][b], res[c - 1]["y"][b]) for b in bs]
        _interleave(matmuls(c, ops[c], res[c]), fillers)
    for b in bs:
        _drain(finish(chunks - 1, b, ops[chunks - 1][b], res[chunks - 1]["y"][b]))


def _wkv(r, k, v, lw, a, kk_p, ka_p, rk_p, lnw_p, lnb_p, weights, rows=256):
    b, s, d = r.shape
    steps = s // rows
    seq_spec = pl.BlockSpec((b, rows, d), lambda ci: (0, ci, 0))
    par_spec = pl.BlockSpec((1, d), lambda ci: (0, 0))
    w_specs = [pl.BlockSpec((w.shape[0] // steps, w.shape[1]), lambda ci: (ci, 0))
               for w in weights]
    w_out = [jax.ShapeDtypeStruct(w.shape, BF16) for w in weights]
    outs = pl.pallas_call(
        functools.partial(_wkv_kernel, rows // CHUNK, len(weights)),
        out_shape=[jax.ShapeDtypeStruct((b, s, d), F32)] + w_out,
        grid=(steps,),
        in_specs=[seq_spec] * 5 + [par_spec] * 5 + w_specs,
        out_specs=[seq_spec] + w_specs,
        scratch_shapes=[pltpu.VMEM((b * d // GROUP_W, GROUP_W, GROUP_W), F32)],
        compiler_params=pltpu.CompilerParams(
            dimension_semantics=("arbitrary",), vmem_limit_bytes=VMEM_LIMIT),
        name="wkv",
    )(r, k, v, lw, a, kk_p, ka_p, rk_p, lnw_p, lnb_p, *weights)
    return outs[0], outs[1:]


def _proj_mlp_kernel(has_gate, final_norm, with_rope, ff_chunk, sub, *refs):
    refs = list(refs)
    y_ref = refs.pop(0)
    gate_ref = refs.pop(0) if has_gate else None
    x_ref, wo_ref, bo_ref, gm_ref, win_ref, wout_ref = refs[:6]
    refs = refs[6:]
    gf_ref = refs.pop(0) if final_norm else None
    if with_rope:
        pos_ref, freq_ref = refs.pop(0), refs.pop(0)
    o_ref = refs.pop(0)
    if with_rope:
        cos_ref, sin_ref = refs.pop(0), refs.pop(0)

    def rope_tables(part, parts, pieces=2):
        rows = pos_ref.shape[0] // (parts * pieces)
        for n in range(part * pieces, (part + 1) * pieces):
            sl = slice(n * rows, (n + 1) * rows)
            ang = pos_ref[sl, :] * freq_ref[...]
            cos_ref[sl, :] = jnp.cos(ang)
            sin_ref[sl, :] = jnp.sin(ang)
            yield

    n_sub = x_ref.shape[0] // sub
    rows_of = lambda j: slice(j * sub, (j + 1) * sub)

    def head(j, out):
        rows = rows_of(j)
        y = y_ref[rows, :].astype(F32)
        if has_gate:
            y = y * gate_ref[rows, :]
        yield
        x1 = x_ref[rows, :] + _dot(y, wo_ref[...]) + bo_ref[...]
        yield
        out["x1"] = x1
        out["hb"] = _rms(x1, gm_ref[...]).astype(BF16)
        yield

    def mlp(j, ins, out):
        acc = ins["x1"]
        for c in range(D_FF // ff_chunk):
            cols = slice(c * ff_chunk, (c + 1) * ff_chunk)
            hid = jnp.maximum(jnp.dot(ins["hb"], win_ref[:, cols], preferred_element_type=F32), 0.0)
            yield
            acc = acc + _dot(hid * hid, wout_ref[cols, :])
            yield
        out["acc"] = acc

    def tail(j, acc):
        if final_norm:
            acc = _rms(acc, gf_ref[...])
        o_ref[rows_of(j), :] = acc
        yield

    ins = [dict() for _ in range(n_sub)]
    outs = [dict() for _ in range(n_sub)]
    _drain(head(0, ins[0]))
    for j in range(n_sub):
        fillers = []
        if j + 1 < n_sub:
            fillers.append(head(j + 1, ins[j + 1]))
        if j > 0:
            fillers.append(tail(j - 1, outs[j - 1]["acc"]))
        if with_rope:
            fillers.append(rope_tables(j, n_sub))
        _interleave(mlp(j, ins[j], outs[j]), fillers)
    _drain(tail(n_sub - 1, outs[n_sub - 1]["acc"]))


def _proj_mlp(y, gate, x, wo, bo, gm, win_all, wout_all, layer, gf, rope=None, tm=512,
              sub=256, ff_chunk=1024):
    t, d = x.shape
    row_spec = pl.BlockSpec((tm, d), lambda i: (i, 0))
    args = [y] + ([gate] if gate is not None else []) + [x, wo, bo, gm, win_all, wout_all]
    resident = lambda shape, blk: pl.BlockSpec(shape, lambda i: (blk, 0),
                                               pipeline_mode=pl.Buffered(1))
    specs = [row_spec] * (len(args) - 5) + [
        resident(wo.shape, 0), resident(bo.shape, 0), resident(gm.shape, 0),
        resident((d, D_FF), layer), resident((D_FF, d), layer)]
    if gf is not None:
        args.append(gf)
        specs.append(_const_spec(gf.shape))
    out_shape = [jax.ShapeDtypeStruct((t, d), F32)]
    out_specs = [row_spec]
    if rope is not None:
        pos_packed, freq_row = rope
        packed_spec = pl.BlockSpec((tm // ROPE_PACK, LANES), lambda i: (i, 0))
        args += [pos_packed, freq_row]
        specs += [packed_spec, _const_spec(freq_row.shape)]
        out_shape += [jax.ShapeDtypeStruct(pos_packed.shape, F32)] * 2
        out_specs += [packed_spec] * 2
    outs = pl.pallas_call(
        functools.partial(_proj_mlp_kernel, gate is not None, gf is not None,
                          rope is not None, ff_chunk, sub),
        out_shape=out_shape,
        grid=(t // tm,),
        in_specs=specs,
        out_specs=out_specs,
        compiler_params=pltpu.CompilerParams(
            dimension_semantics=("parallel",), vmem_limit_bytes=VMEM_LIMIT),
        name="proj_mlp",
    )(*args)
    return outs[0] if rope is None else outs


def _pack_positions(pos):
    per = ROPE_SUB // ROPE_PACK
    grouped = pos.astype(F32).reshape(-1, ROPE_PACK, per).transpose(0, 2, 1)
    return jnp.repeat(grouped, LANES // ROPE_PACK, axis=-1).reshape(-1, LANES)


def _attn_qkv_kernel(sub, x_ref, g_ref, w_ref, b_ref, cos_ref, sin_ref, o_ref):
    n_sub = x_ref.shape[0] // sub
    rows_of = lambda j: slice(j * sub, (j + 1) * sub)
    lane = lax.broadcasted_iota(jnp.int32, (sub, LANES), 1)
    first_half = lane % HEAD < HEAD // 2
    n_q = N_HEADS * HEAD // LANES
    n_rot = (N_HEADS + N_KV_HEADS) * HEAD // LANES
    n_blk = 2 * LANES

    def norm(j, out):
        out["h"] = _rms(x_ref[rows_of(j), :], g_ref[...]).astype(BF16)
        yield

    def project(j, ins, out):
        out["qkv"] = []
        for c in range(QKV_DIM // n_blk):
            cols = slice(c * n_blk, (c + 1) * n_blk)
            out["qkv"].append(jnp.dot(ins["h"], w_ref[:, cols], preferred_element_type=F32)
                              + b_ref[:, cols])
            yield

    def unpack(packed):
        quarter = lax.shift_right_logical(
            lax.broadcasted_iota(jnp.int32, packed.shape, 1), (LANES // ROPE_PACK).bit_length() - 1)
        rolled = [packed] + [pltpu.roll(packed, n * (LANES // ROPE_PACK), axis=1)
                             for n in range(1, ROPE_PACK)]
        blocks = []
        for n in range(ROPE_PACK):
            blk = rolled[(ROPE_PACK - 1 - n) % ROPE_PACK]
            for qq in range(ROPE_PACK - 2, -1, -1):
                blk = jnp.where(quarter == qq, rolled[(qq - n) % ROPE_PACK], blk)
            blocks.append(blk)
        return jnp.concatenate(blocks, axis=0)

    def rotary(j, qkv):
        rows = rows_of(j)
        packed_rows = slice(j * sub // ROPE_PACK, (j + 1) * sub // ROPE_PACK)
        cos = unpack(cos_ref[packed_rows, :])
        sin = jnp.where(first_half, -1.0, 1.0) * unpack(sin_ref[packed_rows, :])
        yield
        for s in range(QKV_DIM // LANES):
            c, half = divmod(s, n_blk // LANES)
            blk = qkv[c][:, half * LANES:(half + 1) * LANES]
            if s < n_rot:
                rot = jnp.where(first_half, pltpu.roll(blk, LANES - HEAD // 2, axis=1),
                                pltpu.roll(blk, HEAD // 2, axis=1))
                blk = blk * cos + rot * sin
            if s < n_q:
                blk = blk * Q_SCALE
            o_ref[rows, s * LANES:(s + 1) * LANES] = blk.astype(o_ref.dtype)
            if s % 2 == 1:
                yield

    ins = [dict() for _ in range(n_sub)]
    outs = [dict() for _ in range(n_sub)]
    _drain(norm(0, ins[0]))
    for j in range(n_sub):
        fillers = []
        if j + 1 < n_sub:
            fillers.append(norm(j + 1, ins[j + 1]))
        if j > 0:
            fillers.append(rotary(j - 1, outs[j - 1]["qkv"]))
        _interleave(project(j, ins[j], outs[j]), fillers)
    _drain(rotary(n_sub - 1, outs[n_sub - 1]["qkv"]))


def _attn_qkv(x, g, w, b, cos, sin, tm=512, sub=ROPE_SUB):
    t, d = x.shape
    row = lambda width: pl.BlockSpec((tm, width), lambda i: (i, 0))
    packed = pl.BlockSpec((tm // ROPE_PACK, LANES), lambda i: (i, 0))
    return pl.pallas_call(
        functools.partial(_attn_qkv_kernel, sub),
        out_shape=jax.ShapeDtypeStruct((t, QKV_DIM), BF16),
        grid=(t // tm,),
        in_specs=[row(d), _const_spec(g.shape), _const_spec(w.shape), _const_spec(b.shape),
                  packed, packed],
        out_specs=row(QKV_DIM),
        compiler_params=pltpu.CompilerParams(
            dimension_semantics=("parallel",), vmem_limit_bytes=VMEM_LIMIT),
        name="attn_qkv",
    )(x, g, w, b, cos, sin)


def _attn_core_kernel(q_blocks, q_ref, kp_ref, kc_ref, vp_ref, vc_ref, sink_ref, o_ref):
    n = pl.program_id(1)
    kr = lax.broadcasted_iota(jnp.int32, (2 * WINDOW, WINDOW), 0)
    qc = lax.broadcasted_iota(jnp.int32, (2 * WINDOW, WINDOW), 1)
    own = kr >= WINDOW
    band = (own & (kr - WINDOW <= qc)) | (jnp.logical_not(own) & (kr > qc))
    first_band = band & (own | (n > 0))
    neg = -jnp.inf
    probs = [(i, j) for i in range(q_blocks) for j in range(N_KV_HEADS)]
    blk = lambda i: slice(i * WINDOW, (i + 1) * WINDOW)
    kvl = lambda j: slice(j * HEAD, (j + 1) * HEAD)
    head = lambda j, g: j * GROUP + g

    def keys(prev_ref, cur_ref, i, j):
        prev = prev_ref[0, :, kvl(j)] if i == 0 else cur_ref[0, blk(i - 1), kvl(j)]
        return jnp.concatenate([prev, cur_ref[0, blk(i), kvl(j)]], axis=0)

    q = [jnp.concatenate([q_ref[0, blk(i), head(j, g) * HEAD:(head(j, g) + 1) * HEAD]
                          for g in range(GROUP)], axis=0) for i, j in probs]
    s_t = [_dot_nt(keys(kp_ref, kc_ref, i, j), q[p]) for p, (i, j) in enumerate(probs)]
    p_t = [[] for _ in probs]
    for g in range(GROUP):
        sink = [sink_ref[:, head(j, g):head(j, g) + 1] * LOG2E for _, j in probs]
        x = [jnp.where(first_band if i == 0 else band, s_t[p][:, blk(g)], neg)
             for p, (i, j) in enumerate(probs)]
        m = [jnp.maximum(jnp.max(x[p], axis=0, keepdims=True), sink[p])
             for p in range(len(probs))]
        e = [jnp.exp2(x[p] - m[p]) for p in range(len(probs))]
        den = [jnp.sum(e[p], axis=0, keepdims=True) + jnp.exp2(sink[p] - m[p])
               for p in range(len(probs))]
        for p in range(len(probs)):
            p_t[p].append((e[p] * (1.0 / den[p])).astype(BF16))
    for p, (i, j) in enumerate(probs):
        o = lax.dot_general(jnp.concatenate(p_t[p], axis=1), keys(vp_ref, vc_ref, i, j),
                            (((0,), (0,)), ((), ())), preferred_element_type=F32)
        for g in range(GROUP):
            o_ref[0, blk(i), head(j, g) * HEAD:(head(j, g) + 1) * HEAD] = (
                o[blk(g)].astype(o_ref.dtype))


def _attn_core(qkv, sinks, q_blocks=8):
    b, s, _ = qkv.shape
    nq = N_HEADS * HEAD
    tq = q_blocks * WINDOW
    k_blk = nq // LANES
    v_blk = k_blk + N_KV_HEADS * HEAD // LANES
    q_spec = pl.BlockSpec((1, tq, nq), lambda bi, n: (bi, n, 0))
    cur = lambda blk: pl.BlockSpec((1, tq, LANES), lambda bi, n: (bi, n, blk))
    prev = lambda blk: pl.BlockSpec((1, WINDOW, LANES),
                                    lambda bi, n: (bi, jnp.maximum(n * q_blocks - 1, 0), blk))
    return pl.pallas_call(
        functools.partial(_attn_core_kernel, q_blocks),
        out_shape=jax.ShapeDtypeStruct((b, s, nq), BF16),
        grid=(b, s // tq),
        in_specs=[q_spec, prev(k_blk), cur(k_blk), prev(v_blk), cur(v_blk),
                  _const_spec(sinks.shape)],
        out_specs=q_spec,
        compiler_params=pltpu.CompilerParams(
            dimension_semantics=("parallel", "arbitrary"), vmem_limit_bytes=VMEM_LIMIT),
        name="attn_core",
    )(qkv, qkv, qkv, qkv, qkv, sinks)


def kernel(x, positions, norm_mix_g, norm_mlp_g, norm_final_g, rwkv_mu, rwkv_w_r, rwkv_w_k, rwkv_w_v, rwkv_w_o, rwkv_w0, rwkv_w1, rwkv_w2, rwkv_a0, rwkv_a1, rwkv_a2, rwkv_g1, rwkv_g2, rwkv_k_k, rwkv_k_a, rwkv_r_k, rwkv_ln_w, rwkv_ln_b, attn_w_qkv, attn_b_qkv, attn_sinks, attn_w_o, attn_b_o, mlp_w_in, mlp_w_out):
    b, s, d = x.shape
    t = b * s
    x2d = x.reshape(t, d)
    row = lambda p: p.reshape(1, -1)
    bf = lambda w: w.astype(BF16)
    zero_bias = jnp.zeros((1, d), F32)

    r, k, v, lw, a, gate = _rwkv_pre(
        x2d, s, row(norm_mix_g[0]), rwkv_mu[0], bf(rwkv_w_r[0]), bf(rwkv_w_k[0]),
        bf(rwkv_w_v[0]), row(rwkv_w0[0]), bf(rwkv_w1[0]), bf(rwkv_w2[0]), row(rwkv_a0[0]),
        bf(rwkv_a1[0]), bf(rwkv_a2[0]), bf(rwkv_g1[0]), bf(rwkv_g2[0]))
    seq = lambda z: z.reshape(b, s, d)
    later_weights = [rwkv_w_o[0], attn_w_qkv[0], attn_w_o[0],
                     mlp_w_in.reshape(-1, D_FF), mlp_w_out.reshape(-1, d)]
    y, (w_o0, w_qkv, w_o1, w_in_all, w_out_all) = _wkv(
        seq(r), seq(k), seq(v), seq(lw), seq(a), row(rwkv_k_k[0]), row(rwkv_k_a[0]),
        row(rwkv_r_k[0]), row(rwkv_ln_w[0]), row(rwkv_ln_b[0]), later_weights)
    inv_freq = ROPE_THETA ** (-jnp.arange(0, HEAD, 2, dtype=F32) / HEAD)
    freq_row = jnp.tile(inv_freq, LANES // (HEAD // 2)).reshape(1, LANES)
    x2d, cos, sin = _proj_mlp(y.reshape(t, d), gate, x2d, w_o0, zero_bias,
                              row(norm_mlp_g[0]), w_in_all, w_out_all, 0, None,
                              rope=(_pack_positions(positions.reshape(t)), freq_row))

    qkv = _attn_qkv(x2d, row(norm_mix_g[1]), w_qkv, row(attn_b_qkv[0]), cos, sin)
    o = _attn_core(qkv.reshape(b, s, QKV_DIM), row(attn_sinks[0]))
    out = _proj_mlp(o.reshape(t, d), None, x2d, w_o1, row(attn_b_o[0]),
                    row(norm_mlp_g[1]), w_in_all, w_out_all, 1, row(norm_final_g))
    return out.reshape(b, s, d)
```

```python
import functools
import math

import jax
import jax.numpy as jnp
from jax import lax
from jax.experimental import pallas as pl
from jax.experimental.pallas import tpu as pltpu

F32 = jnp.float32
BF16 = jnp.bfloat16

D_MODEL = 1024
HEAD = 64
N_HEADS = D_MODEL // HEAD
N_KV_HEADS = 2
GROUP = N_HEADS // N_KV_HEADS
WINDOW = 128
QKV_DIM = (N_HEADS + 2 * N_KV_HEADS) * HEAD
D_FF = 4 * D_MODEL
ROPE_THETA = 10000.0
RMS_EPS = 1e-5
GN_EPS = 64e-5

LANES = 128
ROPE_PACK = LANES // (HEAD // 2)
ROPE_SUB = 256
LOG2E = math.log2(math.e)
Q_SCALE = HEAD ** -0.5 * LOG2E
CHUNK = 64
VMEM_LIMIT = 56 * 1024 * 1024


def _rms(x, g):
    return x * lax.rsqrt(jnp.mean(x * x, axis=-1, keepdims=True) + RMS_EPS) * g


def _dot(a, b):
    return jnp.dot(a.astype(BF16), b.astype(BF16), preferred_element_type=F32)


def _dot_nt(a, b):
    return lax.dot_general(a.astype(BF16), b.astype(BF16), (((1,), (1,)), ((), ())),
                           preferred_element_type=F32)


def _dot_tn(a, b):
    return lax.dot_general(a.astype(BF16), b.astype(BF16), (((0,), (0,)), ((), ())),
                           preferred_element_type=F32)


def _const_spec(shape):
    nd = len(shape)
    return pl.BlockSpec(shape, lambda *_: (0,) * nd)


def _drain(gen):
    for _ in gen:
        pass


def _interleave(main, fillers):
    for _ in main:
        for f in fillers:
            next(f, None)
    for f in fillers:
        _drain(f)


def _rwkv_pre_kernel(seq_tiles, sub, x_ref, xp_ref, g_ref, mu_ref, wr_ref, wk_ref, wv_ref,
                     w0_ref, w1_ref, w2_ref, a0_ref, a1_ref, a2_ref, g1_ref, g2_ref,
                     r_out, k_out, v_out, lw_out, a_out, g_out):
    i = pl.program_id(0)
    n_sub = x_ref.shape[0] // sub
    g = g_ref[...]
    mu = mu_ref[...]
    rows_of = lambda j: slice(j * sub, (j + 1) * sub)
    hp = _rms(xp_ref[...], g)
    carry_row = {0: jnp.where(i % seq_tiles == 0, 0.0, hp[7:8, :])}

    def mix(j, out):
        h = _rms(x_ref[rows_of(j), :], g)
        carry_row[j + 1] = h[sub - 1:sub, :]
        yield
        row = lax.broadcasted_iota(jnp.int32, h.shape, 0)
        dx = jnp.where(row == 0, carry_row[j], pltpu.roll(h, 1, axis=0)) - h
        yield
        for n, name in enumerate(("r", "w", "k", "v", "a", "g")):
            out[name] = (h + dx * mu[n:n + 1]).astype(BF16)
            if n % 2 == 1:
                yield

    def project(j, xs):
        rows = rows_of(j)
        dw = _dot(xs["w"], w1_ref[...])
        da = _dot(xs["a"], a1_ref[...])
        dg = _dot(xs["g"], g1_ref[...])
        yield
        r_out[rows, :] = jnp.dot(xs["r"], wr_ref[...], preferred_element_type=F32)
        yield
        w_pre = w0_ref[...] + _dot(jnp.tanh(dw), w2_ref[...])
        a_pre = a0_ref[...] + _dot(da, a2_ref[...])
        g_out[rows, :] = _dot(jax.nn.sigmoid(dg), g2_ref[...])
        yield
        k_out[rows, :] = jnp.dot(xs["k"], wk_ref[...], preferred_element_type=F32)
        yield
        lw_out[rows, :] = jax.nn.sigmoid(w_pre) * (-math.exp(-0.5))
        a_out[rows, :] = jax.nn.sigmoid(a_pre)
        yield
        v_out[rows, :] = jnp.dot(xs["v"], wv_ref[...], preferred_element_type=F32)
        yield

    xs = [dict() for _ in range(n_sub)]
    _drain(mix(0, xs[0]))
    for j in range(n_sub):
        fillers = [mix(j + 1, xs[j + 1])] if j + 1 < n_sub else []
        _interleave(project(j, xs[j]), fillers)


def _rwkv_pre(x2d, seq_len, g, mu, wr, wk, wv, w0, w1, w2, a0, a1, a2, g1, g2, tm=512, sub=256):
    t, d = x2d.shape
    seq_tiles = seq_len // tm
    row_spec = pl.BlockSpec((tm, d), lambda i: (i, 0))
    prev_spec = pl.BlockSpec((8, d), lambda i: (jnp.maximum(i * (tm // 8) - 1, 0), 0))
    consts = (g, mu, wr, wk, wv, w0, w1, w2, a0, a1, a2, g1, g2)
    out = jax.ShapeDtypeStruct((t, d), F32)
    return pl.pallas_call(
        functools.partial(_rwkv_pre_kernel, seq_tiles, sub),
        out_shape=(out,) * 6,
        grid=(t // tm,),
        in_specs=[row_spec, prev_spec] + [_const_spec(c.shape) for c in consts],
        out_specs=(row_spec,) * 6,
        compiler_params=pltpu.CompilerParams(
            dimension_semantics=("parallel",), vmem_limit_bytes=VMEM_LIMIT),
        name="rwkv_pre",
    )(x2d, x2d, *consts)


def _cumsum_rows(x):
    n = x.shape[0]
    row = lax.broadcasted_iota(jnp.int32, x.shape, 0)
    s = 1
    while s < n:
        x = x + jnp.where(row >= s, pltpu.roll(x, s, axis=0), 0.0)
        s *= 2
    return x


INV_BASE = 8
GROUP_HEADS = 4
GROUP_W = GROUP_HEADS * HEAD


def _head_sum(x):
    low = lax.broadcasted_iota(jnp.int32, (x.shape[0], LANES), 1) < HEAD
    outs = []
    for c in range(x.shape[1] // LANES):
        xc = x[:, c * LANES:(c + 1) * LANES]
        s_lo = jnp.sum(jnp.where(low, xc, 0.0), axis=-1, keepdims=True)
        s_hi = jnp.sum(jnp.where(low, 0.0, xc), axis=-1, keepdims=True)
        outs.append(jnp.where(low, s_lo, s_hi))
    return jnp.concatenate(outs, axis=1)


def _wkv_kernel(chunks, n_w, r_ref, k_ref, v_ref, lw_ref, a_ref,
                kk_ref, ka_ref, rk_ref, lnw_ref, lnb_ref, *rest):
    o_ref, state_ref = rest[n_w], rest[-1]
    w_refs = rest[:n_w] + rest[n_w + 1:-1]
    n_batch = r_ref.shape[0]
    n_groups = D_MODEL // GROUP_W
    probs = [(b, g) for b in range(n_batch) for g in range(n_groups)]
    ps = range(len(probs))

    @pl.when(pl.program_id(0) == 0)
    def _():
        state_ref[...] = jnp.zeros_like(state_ref)

    ri = lax.broadcasted_iota(jnp.int32, (CHUNK, GROUP_W), 0)
    lane = lax.broadcasted_iota(jnp.int32, (CHUNK, GROUP_W), 1)
    cj = lane & (HEAD - 1)
    lane_head = lax.shift_right_logical(lane, HEAD.bit_length() - 1)
    head_masks = [lane_head == h for h in range(GROUP_HEADS)]
    blk = lambda idx, size: lax.shift_right_logical(idx, size.bit_length() - 1)
    strict = cj < ri
    incl = cj <= ri
    eye = (cj == ri).astype(F32)
    diag = blk(ri, INV_BASE) == blk(cj, INV_BASE)
    bands = []
    size = INV_BASE
    while size < CHUNK:
        bands.append((blk(ri, 2 * size) == blk(cj, 2 * size))
                     & (blk(ri, size) != blk(cj, size)))
        size *= 2
    sr = lax.broadcasted_iota(jnp.int32, (GROUP_W, GROUP_W), 0)
    sc = lax.broadcasted_iota(jnp.int32, (GROUP_W, GROUP_W), 1)
    state_mask = blk(sr, HEAD) == blk(sc, HEAD)

    def bd(y):
        return jnp.concatenate([jnp.where(m, y, 0.0) for m in head_masks], axis=0).astype(BF16)

    def mm(x, w):
        return jnp.dot(x.astype(BF16), w, preferred_element_type=F32)

    def mm_nt(x, w):
        return lax.dot_general(x.astype(BF16), w, (((1,), (1,)), ((), ())),
                               preferred_element_type=F32)

    cols = [slice(g * GROUP_W, (g + 1) * GROUP_W) for g in range(n_groups)]
    rows_of = lambda c: slice(c * CHUNK, (c + 1) * CHUNK)


    def prepare(c, b, out):
        rows = rows_of(c)
        lw = lw_ref[b, rows, :]
        linc = _cumsum_rows(lw)
        yield
        p_inc = jnp.exp(linc)
        p_exc = jnp.exp(linc - lw)
        p_inv = jnp.exp(-linc)
        yield
        k = k_ref[b, rows, :]
        a = a_ref[b, rows, :]
        kk = k * kk_ref[...]
        kk = kk * lax.rsqrt(jnp.maximum(_head_sum(kk * kk), 1e-24))
        yield
        kmod = k * (1.0 + (a - 1.0) * ka_ref[...])
        at_f = (-kk * p_exc).astype(BF16)
        rt_f = (r_ref[b, rows, :] * p_inc).astype(BF16)
        yield
        out["lhs"] = [jnp.concatenate([at_f[:, cl], rt_f[:, cl]], axis=0) for cl in cols]
        out["bt"] = kk * a * p_inv
        out["kt"] = kmod * p_inv
        out["kmod"] = kmod
        out["p_end"] = p_inc[CHUNK - 1:CHUNK, :]
        yield

    def matmuls(c, ops, out):
        v = [v_ref[b, rows_of(c), :] for b in range(n_batch)]
        lhs = [ops[b]["lhs"][g] for b, g in probs]
        bt = [ops[b]["bt"][:, cols[g]] for b, g in probs]
        kt = [ops[b]["kt"][:, cols[g]] for b, g in probs]
        vg = [v[b][:, cols[g]] for b, g in probs]
        ab = [mm_nt(lhs[p], bd(bt[p])) for p in ps]
        ak = [mm_nt(lhs[p], bd(kt[p])) for p in ps]
        yield
        a_ab = [jnp.where(strict, x[:CHUNK], 0.0) for x in ab]
        a_rb = [jnp.where(incl, x[CHUNK:], 0.0) for x in ab]
        a_k = [jnp.concatenate([jnp.where(strict, x[:CHUNK], 0.0),
                                jnp.where(incl, x[CHUNK:], 0.0)], axis=0) for x in ak]
        d1 = [jnp.where(diag, x, 0.0) for x in a_ab]
        d2 = [mm(d1[p], bd(d1[p])) for p in ps]
        yield
        t = [eye + x for x in d1]
        td = [mm(jnp.concatenate([t[p], d2[p]], axis=0), bd(d2[p])) for p in ps]
        yield
        t = [t[p] + td[p][:CHUNK] for p in ps]
        t = [t[p] + mm(t[p], bd(td[p][CHUNK:])) for p in ps]
        yield
        for band in bands:
            te = [mm(t[p], bd(jnp.where(band, a_ab[p], 0.0))) for p in ps]
            yield
            t = [t[p] + mm(te[p], bd(t[p])) for p in ps]
            yield
        s0 = [state_ref[p] for p in ps]
        zy = [mm_nt(lhs[p], s0[p].astype(BF16)) for p in ps]
        av = [mm(a_k[p], bd(vg[p])) for p in ps]
        yield
        u = [mm(t[p], bd(zy[p][:CHUNK] + av[p][:CHUNK])) for p in ps]
        yield
        y = [zy[p][CHUNK:] + av[p][CHUNK:] + mm(a_rb[p], bd(u[p])) for p in ps]
        for p, (b, g) in enumerate(probs):
            uv = jnp.concatenate([u[p], vg[p]], axis=0)
            bk = jnp.concatenate([bt[p], kt[p]], axis=0)
            state_ref[p] = (jnp.where(state_mask, s0[p] + _dot_tn(uv, bk), 0.0)
                            * ops[b]["p_end"][:, cols[g]])
        out["y"] = [jnp.concatenate(y[b * n_groups:(b + 1) * n_groups], axis=1)
                    for b in range(n_batch)]
        yield

    def finish(c, b, ops, y_f):
        rows = rows_of(c)
        inv_n = 1.0 / HEAD
        yc = y_f - _head_sum(y_f) * inv_n
        yield
        var = _head_sum(yc * yc) * inv_n
        yn = yc * lax.rsqrt(var + GN_EPS) * lnw_ref[...] + lnb_ref[...]
        yield
        bonus = _head_sum(r_ref[b, rows, :] * ops["kmod"] * rk_ref[...])
        o_ref[b, rows, :] = yn + bonus * v_ref[b, rows, :]
        yield

    def cast_weights():
        for src, dst in zip(w_refs[:len(w_refs) // 2], w_refs[len(w_refs) // 2:]):
            dst[...] = src[...].astype(dst.dtype)
            yield

    bs = range(n_batch)
    ops = [[dict() for _ in bs] for _ in range(chunks)]
    res = [dict() for _ in range(chunks)]
    for b in bs:
        _drain(prepare(0, b, ops[0][b]))
    for c in range(chunks):
        fillers = [cast_weights()] if c == 0 else []
        if c + 1 < chunks:
            fillers += [prepare(c + 1, b, ops[c + 1][b]) for b in bs]
        if c > 0:
            fillers += [finish(c - 1, b, ops[c - 1][b], res[c - 1]["y"][b]) for b in bs]
        _interleave(matmuls(c, ops[c], res[c]), fillers)
    for b in bs:
        _drain(finish(chunks - 1, b, ops[chunks - 1][b], res[chunks - 1]["y"][b]))


def _wkv(r, k, v, lw, a, kk_p, ka_p, rk_p, lnw_p, lnb_p, weights, rows=256):
    b, s, d = r.shape
    steps = s // rows
    seq_spec = pl.BlockSpec((b, rows, d), lambda ci: (0, ci, 0))
    par_spec = pl.BlockSpec((1, d), lambda ci: (0, 0))
    w_specs = [pl.BlockSpec((w.shape[0] // steps, w.shape[1]), lambda ci: (ci, 0))
               for w in weights]
    w_out = [jax.ShapeDtypeStruct(w.shape, BF16) for w in weights]
    outs = pl.pallas_call(
        functools.partial(_wkv_kernel, rows // CHUNK, len(weights)),
        out_shape=[jax.ShapeDtypeStruct((b, s, d), F32)] + w_out,
        grid=(steps,),
        in_specs=[seq_spec] * 5 + [par_spec] * 5 + w_specs,
        out_specs=[seq_spec] + w_specs,
        scratch_shapes=[pltpu.VMEM((b * d // GROUP_W, GROUP_W, GROUP_W), F32)],
        compiler_params=pltpu.CompilerParams(
            dimension_semantics=("arbitrary",), vmem_limit_bytes=VMEM_LIMIT),
        name="wkv",
    )(r, k, v, lw, a, kk_p, ka_p, rk_p, lnw_p, lnb_p, *weights)
    return outs[0], outs[1:]


def _proj_mlp_kernel(has_gate, final_norm, with_rope, ff_chunk, sub, *refs):
    refs = list(refs)
    y_ref = refs.pop(0)
    gate_ref = refs.pop(0) if has_gate else None
    x_ref, wo_ref, bo_ref, gm_ref, win_ref, wout_ref = refs[:6]
    refs = refs[6:]
    gf_ref = refs.pop(0) if final_norm else None
    if with_rope:
        pos_ref, freq_ref = refs.pop(0), refs.pop(0)
    o_ref = refs.pop(0)
    if with_rope:
        cos_ref, sin_ref = refs.pop(0), refs.pop(0)

    def rope_tables(part, parts, pieces=2):
        rows = pos_ref.shape[0] // (parts * pieces)
        for n in range(part * pieces, (part + 1) * pieces):
            sl = slice(n * rows, (n + 1) * rows)
            ang = pos_ref[sl, :] * freq_ref[...]
            cos_ref[sl, :] = jnp.cos(ang)
            sin_ref[sl, :] = jnp.sin(ang)
            yield

    n_sub = x_ref.shape[0] // sub
    rows_of = lambda j: slice(j * sub, (j + 1) * sub)

    def head(j, out):
        rows = rows_of(j)
        y = y_ref[rows, :].astype(F32)
        if has_gate:
            y = y * gate_ref[rows, :]
        yield
        x1 = x_ref[rows, :] + _dot(y, wo_ref[...]) + bo_ref[...]
        yield
        out["x1"] = x1
        out["hb"] = _rms(x1, gm_ref[...]).astype(BF16)
        yield

    def mlp(j, ins, out):
        acc = ins["x1"]
        for c in range(D_FF // ff_chunk):
            cols = slice(c * ff_chunk, (c + 1) * ff_chunk)
            hid = jnp.maximum(jnp.dot(ins["hb"], win_ref[:, cols], preferred_element_type=F32), 0.0)
            yield
            acc = acc + _dot(hid * hid, wout_ref[cols, :])
            yield
        out["acc"] = acc

    def tail(j, acc):
        if final_norm:
            acc = _rms(acc, gf_ref[...])
        o_ref[rows_of(j), :] = acc
        yield

    ins = [dict() for _ in range(n_sub)]
    outs = [dict() for _ in range(n_sub)]
    _drain(head(0, ins[0]))
    for j in range(n_sub):
        fillers = []
        if j + 1 < n_sub:
            fillers.append(head(j + 1, ins[j + 1]))
        if j > 0:
            fillers.append(tail(j - 1, outs[j - 1]["acc"]))
        if with_rope:
            fillers.append(rope_tables(j, n_sub))
        _interleave(mlp(j, ins[j], outs[j]), fillers)
    _drain(tail(n_sub - 1, outs[n_sub - 1]["acc"]))


def _proj_mlp(y, gate, x, wo, bo, gm, win_all, wout_all, layer, gf, rope=None, tm=1024,
              sub=256, ff_chunk=1024):
    t, d = x.shape
    row_spec = pl.BlockSpec((tm, d), lambda i: (i, 0))
    args = [y] + ([gate] if gate is not None else []) + [x, wo, bo, gm, win_all, wout_all]
    resident = lambda shape, blk: pl.BlockSpec(shape, lambda i: (blk, 0),
                                               pipeline_mode=pl.Buffered(1))
    specs = [row_spec] * (len(args) - 5) + [
        resident(wo.shape, 0), resident(bo.shape, 0), resident(gm.shape, 0),
        resident((d, D_FF), layer), resident((D_FF, d), layer)]
    if gf is not None:
        args.append(gf)
        specs.append(_const_spec(gf.shape))
    out_shape = [jax.ShapeDtypeStruct((t, d), F32)]
    out_specs = [row_spec]
    if rope is not None:
        pos_packed, freq_row = rope
        packed_spec = pl.BlockSpec((tm // ROPE_PACK, LANES), lambda i: (i, 0))
        args += [pos_packed, freq_row]
        specs += [packed_spec, _const_spec(freq_row.shape)]
        out_shape += [jax.ShapeDtypeStruct(pos_packed.shape, F32)] * 2
        out_specs += [packed_spec] * 2
    outs = pl.pallas_call(
        functools.partial(_proj_mlp_kernel, gate is not None, gf is not None,
                          rope is not None, ff_chunk, sub),
        out_shape=out_shape,
        grid=(t // tm,),
        in_specs=specs,
        out_specs=out_specs,
        compiler_params=pltpu.CompilerParams(
            dimension_semantics=("parallel",), vmem_limit_bytes=VMEM_LIMIT),
        name="proj_mlp",
    )(*args)
    return outs[0] if rope is None else outs


def _pack_positions(pos):
    per = ROPE_SUB // ROPE_PACK
    grouped = pos.astype(F32).reshape(-1, ROPE_PACK, per).transpose(0, 2, 1)
    return jnp.repeat(grouped, LANES // ROPE_PACK, axis=-1).reshape(-1, LANES)


def _attn_qkv_kernel(sub, x_ref, g_ref, w_ref, b_ref, cos_ref, sin_ref, o_ref):
    n_sub = x_ref.shape[0] // sub
    rows_of = lambda j: slice(j * sub, (j + 1) * sub)
    lane = lax.broadcasted_iota(jnp.int32, (sub, LANES), 1)
    first_half = lane % HEAD < HEAD // 2
    n_q = N_HEADS * HEAD // LANES
    n_rot = (N_HEADS + N_KV_HEADS) * HEAD // LANES
    n_blk = 2 * LANES

    def norm(j, out):
        out["h"] = _rms(x_ref[rows_of(j), :], g_ref[...]).astype(BF16)
        yield

    def project(j, ins, out):
        out["qkv"] = []
        for c in range(QKV_DIM // n_blk):
            cols = slice(c * n_blk, (c + 1) * n_blk)
            out["qkv"].append(jnp.dot(ins["h"], w_ref[:, cols], preferred_element_type=F32)
                              + b_ref[:, cols])
            yield

    def unpack(packed):
        quarter = lax.shift_right_logical(
            lax.broadcasted_iota(jnp.int32, packed.shape, 1), (LANES // ROPE_PACK).bit_length() - 1)
        rolled = [packed] + [pltpu.roll(packed, n * (LANES // ROPE_PACK), axis=1)
                             for n in range(1, ROPE_PACK)]
        blocks = []
        for n in range(ROPE_PACK):
            blk = rolled[(ROPE_PACK - 1 - n) % ROPE_PACK]
            for qq in range(ROPE_PACK - 2, -1, -1):
                blk = jnp.where(quarter == qq, rolled[(qq - n) % ROPE_PACK], blk)
            blocks.append(blk)
        return jnp.concatenate(blocks, axis=0)

    def rotary(j, qkv):
        rows = rows_of(j)
        packed_rows = slice(j * sub // ROPE_PACK, (j + 1) * sub // ROPE_PACK)
        cos = unpack(cos_ref[packed_rows, :])
        sin = jnp.where(first_half, -1.0, 1.0) * unpack(sin_ref[packed_rows, :])
        yield
        for s in range(QKV_DIM // LANES):
            c, half = divmod(s, n_blk // LANES)
            blk = qkv[c][:, half * LANES:(half + 1) * LANES]
            if s < n_rot:
                rot = jnp.where(first_half, pltpu.roll(blk, LANES - HEAD // 2, axis=1),
                                pltpu.roll(blk, HEAD // 2, axis=1))
                blk = blk * cos + rot * sin
            if s < n_q:
                blk = blk * Q_SCALE
            o_ref[rows, s * LANES:(s + 1) * LANES] = blk.astype(o_ref.dtype)
            if s % 2 == 1:
                yield

    ins = [dict() for _ in range(n_sub)]
    outs = [dict() for _ in range(n_sub)]
    _drain(norm(0, ins[0]))
    for j in range(n_sub):
        fillers = []
        if j + 1 < n_sub:
            fillers.append(norm(j + 1, ins[j + 1]))
        if j > 0:
            fillers.append(rotary(j - 1, outs[j - 1]["qkv"]))
        _interleave(project(j, ins[j], outs[j]), fillers)
    _drain(rotary(n_sub - 1, outs[n_sub - 1]["qkv"]))


def _attn_qkv(x, g, w, b, cos, sin, tm=1024, sub=ROPE_SUB):
    t, d = x.shape
    row = lambda width: pl.BlockSpec((tm, width), lambda i: (i, 0))
    packed = pl.BlockSpec((tm // ROPE_PACK, LANES), lambda i: (i, 0))
    return pl.pallas_call(
        functools.partial(_attn_qkv_kernel, sub),
        out_shape=jax.ShapeDtypeStruct((t, QKV_DIM), BF16),
        grid=(t // tm,),
        in_specs=[row(d), _const_spec(g.shape), _const_spec(w.shape), _const_spec(b.shape),
                  packed, packed],
        out_specs=row(QKV_DIM),
        compiler_params=pltpu.CompilerParams(
            dimension_semantics=("parallel",), vmem_limit_bytes=VMEM_LIMIT),
        name="attn_qkv",
    )(x, g, w, b, cos, sin)


def _attn_core_kernel(q_blocks, q_ref, kp_ref, kc_ref, vp_ref, vc_ref, sink_ref, o_ref):
    n = pl.program_id(1)
    kr = lax.broadcasted_iota(jnp.int32, (2 * WINDOW, WINDOW), 0)
    qc = lax.broadcasted_iota(jnp.int32, (2 * WINDOW, WINDOW), 1)
    own = kr >= WINDOW
    band = (own & (kr - WINDOW <= qc)) | (jnp.logical_not(own) & (kr > qc))
    first_band = band & (own | (n > 0))
    neg = -jnp.inf
    probs = [(i, j) for i in range(q_blocks) for j in range(N_KV_HEADS)]
    blk = lambda i: slice(i * WINDOW, (i + 1) * WINDOW)
    kvl = lambda j: slice(j * HEAD, (j + 1) * HEAD)
    head = lambda j, g: j * GROUP + g

    def keys(prev_ref, cur_ref, i, j):
        prev = prev_ref[0, :, kvl(j)] if i == 0 else cur_ref[0, blk(i - 1), kvl(j)]
        return jnp.concatenate([prev, cur_ref[0, blk(i), kvl(j)]], axis=0)

    q = [jnp.concatenate([q_ref[0, blk(i), head(j, g) * HEAD:(head(j, g) + 1) * HEAD]
                          for g in range(GROUP)], axis=0) for i, j in probs]
    s_t = [_dot_nt(keys(kp_ref, kc_ref, i, j), q[p]) for p, (i, j) in enumerate(probs)]
    p_t = [[] for _ in probs]
    for g in range(GROUP):
        sink = [sink_ref[:, head(j, g):head(j, g) + 1] * LOG2E for _, j in probs]
        x = [jnp.where(first_band if i == 0 else band, s_t[p][:, blk(g)], neg)
             for p, (i, j) in enumerate(probs)]
        m = [jnp.maximum(jnp.max(x[p], axis=0, keepdims=True), sink[p])
             for p in range(len(probs))]
        e = [jnp.exp2(x[p] - m[p]) for p in range(len(probs))]
        den = [jnp.sum(e[p], axis=0, keepdims=True) + jnp.exp2(sink[p] - m[p])
               for p in range(len(probs))]
        for p in range(len(probs)):
            p_t[p].append((e[p] * (1.0 / den[p])).astype(BF16))
    for p, (i, j) in enumerate(probs):
        o = lax.dot_general(jnp.concatenate(p_t[p], axis=1), keys(vp_ref, vc_ref, i, j),
                            (((0,), (0,)), ((), ())), preferred_element_type=F32)
        for g in range(GROUP):
            o_ref[0, blk(i), head(j, g) * HEAD:(head(j, g) + 1) * HEAD] = (
                o[blk(g)].astype(o_ref.dtype))


def _attn_core(qkv, sinks, q_blocks=8):
    b, s, _ = qkv.shape
    nq = N_HEADS * HEAD
    tq = q_blocks * WINDOW
    k_blk = nq // LANES
    v_blk = k_blk + N_KV_HEADS * HEAD // LANES
    q_spec = pl.BlockSpec((1, tq, nq), lambda bi, n: (bi, n, 0))
    cur = lambda blk: pl.BlockSpec((1, tq, LANES), lambda bi, n: (bi, n, blk))
    prev = lambda blk: pl.BlockSpec((1, WINDOW, LANES),
                                    lambda bi, n: (bi, jnp.maximum(n * q_blocks - 1, 0), blk))
    return pl.pallas_call(
        functools.partial(_attn_core_kernel, q_blocks),
        out_shape=jax.ShapeDtypeStruct((b, s, nq), BF16),
        grid=(b, s // tq),
        in_specs=[q_spec, prev(k_blk), cur(k_blk), prev(v_blk), cur(v_blk),
                  _const_spec(sinks.shape)],
        out_specs=q_spec,
        compiler_params=pltpu.CompilerParams(
            dimension_semantics=("parallel", "arbitrary"), vmem_limit_bytes=VMEM_LIMIT),
        name="attn_core",
    )(qkv, qkv, qkv, qkv, qkv, sinks)


def kernel(x, positions, norm_mix_g, norm_mlp_g, norm_final_g, rwkv_mu, rwkv_w_r, rwkv_w_k, rwkv_w_v, rwkv_w_o, rwkv_w0, rwkv_w1, rwkv_w2, rwkv_a0, rwkv_a1, rwkv_a2, rwkv_g1, rwkv_g2, rwkv_k_k, rwkv_k_a, rwkv_r_k, rwkv_ln_w, rwkv_ln_b, attn_w_qkv, attn_b_qkv, attn_sinks, attn_w_o, attn_b_o, mlp_w_in, mlp_w_out):
    b, s, d = x.shape
    t = b * s
    x2d = x.reshape(t, d)
    row = lambda p: p.reshape(1, -1)
    bf = lambda w: w.astype(BF16)
    zero_bias = jnp.zeros((1, d), F32)

    r, k, v, lw, a, gate = _rwkv_pre(
        x2d, s, row(norm_mix_g[0]), rwkv_mu[0], bf(rwkv_w_r[0]), bf(rwkv_w_k[0]),
        bf(rwkv_w_v[0]), row(rwkv_w0[0]), bf(rwkv_w1[0]), bf(rwkv_w2[0]), row(rwkv_a0[0]),
        bf(rwkv_a1[0]), bf(rwkv_a2[0]), bf(rwkv_g1[0]), bf(rwkv_g2[0]))
    seq = lambda z: z.reshape(b, s, d)
    later_weights = [rwkv_w_o[0], attn_w_qkv[0], attn_w_o[0],
                     mlp_w_in.reshape(-1, D_FF), mlp_w_out.reshape(-1, d)]
    y, (w_o0, w_qkv, w_o1, w_in_all, w_out_all) = _wkv(
        seq(r), seq(k), seq(v), seq(lw), seq(a), row(rwkv_k_k[0]), row(rwkv_k_a[0]),
        row(rwkv_r_k[0]), row(rwkv_ln_w[0]), row(rwkv_ln_b[0]), later_weights)
    inv_freq = ROPE_THETA ** (-jnp.arange(0, HEAD, 2, dtype=F32) / HEAD)
    freq_row = jnp.tile(inv_freq, LANES // (HEAD // 2)).reshape(1, LANES)
    x2d, cos, sin = _proj_mlp(y.reshape(t, d), gate, x2d, w_o0, zero_bias,
                              row(norm_mlp_g[0]), w_in_all, w_out_all, 0, None,
                              rope=(_pack_positions(positions.reshape(t)), freq_row))

    qkv = _attn_qkv(x2d, row(norm_mix_g[1]), w_qkv, row(attn_b_qkv[0]), cos, sin)
    o = _attn_core(qkv.reshape(b, s, QKV_DIM), row(attn_sinks[0]))
    out = _proj_mlp(o.reshape(t, d), None, x2d, w_o1, row(attn_b_o[0]),
                    row(norm_mlp_g[1]), w_in_all, w_out_all, 1, row(norm_final_g))
    return out.reshape(b, s, d)
```

```python
import functools
import math

import jax
import jax.numpy as jnp
from jax import lax
from jax.experimental import pallas as pl
from jax.experimental.pallas import tpu as pltpu

F32 = jnp.float32
BF16 = jnp.bfloat16

D_MODEL = 1024
HEAD = 64
N_HEADS = D_MODEL // HEAD
N_KV_HEADS = 2
GROUP = N_HEADS // N_KV_HEADS
WINDOW = 128
QKV_DIM = (N_HEADS + 2 * N_KV_HEADS) * HEAD
D_FF = 4 * D_MODEL
ROPE_THETA = 10000.0
RMS_EPS = 1e-5
GN_EPS = 64e-5

LANES = 128
ROPE_PACK = LANES // (HEAD // 2)
ROPE_SUB = 256
LOG2E = math.log2(math.e)
Q_SCALE = HEAD ** -0.5 * LOG2E
CHUNK = 64
VMEM_LIMIT = 56 * 1024 * 1024


def _rms(x, g):
    return x * lax.rsqrt(jnp.mean(x * x, axis=-1, keepdims=True) + RMS_EPS) * g


def _dot(a, b):
    return jnp.dot(a.astype(BF16), b.astype(BF16), preferred_element_type=F32)


def _dot_nt(a, b):
    return lax.dot_general(a.astype(BF16), b.astype(BF16), (((1,), (1,)), ((), ())),
                           preferred_element_type=F32)


def _dot_tn(a, b):
    return lax.dot_general(a.astype(BF16), b.astype(BF16), (((0,), (0,)), ((), ())),
                           preferred_element_type=F32)


def _const_spec(shape):
    nd = len(shape)
    return pl.BlockSpec(shape, lambda *_: (0,) * nd)


def _drain(gen):
    for _ in gen:
        pass


def _interleave(main, fillers):
    fillers = [f if isinstance(f, tuple) else (f, 0, 1) for f in fillers]
    for n, _ in enumerate(main):
        for f, start, stride in fillers:
            if n >= start and (n - start) % stride == 0:
                next(f, None)
    for f, _, _ in fillers:
        _drain(f)


def _rwkv_pre_kernel(seq_tiles, sub, x_ref, xp_ref, g_ref, mu_ref, wr_ref, wk_ref, wv_ref,
                     w0_ref, w1_ref, w2_ref, a0_ref, a1_ref, a2_ref, g1_ref, g2_ref,
                     r_out, k_out, v_out, lw_out, a_out, g_out):
    i = pl.program_id(0)
    n_sub = x_ref.shape[0] // sub
    g = g_ref[...]
    mu = mu_ref[...]
    rows_of = lambda j: slice(j * sub, (j + 1) * sub)
    hp = _rms(xp_ref[...], g)
    carry_row = {0: jnp.where(i % seq_tiles == 0, 0.0, hp[7:8, :])}

    def mix(j, out):
        h = _rms(x_ref[rows_of(j), :], g)
        carry_row[j + 1] = h[sub - 1:sub, :]
        yield
        row = lax.broadcasted_iota(jnp.int32, h.shape, 0)
        dx = jnp.where(row == 0, carry_row[j], pltpu.roll(h, 1, axis=0)) - h
        yield
        for n, name in enumerate(("r", "w", "k", "v", "a", "g")):
            out[name] = (h + dx * mu[n:n + 1]).astype(BF16)
            if n % 2 == 1:
                yield

    def project(j, xs):
        rows = rows_of(j)
        dw = _dot(xs["w"], w1_ref[...])
        da = _dot(xs["a"], a1_ref[...])
        dg = _dot(xs["g"], g1_ref[...])
        yield
        r_out[rows, :] = jnp.dot(xs["r"], wr_ref[...], preferred_element_type=F32)
        yield
        w_pre = w0_ref[...] + _dot(jnp.tanh(dw), w2_ref[...])
        a_pre = a0_ref[...] + _dot(da, a2_ref[...])
        g_out[rows, :] = _dot(jax.nn.sigmoid(dg), g2_ref[...])
        yield
        k_out[rows, :] = jnp.dot(xs["k"], wk_ref[...], preferred_element_type=F32)
        yield
        lw_out[rows, :] = jax.nn.sigmoid(w_pre) * (-math.exp(-0.5))
        a_out[rows, :] = jax.nn.sigmoid(a_pre)
        yield
        v_out[rows, :] = jnp.dot(xs["v"], wv_ref[...], preferred_element_type=F32)
        yield

    xs = [dict() for _ in range(n_sub)]
    _drain(mix(0, xs[0]))
    for j in range(n_sub):
        fillers = [mix(j + 1, xs[j + 1])] if j + 1 < n_sub else []
        _interleave(project(j, xs[j]), fillers)


def _rwkv_pre(x2d, seq_len, g, mu, wr, wk, wv, w0, w1, w2, a0, a1, a2, g1, g2, tm=512, sub=256):
    t, d = x2d.shape
    seq_tiles = seq_len // tm
    row_spec = pl.BlockSpec((tm, d), lambda i: (i, 0))
    prev_spec = pl.BlockSpec((8, d), lambda i: (jnp.maximum(i * (tm // 8) - 1, 0), 0))
    consts = (g, mu, wr, wk, wv, w0, w1, w2, a0, a1, a2, g1, g2)
    out = jax.ShapeDtypeStruct((t, d), F32)
    return pl.pallas_call(
        functools.partial(_rwkv_pre_kernel, seq_tiles, sub),
        out_shape=(out,) * 6,
        grid=(t // tm,),
        in_specs=[row_spec, prev_spec] + [_const_spec(c.shape) for c in consts],
        out_specs=(row_spec,) * 6,
        compiler_params=pltpu.CompilerParams(
            dimension_semantics=("parallel",), vmem_limit_bytes=VMEM_LIMIT),
        name="rwkv_pre",
    )(x2d, x2d, *consts)


def _cumsum_rows(x):
    n = x.shape[0]
    row = lax.broadcasted_iota(jnp.int32, x.shape, 0)
    s = 1
    while s < n:
        x = x + jnp.where(row >= s, pltpu.roll(x, s, axis=0), 0.0)
        s *= 2
    return x


INV_BASE = 8
GROUP_HEADS = 4
GROUP_W = GROUP_HEADS * HEAD


def _head_sum(x):
    low = lax.broadcasted_iota(jnp.int32, (x.shape[0], LANES), 1) < HEAD
    outs = []
    for c in range(x.shape[1] // LANES):
        xc = x[:, c * LANES:(c + 1) * LANES]
        s_lo = jnp.sum(jnp.where(low, xc, 0.0), axis=-1, keepdims=True)
        s_hi = jnp.sum(jnp.where(low, 0.0, xc), axis=-1, keepdims=True)
        outs.append(jnp.where(low, s_lo, s_hi))
    return jnp.concatenate(outs, axis=1)


def _wkv_kernel(chunks, n_w, r_ref, k_ref, v_ref, lw_ref, a_ref,
                kk_ref, ka_ref, rk_ref, lnw_ref, lnb_ref, *rest):
    o_ref, state_ref = rest[n_w], rest[-1]
    w_refs = rest[:n_w] + rest[n_w + 1:-1]
    n_batch = r_ref.shape[0]
    n_groups = D_MODEL // GROUP_W
    probs = [(b, g) for b in range(n_batch) for g in range(n_groups)]
    ps = range(len(probs))

    @pl.when(pl.program_id(0) == 0)
    def _():
        state_ref[...] = jnp.zeros_like(state_ref)

    ri = lax.broadcasted_iota(jnp.int32, (CHUNK, GROUP_W), 0)
    lane = lax.broadcasted_iota(jnp.int32, (CHUNK, GROUP_W), 1)
    cj = lane & (HEAD - 1)
    lane_head = lax.shift_right_logical(lane, HEAD.bit_length() - 1)
    head_masks = [lane_head == h for h in range(GROUP_HEADS)]
    blk = lambda idx, size: lax.shift_right_logical(idx, size.bit_length() - 1)
    strict = cj < ri
    incl = cj <= ri
    eye = (cj == ri).astype(F32)
    diag = blk(ri, INV_BASE) == blk(cj, INV_BASE)
    bands = []
    size = INV_BASE
    while size < CHUNK:
        bands.append((blk(ri, 2 * size) == blk(cj, 2 * size))
                     & (blk(ri, size) != blk(cj, size)))
        size *= 2
    sr = lax.broadcasted_iota(jnp.int32, (GROUP_W, GROUP_W), 0)
    sc = lax.broadcasted_iota(jnp.int32, (GROUP_W, GROUP_W), 1)
    state_mask = blk(sr, HEAD) == blk(sc, HEAD)

    def bd(y):
        return jnp.concatenate([jnp.where(m, y, 0.0) for m in head_masks], axis=0).astype(BF16)

    def mm(x, w):
        return jnp.dot(x.astype(BF16), w, preferred_element_type=F32)

    def mm_nt(x, w):
        return lax.dot_general(x.astype(BF16), w, (((1,), (1,)), ((), ())),
                               preferred_element_type=F32)

    cols = [slice(g * GROUP_W, (g + 1) * GROUP_W) for g in range(n_groups)]
    rows_of = lambda c: slice(c * CHUNK, (c + 1) * CHUNK)


    def exact_zero(x):
        bits = pltpu.bitcast(x[:8, :LANES], jnp.uint32)
        half = jnp.uint32(16)
        zero = lax.shift_right_logical(lax.shift_right_logical(bits, half), half)
        return pltpu.bitcast(zero, F32)[:1, :1]

    def prepare(c, b, out, after=None):
        rows = rows_of(c)
        lw = lw_ref[b, rows, :]
        if after is not None:
            lw = lw + exact_zero(after())
        linc = _cumsum_rows(lw)
        yield
        p_inc = jnp.exp(linc)
        p_exc = jnp.exp(linc - lw)
        p_inv = jnp.exp(-linc)
        yield
        k = k_ref[b, rows, :]
        a = a_ref[b, rows, :]
        kk = k * kk_ref[...]
        kk = kk * lax.rsqrt(jnp.maximum(_head_sum(kk * kk), 1e-24))
        yield
        kmod = k * (1.0 + (a - 1.0) * ka_ref[...])
        at_f = (-kk * p_exc).astype(BF16)
        rt_f = (r_ref[b, rows, :] * p_inc).astype(BF16)
        yield
        out["lhs"] = [jnp.concatenate([at_f[:, cl], rt_f[:, cl]], axis=0) for cl in cols]
        out["bt"] = kk * a * p_inv
        out["kt"] = kmod * p_inv
        out["kmod"] = kmod
        out["p_end"] = p_inc[CHUNK - 1:CHUNK, :]
        yield

    def matmuls(c, ops, out):
        v = [v_ref[b, rows_of(c), :] for b in range(n_batch)]
        lhs = [ops[b]["lhs"][g] for b, g in probs]
        bt = [ops[b]["bt"][:, cols[g]] for b, g in probs]
        kt = [ops[b]["kt"][:, cols[g]] for b, g in probs]
        vg = [v[b][:, cols[g]] for b, g in probs]
        ab = [mm_nt(lhs[p], bd(bt[p])) for p in ps]
        ak = [mm_nt(lhs[p], bd(kt[p])) for p in ps]
        yield
        a_ab = [jnp.where(strict, x[:CHUNK], 0.0) for x in ab]
        a_rb = [jnp.where(incl, x[CHUNK:], 0.0) for x in ab]
        a_k = [jnp.concatenate([jnp.where(strict, x[:CHUNK], 0.0),
                                jnp.where(incl, x[CHUNK:], 0.0)], axis=0) for x in ak]
        d1 = [jnp.where(diag, x, 0.0) for x in a_ab]
        d2 = [mm(d1[p], bd(d1[p])) for p in ps]
        yield
        t = [eye + x for x in d1]
        td = [mm(jnp.concatenate([t[p], d2[p]], axis=0), bd(d2[p])) for p in ps]
        yield
        t = [t[p] + td[p][:CHUNK] for p in ps]
        out["anchors"] = [td[0]]
        t = [t[p] + mm(t[p], bd(td[p][CHUNK:])) for p in ps]
        yield
        for band in bands:
            te = [mm(t[p], bd(jnp.where(band, a_ab[p], 0.0))) for p in ps]
            out["anchors"].append(te[0])
            yield
            t = [t[p] + mm(te[p], bd(t[p])) for p in ps]
            yield
        s0 = [state_ref[p] for p in ps]
        zy = [mm_nt(lhs[p], s0[p].astype(BF16)) for p in ps]
        av = [mm(a_k[p], bd(vg[p])) for p in ps]
        yield
        u = [mm(t[p], bd(zy[p][:CHUNK] + av[p][:CHUNK])) for p in ps]
        yield
        y = [zy[p][CHUNK:] + av[p][CHUNK:] + mm(a_rb[p], bd(u[p])) for p in ps]
        for p, (b, g) in enumerate(probs):
            uv = jnp.concatenate([u[p], vg[p]], axis=0)
            bk = jnp.concatenate([bt[p], kt[p]], axis=0)
            state_ref[p] = (jnp.where(state_mask, s0[p] + _dot_tn(uv, bk), 0.0)
                            * ops[b]["p_end"][:, cols[g]])
        out["y"] = [jnp.concatenate(y[b * n_groups:(b + 1) * n_groups], axis=1)
                    for b in range(n_batch)]
        yield

    def finish(c, b, ops, y_f, after=None):
        rows = rows_of(c)
        inv_n = 1.0 / HEAD
        if after is not None:
            y_f = y_f + exact_zero(after())
        yc = y_f - _head_sum(y_f) * inv_n
        yield
        var = _head_sum(yc * yc) * inv_n
        yn = yc * lax.rsqrt(var + GN_EPS) * lnw_ref[...] + lnb_ref[...]
        yield
        bonus = _head_sum(r_ref[b, rows, :] * ops["kmod"] * rk_ref[...])
        o_ref[b, rows, :] = yn + bonus * v_ref[b, rows, :]
        yield

    def cast_weights():
        for src, dst in zip(w_refs[:len(w_refs) // 2], w_refs[len(w_refs) // 2:]):
            dst[...] = src[...].astype(dst.dtype)
            yield

    bs = range(n_batch)
    ops = [[dict() for _ in bs] for _ in range(chunks)]
    res = [dict() for _ in range(chunks)]
    for b in bs:
        _drain(prepare(0, b, ops[0][b]))
    for c in range(chunks):
        fillers = [cast_weights()] if c == 0 else []
        tie = lambda n, c=c: (lambda: res[c]["anchors"][n])
        if c + 1 < chunks:
            fillers += [(prepare(c + 1, b, ops[c + 1][b], tie(b)), 3 + b, 1) for b in bs]
        if c > 0:
            fillers += [(finish(c - 1, b, ops[c - 1][b], res[c - 1]["y"][b], tie(2 + b)),
                         6 + 2 * b, 1) for b in bs]
        _interleave(matmuls(c, ops[c], res[c]), fillers)
    for b in bs:
        _drain(finish(chunks - 1, b, ops[chunks - 1][b], res[chunks - 1]["y"][b]))


def _wkv(r, k, v, lw, a, kk_p, ka_p, rk_p, lnw_p, lnb_p, weights, rows=256):
    b, s, d = r.shape
    steps = s // rows
    seq_spec = pl.BlockSpec((b, rows, d), lambda ci: (0, ci, 0))
    par_spec = pl.BlockSpec((1, d), lambda ci: (0, 0))
    w_specs = [pl.BlockSpec((w.shape[0] // steps, w.shape[1]), lambda ci: (ci, 0))
               for w in weights]
    w_out = [jax.ShapeDtypeStruct(w.shape, BF16) for w in weights]
    outs = pl.pallas_call(
        functools.partial(_wkv_kernel, rows // CHUNK, len(weights)),
        out_shape=[jax.ShapeDtypeStruct((b, s, d), F32)] + w_out,
        grid=(steps,),
        in_specs=[seq_spec] * 5 + [par_spec] * 5 + w_specs,
        out_specs=[seq_spec] + w_specs,
        scratch_shapes=[pltpu.VMEM((b * d // GROUP_W, GROUP_W, GROUP_W), F32)],
        compiler_params=pltpu.CompilerParams(
            dimension_semantics=("arbitrary",), vmem_limit_bytes=VMEM_LIMIT),
        name="wkv",
    )(r, k, v, lw, a, kk_p, ka_p, rk_p, lnw_p, lnb_p, *weights)
    return outs[0], outs[1:]


def _proj_mlp_kernel(has_gate, final_norm, with_rope, ff_chunk, sub, *refs):
    refs = list(refs)
    y_ref = refs.pop(0)
    gate_ref = refs.pop(0) if has_gate else None
    x_ref, wo_ref, bo_ref, gm_ref, win_ref, wout_ref = refs[:6]
    refs = refs[6:]
    gf_ref = refs.pop(0) if final_norm else None
    if with_rope:
        pos_ref, freq_ref = refs.pop(0), refs.pop(0)
    o_ref = refs.pop(0)
    if with_rope:
        cos_ref, sin_ref = refs.pop(0), refs.pop(0)

    def rope_tables(part, parts, pieces=2):
        rows = pos_ref.shape[0] // (parts * pieces)
        for n in range(part * pieces, (part + 1) * pieces):
            sl = slice(n * rows, (n + 1) * rows)
            ang = pos_ref[sl, :] * freq_ref[...]
            cos_ref[sl, :] = jnp.cos(ang)
            sin_ref[sl, :] = jnp.sin(ang)
            yield

    n_sub = x_ref.shape[0] // sub
    rows_of = lambda j: slice(j * sub, (j + 1) * sub)

    def head(j, out):
        rows = rows_of(j)
        y = y_ref[rows, :].astype(F32)
        if has_gate:
            y = y * gate_ref[rows, :]
        yield
        x1 = x_ref[rows, :] + _dot(y, wo_ref[...]) + bo_ref[...]
        yield
        out["x1"] = x1
        out["hb"] = _rms(x1, gm_ref[...]).astype(BF16)
        yield

    def mlp(j, ins, out):
        acc = ins["x1"]
        for c in range(D_FF // ff_chunk):
            cols = slice(c * ff_chunk, (c + 1) * ff_chunk)
            hid = jnp.maximum(jnp.dot(ins["hb"], win_ref[:, cols], preferred_element_type=F32), 0.0)
            yield
            acc = acc + _dot(hid * hid, wout_ref[cols, :])
            yield
        out["acc"] = acc

    def tail(j, acc):
        if final_norm:
            acc = _rms(acc, gf_ref[...])
        o_ref[rows_of(j), :] = acc
        yield

    ins = [dict() for _ in range(n_sub)]
    outs = [dict() for _ in range(n_sub)]
    _drain(head(0, ins[0]))
    for j in range(n_sub):
        fillers = []
        if j + 1 < n_sub:
            fillers.append(head(j + 1, ins[j + 1]))
        if j > 0:
            fillers.append(tail(j - 1, outs[j - 1]["acc"]))
        if with_rope:
            fillers.append(rope_tables(j, n_sub))
        _interleave(mlp(j, ins[j], outs[j]), fillers)
    _drain(tail(n_sub - 1, outs[n_sub - 1]["acc"]))


def _proj_mlp(y, gate, x, wo, bo, gm, win_all, wout_all, layer, gf, rope=None, tm=1024,
              sub=256, ff_chunk=1024):
    t, d = x.shape
    row_spec = pl.BlockSpec((tm, d), lambda i: (i, 0))
    args = [y] + ([gate] if gate is not None else []) + [x, wo, bo, gm, win_all, wout_all]
    resident = lambda shape, blk: pl.BlockSpec(shape, lambda i: (blk, 0),
                                               pipeline_mode=pl.Buffered(1))
    specs = [row_spec] * (len(args) - 5) + [
        resident(wo.shape, 0), resident(bo.shape, 0), resident(gm.shape, 0),
        resident((d, D_FF), layer), resident((D_FF, d), layer)]
    if gf is not None:
        args.append(gf)
        specs.append(_const_spec(gf.shape))
    out_shape = [jax.ShapeDtypeStruct((t, d), F32)]
    out_specs = [row_spec]
    if rope is not None:
        pos_packed, freq_row = rope
        packed_spec = pl.BlockSpec((tm // ROPE_PACK, LANES), lambda i: (i, 0))
        args += [pos_packed, freq_row]
        specs += [packed_spec, _const_spec(freq_row.shape)]
        out_shape += [jax.ShapeDtypeStruct(pos_packed.shape, F32)] * 2
        out_specs += [packed_spec] * 2
    outs = pl.pallas_call(
        functools.partial(_proj_mlp_kernel, gate is not None, gf is not None,
                          rope is not None, ff_chunk, sub),
        out_shape=out_shape,
        grid=(t // tm,),
        in_specs=specs,
        out_specs=out_specs,
        compiler_params=pltpu.CompilerParams(
            dimension_semantics=("parallel",), vmem_limit_bytes=VMEM_LIMIT),
        name="proj_mlp",
    )(*args)
    return outs[0] if rope is None else outs


def _pack_positions(pos):
    per = ROPE_SUB // ROPE_PACK
    grouped = pos.astype(F32).reshape(-1, ROPE_PACK, per).transpose(0, 2, 1)
    return jnp.repeat(grouped, LANES // ROPE_PACK, axis=-1).reshape(-1, LANES)


def _attn_qkv_kernel(sub, x_ref, g_ref, w_ref, b_ref, cos_ref, sin_ref, o_ref):
    n_sub = x_ref.shape[0] // sub
    rows_of = lambda j: slice(j * sub, (j + 1) * sub)
    lane = lax.broadcasted_iota(jnp.int32, (sub, LANES), 1)
    first_half = lane % HEAD < HEAD // 2
    n_q = N_HEADS * HEAD // LANES
    n_rot = (N_HEADS + N_KV_HEADS) * HEAD // LANES
    n_blk = 2 * LANES

    def norm(j, out):
        out["h"] = _rms(x_ref[rows_of(j), :], g_ref[...]).astype(BF16)
        yield

    def project(j, ins, out):
        out["qkv"] = []
        for c in range(QKV_DIM // n_blk):
            cols = slice(c * n_blk, (c + 1) * n_blk)
            out["qkv"].append(jnp.dot(ins["h"], w_ref[:, cols], preferred_element_type=F32)
                              + b_ref[:, cols])
            yield

    def unpack(packed):
        quarter = lax.shift_right_logical(
            lax.broadcasted_iota(jnp.int32, packed.shape, 1), (LANES // ROPE_PACK).bit_length() - 1)
        rolled = [packed] + [pltpu.roll(packed, n * (LANES // ROPE_PACK), axis=1)
                             for n in range(1, ROPE_PACK)]
        blocks = []
        for n in range(ROPE_PACK):
            blk = rolled[(ROPE_PACK - 1 - n) % ROPE_PACK]
            for qq in range(ROPE_PACK - 2, -1, -1):
                blk = jnp.where(quarter == qq, rolled[(qq - n) % ROPE_PACK], blk)
            blocks.append(blk)
        return jnp.concatenate(blocks, axis=0)

    def rotary(j, qkv):
        rows = rows_of(j)
        packed_rows = slice(j * sub // ROPE_PACK, (j + 1) * sub // ROPE_PACK)
        cos = unpack(cos_ref[packed_rows, :])
        sin = jnp.where(first_half, -1.0, 1.0) * unpack(sin_ref[packed_rows, :])
        yield
        for s in range(QKV_DIM // LANES):
            c, half = divmod(s, n_blk // LANES)
            blk = qkv[c][:, half * LANES:(half + 1) * LANES]
            if s < n_rot:
                rot = jnp.where(first_half, pltpu.roll(blk, LANES - HEAD // 2, axis=1),
                                pltpu.roll(blk, HEAD // 2, axis=1))
                blk = blk * cos + rot * sin
            if s < n_q:
                blk = blk * Q_SCALE
            o_ref[rows, s * LANES:(s + 1) * LANES] = blk.astype(o_ref.dtype)
            if s % 2 == 1:
                yield

    ins = [dict() for _ in range(n_sub)]
    outs = [dict() for _ in range(n_sub)]
    _drain(norm(0, ins[0]))
    for j in range(n_sub):
        fillers = []
        if j + 1 < n_sub:
            fillers.append(norm(j + 1, ins[j + 1]))
        if j > 0:
            fillers.append(rotary(j - 1, outs[j - 1]["qkv"]))
        _interleave(project(j, ins[j], outs[j]), fillers)
    _drain(rotary(n_sub - 1, outs[n_sub - 1]["qkv"]))


def _attn_qkv(x, g, w, b, cos, sin, tm=1024, sub=ROPE_SUB):
    t, d = x.shape
    row = lambda width: pl.BlockSpec((tm, width), lambda i: (i, 0))
    packed = pl.BlockSpec((tm // ROPE_PACK, LANES), lambda i: (i, 0))
    return pl.pallas_call(
        functools.partial(_attn_qkv_kernel, sub),
        out_shape=jax.ShapeDtypeStruct((t, QKV_DIM), BF16),
        grid=(t // tm,),
        in_specs=[row(d), _const_spec(g.shape), _const_spec(w.shape), _const_spec(b.shape),
                  packed, packed],
        out_specs=row(QKV_DIM),
        compiler_params=pltpu.CompilerParams(
            dimension_semantics=("parallel",), vmem_limit_bytes=VMEM_LIMIT),
        name="attn_qkv",
    )(x, g, w, b, cos, sin)


def _attn_core_kernel(q_blocks, q_ref, kp_ref, kc_ref, vp_ref, vc_ref, sink_ref, o_ref):
    n = pl.program_id(1)
    kr = lax.broadcasted_iota(jnp.int32, (2 * WINDOW, WINDOW), 0)
    qc = lax.broadcasted_iota(jnp.int32, (2 * WINDOW, WINDOW), 1)
    own = kr >= WINDOW
    band = (own & (kr - WINDOW <= qc)) | (jnp.logical_not(own) & (kr > qc))
    first_band = band & (own | (n > 0))
    neg = -jnp.inf
    probs = [(i, j) for i in range(q_blocks) for j in range(N_KV_HEADS)]
    blk = lambda i: slice(i * WINDOW, (i + 1) * WINDOW)
    kvl = lambda j: slice(j * HEAD, (j + 1) * HEAD)
    head = lambda j, g: j * GROUP + g

    def keys(prev_ref, cur_ref, i, j):
        prev = prev_ref[0, :, kvl(j)] if i == 0 else cur_ref[0, blk(i - 1), kvl(j)]
        return jnp.concatenate([prev, cur_ref[0, blk(i), kvl(j)]], axis=0)

    q = [jnp.concatenate([q_ref[0, blk(i), head(j, g) * HEAD:(head(j, g) + 1) * HEAD]
                          for g in range(GROUP)], axis=0) for i, j in probs]
    s_t = [_dot_nt(keys(kp_ref, kc_ref, i, j), q[p]) for p, (i, j) in enumerate(probs)]
    p_t = [[] for _ in probs]
    for g in range(GROUP):
        sink = [sink_ref[:, head(j, g):head(j, g) + 1] * LOG2E for _, j in probs]
        x = [jnp.where(first_band if i == 0 else band, s_t[p][:, blk(g)], neg)
             for p, (i, j) in enumerate(probs)]
        m = [jnp.maximum(jnp.max(x[p], axis=0, keepdims=True), sink[p])
             for p in range(len(probs))]
        e = [jnp.exp2(x[p] - m[p]) for p in range(len(probs))]
        den = [jnp.sum(e[p], axis=0, keepdims=True) + jnp.exp2(sink[p] - m[p])
               for p in range(len(probs))]
        for p in range(len(probs)):
            p_t[p].append((e[p] * (1.0 / den[p])).astype(BF16))
    for p, (i, j) in enumerate(probs):
        o = lax.dot_general(jnp.concatenate(p_t[p], axis=1), keys(vp_ref, vc_ref, i, j),
                            (((0,), (0,)), ((), ())), preferred_element_type=F32)
        for g in range(GROUP):
            o_ref[0, blk(i), head(j, g) * HEAD:(head(j, g) + 1) * HEAD] = (
                o[blk(g)].astype(o_ref.dtype))


def _attn_core(qkv, sinks, q_blocks=8):
    b, s, _ = qkv.shape
    nq = N_HEADS * HEAD
    tq = q_blocks * WINDOW
    k_blk = nq // LANES
    v_blk = k_blk + N_KV_HEADS * HEAD // LANES
    q_spec = pl.BlockSpec((1, tq, nq), lambda bi, n: (bi, n, 0))
    cur = lambda blk: pl.BlockSpec((1, tq, LANES), lambda bi, n: (bi, n, blk))
    prev = lambda blk: pl.BlockSpec((1, WINDOW, LANES),
                                    lambda bi, n: (bi, jnp.maximum(n * q_blocks - 1, 0), blk))
    return pl.pallas_call(
        functools.partial(_attn_core_kernel, q_blocks),
        out_shape=jax.ShapeDtypeStruct((b, s, nq), BF16),
        grid=(b, s // tq),
        in_specs=[q_spec, prev(k_blk), cur(k_blk), prev(v_blk), cur(v_blk),
                  _const_spec(sinks.shape)],
        out_specs=q_spec,
        compiler_params=pltpu.CompilerParams(
            dimension_semantics=("parallel", "arbitrary"), vmem_limit_bytes=VMEM_LIMIT),
        name="attn_core",
    )(qkv, qkv, qkv, qkv, qkv, sinks)


def kernel(x, positions, norm_mix_g, norm_mlp_g, norm_final_g, rwkv_mu, rwkv_w_r, rwkv_w_k, rwkv_w_v, rwkv_w_o, rwkv_w0, rwkv_w1, rwkv_w2, rwkv_a0, rwkv_a1, rwkv_a2, rwkv_g1, rwkv_g2, rwkv_k_k, rwkv_k_a, rwkv_r_k, rwkv_ln_w, rwkv_ln_b, attn_w_qkv, attn_b_qkv, attn_sinks, attn_w_o, attn_b_o, mlp_w_in, mlp_w_out):
    b, s, d = x.shape
    t = b * s
    x2d = x.reshape(t, d)
    row = lambda p: p.reshape(1, -1)
    bf = lambda w: w.astype(BF16)
    zero_bias = jnp.zeros((1, d), F32)

    r, k, v, lw, a, gate = _rwkv_pre(
        x2d, s, row(norm_mix_g[0]), rwkv_mu[0], bf(rwkv_w_r[0]), bf(rwkv_w_k[0]),
        bf(rwkv_w_v[0]), row(rwkv_w0[0]), bf(rwkv_w1[0]), bf(rwkv_w2[0]), row(rwkv_a0[0]),
        bf(rwkv_a1[0]), bf(rwkv_a2[0]), bf(rwkv_g1[0]), bf(rwkv_g2[0]))
    seq = lambda z: z.reshape(b, s, d)
    later_weights = [rwkv_w_o[0], attn_w_qkv[0], attn_w_o[0],
                     mlp_w_in.reshape(-1, D_FF), mlp_w_out.reshape(-1, d)]
    y, (w_o0, w_qkv, w_o1, w_in_all, w_out_all) = _wkv(
        seq(r), seq(k), seq(v), seq(lw), seq(a), row(rwkv_k_k[0]), row(rwkv_k_a[0]),
        row(rwkv_r_k[0]), row(rwkv_ln_w[0]), row(rwkv_ln_b[0]), later_weights)
    inv_freq = ROPE_THETA ** (-jnp.arange(0, HEAD, 2, dtype=F32) / HEAD)
    freq_row = jnp.tile(inv_freq, LANES // (HEAD // 2)).reshape(1, LANES)
    x2d, cos, sin = _proj_mlp(y.reshape(t, d), gate, x2d, w_o0, zero_bias,
                              row(norm_mlp_g[0]), w_in_all, w_out_all, 0, None,
                              rope=(_pack_positions(positions.reshape(t)), freq_row))

    qkv = _attn_qkv(x2d, row(norm_mix_g[1]), w_qkv, row(attn_b_qkv[0]), cos, sin)
    o = _attn_core(qkv.reshape(b, s, QKV_DIM), row(attn_sinks[0]))
    out = _proj_mlp(o.reshape(t, d), None, x2d, w_o1, row(attn_b_o[0]),
                    row(norm_mlp_g[1]), w_in_all, w_out_all, 1, row(norm_final_g))
    return out.reshape(b, s, d)
```

```python
import functools
import math

import jax
import jax.numpy as jnp
from jax import lax
from jax.experimental import pallas as pl
from jax.experimental.pallas import tpu as pltpu

F32 = jnp.float32
BF16 = jnp.bfloat16

D_MODEL = 1024
HEAD = 64
N_HEADS = D_MODEL // HEAD
N_KV_HEADS = 2
GROUP = N_HEADS // N_KV_HEADS
WINDOW = 128
QKV_DIM = (N_HEADS + 2 * N_KV_HEADS) * HEAD
D_FF = 4 * D_MODEL
ROPE_THETA = 10000.0
RMS_EPS = 1e-5
GN_EPS = 64e-5

LANES = 128
ROPE_PACK = LANES // (HEAD // 2)
ROPE_SUB = 256
LOG2E = math.log2(math.e)
Q_SCALE = HEAD ** -0.5 * LOG2E
CHUNK = 64
VMEM_LIMIT = 56 * 1024 * 1024


def _rms(x, g):
    return x * lax.rsqrt(jnp.mean(x * x, axis=-1, keepdims=True) + RMS_EPS) * g


def _dot(a, b):
    return jnp.dot(a.astype(BF16), b.astype(BF16), preferred_element_type=F32)


def _dot_nt(a, b):
    return lax.dot_general(a.astype(BF16), b.astype(BF16), (((1,), (1,)), ((), ())),
                           preferred_element_type=F32)


def _dot_tn(a, b):
    return lax.dot_general(a.astype(BF16), b.astype(BF16), (((0,), (0,)), ((), ())),
                           preferred_element_type=F32)


def _const_spec(shape):
    nd = len(shape)
    return pl.BlockSpec(shape, lambda *_: (0,) * nd)


def _drain(gen):
    for _ in gen:
        pass


def _interleave(main, fillers):
    fillers = [f if isinstance(f, tuple) else (f, 0, 1) for f in fillers]
    for n, _ in enumerate(main):
        for f, start, stride in fillers:
            if n >= start and (n - start) % stride == 0:
                next(f, None)
    for f, _, _ in fillers:
        _drain(f)


def _rwkv_pre_kernel(seq_tiles, sub, x_ref, xp_ref, g_ref, mu_ref, wr_ref, wk_ref, wv_ref,
                     w0_ref, w1_ref, w2_ref, a0_ref, a1_ref, a2_ref, g1_ref, g2_ref,
                     r_out, k_out, v_out, lw_out, a_out, g_out):
    i = pl.program_id(0)
    n_sub = x_ref.shape[0] // sub
    g = g_ref[...]
    mu = mu_ref[...]
    rows_of = lambda j: slice(j * sub, (j + 1) * sub)
    hp = _rms(xp_ref[...], g)
    carry_row = {0: jnp.where(i % seq_tiles == 0, 0.0, hp[7:8, :])}

    def mix(j, out):
        h = _rms(x_ref[rows_of(j), :], g)
        carry_row[j + 1] = h[sub - 1:sub, :]
        yield
        row = lax.broadcasted_iota(jnp.int32, h.shape, 0)
        dx = jnp.where(row == 0, carry_row[j], pltpu.roll(h, 1, axis=0)) - h
        yield
        for n, name in enumerate(("r", "w", "k", "v", "a", "g")):
            out[name] = (h + dx * mu[n:n + 1]).astype(BF16)
            if n % 2 == 1:
                yield

    def project(j, xs):
        rows = rows_of(j)
        dw = _dot(xs["w"], w1_ref[...])
        da = _dot(xs["a"], a1_ref[...])
        dg = _dot(xs["g"], g1_ref[...])
        yield
        r_out[rows, :] = jnp.dot(xs["r"], wr_ref[...], preferred_element_type=F32)
        yield
        w_pre = w0_ref[...] + _dot(jnp.tanh(dw), w2_ref[...])
        a_pre = a0_ref[...] + _dot(da, a2_ref[...])
        g_out[rows, :] = _dot(jax.nn.sigmoid(dg), g2_ref[...])
        yield
        k_out[rows, :] = jnp.dot(xs["k"], wk_ref[...], preferred_element_type=F32)
        yield
        lw_out[rows, :] = jax.nn.sigmoid(w_pre) * (-math.exp(-0.5))
        a_out[rows, :] = jax.nn.sigmoid(a_pre)
        yield
        v_out[rows, :] = jnp.dot(xs["v"], wv_ref[...], preferred_element_type=F32)
        yield

    xs = [dict() for _ in range(n_sub)]
    _drain(mix(0, xs[0]))
    for j in range(n_sub):
        fillers = [mix(j + 1, xs[j + 1])] if j + 1 < n_sub else []
        _interleave(project(j, xs[j]), fillers)


def _rwkv_pre(x2d, seq_len, g, mu, wr, wk, wv, w0, w1, w2, a0, a1, a2, g1, g2, tm=512, sub=256):
    t, d = x2d.shape
    seq_tiles = seq_len // tm
    row_spec = pl.BlockSpec((tm, d), lambda i: (i, 0))
    prev_spec = pl.BlockSpec((8, d), lambda i: (jnp.maximum(i * (tm // 8) - 1, 0), 0))
    consts = (g, mu, wr, wk, wv, w0, w1, w2, a0, a1, a2, g1, g2)
    out = jax.ShapeDtypeStruct((t, d), F32)
    return pl.pallas_call(
        functools.partial(_rwkv_pre_kernel, seq_tiles, sub),
        out_shape=(out,) * 6,
        grid=(t // tm,),
        in_specs=[row_spec, prev_spec] + [_const_spec(c.shape) for c in consts],
        out_specs=(row_spec,) * 6,
        compiler_params=pltpu.CompilerParams(
            dimension_semantics=("parallel",), vmem_limit_bytes=VMEM_LIMIT),
        name="rwkv_pre",
    )(x2d, x2d, *consts)


def _cumsum_rows(x):
    n = x.shape[0]
    row = lax.broadcasted_iota(jnp.int32, x.shape, 0)
    s = 1
    while s < n:
        x = x + jnp.where(row >= s, pltpu.roll(x, s, axis=0), 0.0)
        s *= 2
    return x


INV_BASE = 8
GROUP_HEADS = 4
GROUP_W = GROUP_HEADS * HEAD


def _head_sum(x):
    low = lax.broadcasted_iota(jnp.int32, (x.shape[0], LANES), 1) < HEAD
    outs = []
    for c in range(x.shape[1] // LANES):
        xc = x[:, c * LANES:(c + 1) * LANES]
        s_lo = jnp.sum(jnp.where(low, xc, 0.0), axis=-1, keepdims=True)
        s_hi = jnp.sum(jnp.where(low, 0.0, xc), axis=-1, keepdims=True)
        outs.append(jnp.where(low, s_lo, s_hi))
    return jnp.concatenate(outs, axis=1)


def _wkv_kernel(chunks, n_w, r_ref, k_ref, v_ref, lw_ref, a_ref,
                kk_ref, ka_ref, rk_ref, lnw_ref, lnb_ref, *rest):
    o_ref, state_ref = rest[n_w], rest[-1]
    w_refs = rest[:n_w] + rest[n_w + 1:-1]
    n_batch = r_ref.shape[0]
    n_groups = D_MODEL // GROUP_W
    probs = [(b, g) for b in range(n_batch) for g in range(n_groups)]
    ps = range(len(probs))

    @pl.when(pl.program_id(0) == 0)
    def _():
        state_ref[...] = jnp.zeros_like(state_ref)

    ri = lax.broadcasted_iota(jnp.int32, (CHUNK, GROUP_W), 0)
    lane = lax.broadcasted_iota(jnp.int32, (CHUNK, GROUP_W), 1)
    cj = lane & (HEAD - 1)
    lane_head = lax.shift_right_logical(lane, HEAD.bit_length() - 1)
    head_masks = [lane_head == h for h in range(GROUP_HEADS)]
    blk = lambda idx, size: lax.shift_right_logical(idx, size.bit_length() - 1)
    strict = cj < ri
    incl = cj <= ri
    eye = (cj == ri).astype(F32)
    diag = blk(ri, INV_BASE) == blk(cj, INV_BASE)
    bands = []
    size = INV_BASE
    while size < CHUNK:
        bands.append((blk(ri, 2 * size) == blk(cj, 2 * size))
                     & (blk(ri, size) != blk(cj, size)))
        size *= 2
    sr = lax.broadcasted_iota(jnp.int32, (GROUP_W, GROUP_W), 0)
    sc = lax.broadcasted_iota(jnp.int32, (GROUP_W, GROUP_W), 1)
    state_mask = blk(sr, HEAD) == blk(sc, HEAD)

    def bd(y):
        return jnp.concatenate([jnp.where(m, y, 0.0) for m in head_masks], axis=0).astype(BF16)

    def mm(x, w):
        return jnp.dot(x.astype(BF16), w, preferred_element_type=F32)

    def mm_nt(x, w):
        return lax.dot_general(x.astype(BF16), w, (((1,), (1,)), ((), ())),
                               preferred_element_type=F32)

    cols = [slice(g * GROUP_W, (g + 1) * GROUP_W) for g in range(n_groups)]
    rows_of = lambda c: slice(c * CHUNK, (c + 1) * CHUNK)


    def exact_zero(x):
        bits = pltpu.bitcast(x[:8, :LANES], jnp.uint32)
        half = jnp.uint32(16)
        zero = lax.shift_right_logical(lax.shift_right_logical(bits, half), half)
        return pltpu.bitcast(zero, F32)[:1, :1]

    def prepare(c, b, out, after=None):
        rows = rows_of(c)
        lw = lw_ref[b, rows, :]
        if after is not None:
            lw = lw + exact_zero(after())
        linc = _cumsum_rows(lw)
        yield
        p_inc = jnp.exp(linc)
        p_exc = jnp.exp(linc - lw)
        p_inv = jnp.exp(-linc)
        yield
        k = k_ref[b, rows, :]
        a = a_ref[b, rows, :]
        kk = k * kk_ref[...]
        kk = kk * lax.rsqrt(jnp.maximum(_head_sum(kk * kk), 1e-24))
        yield
        kmod = k * (1.0 + (a - 1.0) * ka_ref[...])
        at_f = (-kk * p_exc).astype(BF16)
        rt_f = (r_ref[b, rows, :] * p_inc).astype(BF16)
        yield
        out["lhs"] = [jnp.concatenate([at_f[:, cl], rt_f[:, cl]], axis=0) for cl in cols]
        out["bt"] = kk * a * p_inv
        out["kt"] = kmod * p_inv
        out["kmod"] = kmod
        out["p_end"] = p_inc[CHUNK - 1:CHUNK, :]
        yield

    def matmuls(c, ops, out):
        v = [v_ref[b, rows_of(c), :] for b in range(n_batch)]
        lhs = [ops[b]["lhs"][g] for b, g in probs]
        bt = [ops[b]["bt"][:, cols[g]] for b, g in probs]
        kt = [ops[b]["kt"][:, cols[g]] for b, g in probs]
        vg = [v[b][:, cols[g]] for b, g in probs]
        ab = [mm_nt(lhs[p], bd(bt[p])) for p in ps]
        ak = [mm_nt(lhs[p], bd(kt[p])) for p in ps]
        yield
        a_ab = [jnp.where(strict, x[:CHUNK], 0.0) for x in ab]
        a_rb = [jnp.where(incl, x[CHUNK:], 0.0) for x in ab]
        a_k = [jnp.concatenate([jnp.where(strict, x[:CHUNK], 0.0),
                                jnp.where(incl, x[CHUNK:], 0.0)], axis=0) for x in ak]
        d1 = [jnp.where(diag, x, 0.0) for x in a_ab]
        d2 = [mm(d1[p], bd(d1[p])) for p in ps]
        yield
        t = [eye + x for x in d1]
        td = [mm(jnp.concatenate([t[p], d2[p]], axis=0), bd(d2[p])) for p in ps]
        yield
        t = [t[p] + td[p][:CHUNK] for p in ps]
        out["anchors"] = [td[0]]
        t = [t[p] + mm(t[p], bd(td[p][CHUNK:])) for p in ps]
        yield
        for band in bands:
            te = [mm(t[p], bd(jnp.where(band, a_ab[p], 0.0))) for p in ps]
            out["anchors"].append(te[0])
            yield
            t = [t[p] + mm(te[p], bd(t[p])) for p in ps]
            yield
        s0 = [state_ref[p] for p in ps]
        zy = [mm_nt(lhs[p], s0[p].astype(BF16)) for p in ps]
        av = [mm(a_k[p], bd(vg[p])) for p in ps]
        yield
        u = [mm(t[p], bd(zy[p][:CHUNK] + av[p][:CHUNK])) for p in ps]
        yield
        y = [zy[p][CHUNK:] + av[p][CHUNK:] + mm(a_rb[p], bd(u[p])) for p in ps]
        for p, (b, g) in enumerate(probs):
            uv = jnp.concatenate([u[p], vg[p]], axis=0)
            bk = jnp.concatenate([bt[p], kt[p]], axis=0)
            state_ref[p] = (jnp.where(state_mask, s0[p] + _dot_tn(uv, bk), 0.0)
                            * ops[b]["p_end"][:, cols[g]])
        out["y"] = [jnp.concatenate(y[b * n_groups:(b + 1) * n_groups], axis=1)
                    for b in range(n_batch)]
        yield

    def finish(c, b, ops, y_f, after=None):
        rows = rows_of(c)
        inv_n = 1.0 / HEAD
        if after is not None:
            y_f = y_f + exact_zero(after())
        yc = y_f - _head_sum(y_f) * inv_n
        yield
        var = _head_sum(yc * yc) * inv_n
        yn = yc * lax.rsqrt(var + GN_EPS) * lnw_ref[...] + lnb_ref[...]
        yield
        bonus = _head_sum(r_ref[b, rows, :] * ops["kmod"] * rk_ref[...])
        o_ref[b, rows, :] = yn + bonus * v_ref[b, rows, :]
        yield

    def cast_weights():
        for src, dst in zip(w_refs[:len(w_refs) // 2], w_refs[len(w_refs) // 2:]):
            dst[...] = src[...].astype(dst.dtype)
            yield

    bs = range(n_batch)
    ops = [[dict() for _ in bs] for _ in range(chunks)]
    res = [dict() for _ in range(chunks)]
    for b in bs:
        _drain(prepare(0, b, ops[0][b]))
    for c in range(chunks):
        fillers = [cast_weights()] if c == 0 else []
        tie = lambda n, c=c: (lambda: res[c]["anchors"][n])
        if c + 1 < chunks:
            fillers += [(prepare(c + 1, b, ops[c + 1][b], tie(1 + b)), 4 + 2 * b, 1) for b in bs]
        if c > 0:
            fillers += [(finish(c - 1, b, ops[c - 1][b], res[c - 1]["y"][b], tie(3 * b)),
                         3 + 5 * b, 1) for b in bs]
        _interleave(matmuls(c, ops[c], res[c]), fillers)
    for b in bs:
        _drain(finish(chunks - 1, b, ops[chunks - 1][b], res[chunks - 1]["y"][b]))


def _wkv(r, k, v, lw, a, kk_p, ka_p, rk_p, lnw_p, lnb_p, weights, rows=256):
    b, s, d = r.shape
    steps = s // rows
    seq_spec = pl.BlockSpec((b, rows, d), lambda ci: (0, ci, 0))
    par_spec = pl.BlockSpec((1, d), lambda ci: (0, 0))
    w_specs = [pl.BlockSpec((w.shape[0] // steps, w.shape[1]), lambda ci: (ci, 0))
               for w in weights]
    w_out = [jax.ShapeDtypeStruct(w.shape, BF16) for w in weights]
    outs = pl.pallas_call(
        functools.partial(_wkv_kernel, rows // CHUNK, len(weights)),
        out_shape=[jax.ShapeDtypeStruct((b, s, d), F32)] + w_out,
        grid=(steps,),
        in_specs=[seq_spec] * 5 + [par_spec] * 5 + w_specs,
        out_specs=[seq_spec] + w_specs,
        scratch_shapes=[pltpu.VMEM((b * d // GROUP_W, GROUP_W, GROUP_W), F32)],
        compiler_params=pltpu.CompilerParams(
            dimension_semantics=("arbitrary",), vmem_limit_bytes=VMEM_LIMIT),
        name="wkv",
    )(r, k, v, lw, a, kk_p, ka_p, rk_p, lnw_p, lnb_p, *weights)
    return outs[0], outs[1:]


def _proj_mlp_kernel(has_gate, final_norm, with_rope, ff_chunk, sub, *refs):
    refs = list(refs)
    y_ref = refs.pop(0)
    gate_ref = refs.pop(0) if has_gate else None
    x_ref, wo_ref, bo_ref, gm_ref, win_ref, wout_ref = refs[:6]
    refs = refs[6:]
    gf_ref = refs.pop(0) if final_norm else None
    if with_rope:
        pos_ref, freq_ref = refs.pop(0), refs.pop(0)
    o_ref = refs.pop(0)
    if with_rope:
        cos_ref, sin_ref = refs.pop(0), refs.pop(0)

    def rope_tables(part, parts, pieces=2):
        rows = pos_ref.shape[0] // (parts * pieces)
        for n in range(part * pieces, (part + 1) * pieces):
            sl = slice(n * rows, (n + 1) * rows)
            ang = pos_ref[sl, :] * freq_ref[...]
            cos_ref[sl, :] = jnp.cos(ang)
            sin_ref[sl, :] = jnp.sin(ang)
            yield

    n_sub = x_ref.shape[0] // sub
    rows_of = lambda j: slice(j * sub, (j + 1) * sub)

    def head(j, out):
        rows = rows_of(j)
        y = y_ref[rows, :].astype(F32)
        if has_gate:
            y = y * gate_ref[rows, :]
        yield
        x1 = x_ref[rows, :] + _dot(y, wo_ref[...]) + bo_ref[...]
        yield
        out["x1"] = x1
        out["hb"] = _rms(x1, gm_ref[...]).astype(BF16)
        yield

    def mlp(j, ins, out):
        acc = ins["x1"]
        for c in range(D_FF // ff_chunk):
            cols = slice(c * ff_chunk, (c + 1) * ff_chunk)
            hid = jnp.maximum(jnp.dot(ins["hb"], win_ref[:, cols], preferred_element_type=F32), 0.0)
            yield
            acc = acc + _dot(hid * hid, wout_ref[cols, :])
            yield
        out["acc"] = acc

    def tail(j, acc):
        if final_norm:
            acc = _rms(acc, gf_ref[...])
        o_ref[rows_of(j), :] = acc
        yield

    ins = [dict() for _ in range(n_sub)]
    outs = [dict() for _ in range(n_sub)]
    _drain(head(0, ins[0]))
    for j in range(n_sub):
        fillers = []
        if j + 1 < n_sub:
            fillers.append(head(j + 1, ins[j + 1]))
        if j > 0:
            fillers.append(tail(j - 1, outs[j - 1]["acc"]))
        if with_rope:
            fillers.append(rope_tables(j, n_sub))
        _interleave(mlp(j, ins[j], outs[j]), fillers)
    _drain(tail(n_sub - 1, outs[n_sub - 1]["acc"]))


def _proj_mlp(y, gate, x, wo, bo, gm, win_all, wout_all, layer, gf, rope=None, tm=1024,
              sub=256, ff_chunk=1024):
    t, d = x.shape
    row_spec = pl.BlockSpec((tm, d), lambda i: (i, 0))
    args = [y] + ([gate] if gate is not None else []) + [x, wo, bo, gm, win_all, wout_all]
    resident = lambda shape, blk: pl.BlockSpec(shape, lambda i: (blk, 0),
                                               pipeline_mode=pl.Buffered(1))
    specs = [row_spec] * (len(args) - 5) + [
        resident(wo.shape, 0), resident(bo.shape, 0), resident(gm.shape, 0),
        resident((d, D_FF), layer), resident((D_FF, d), layer)]
    if gf is not None:
        args.append(gf)
        specs.append(_const_spec(gf.shape))
    out_shape = [jax.ShapeDtypeStruct((t, d), F32)]
    out_specs = [row_spec]
    if rope is not None:
        pos_packed, freq_row = rope
        packed_spec = pl.BlockSpec((tm // ROPE_PACK, LANES), lambda i: (i, 0))
        args += [pos_packed, freq_row]
        specs += [packed_spec, _const_spec(freq_row.shape)]
        out_shape += [jax.ShapeDtypeStruct(pos_packed.shape, F32)] * 2
        out_specs += [packed_spec] * 2
    outs = pl.pallas_call(
        functools.partial(_proj_mlp_kernel, gate is not None, gf is not None,
                          rope is not None, ff_chunk, sub),
        out_shape=out_shape,
        grid=(t // tm,),
        in_specs=specs,
        out_specs=out_specs,
        compiler_params=pltpu.CompilerParams(
            dimension_semantics=("parallel",), vmem_limit_bytes=VMEM_LIMIT),
        name="proj_mlp",
    )(*args)
    return outs[0] if rope is None else outs


def _pack_positions(pos):
    per = ROPE_SUB // ROPE_PACK
    grouped = pos.astype(F32).reshape(-1, ROPE_PACK, per).transpose(0, 2, 1)
    return jnp.repeat(grouped, LANES // ROPE_PACK, axis=-1).reshape(-1, LANES)


def _attn_qkv_kernel(sub, x_ref, g_ref, w_ref, b_ref, cos_ref, sin_ref, o_ref):
    n_sub = x_ref.shape[0] // sub
    rows_of = lambda j: slice(j * sub, (j + 1) * sub)
    lane = lax.broadcasted_iota(jnp.int32, (sub, LANES), 1)
    first_half = lane % HEAD < HEAD // 2
    n_q = N_HEADS * HEAD // LANES
    n_rot = (N_HEADS + N_KV_HEADS) * HEAD // LANES
    n_blk = 2 * LANES

    def norm(j, out):
        out["h"] = _rms(x_ref[rows_of(j), :], g_ref[...]).astype(BF16)
        yield

    def project(j, ins, out):
        out["qkv"] = []
        for c in range(QKV_DIM // n_blk):
            cols = slice(c * n_blk, (c + 1) * n_blk)
            out["qkv"].append(jnp.dot(ins["h"], w_ref[:, cols], preferred_element_type=F32)
                              + b_ref[:, cols])
            yield

    def unpack(packed):
        quarter = lax.shift_right_logical(
            lax.broadcasted_iota(jnp.int32, packed.shape, 1), (LANES // ROPE_PACK).bit_length() - 1)
        rolled = [packed] + [pltpu.roll(packed, n * (LANES // ROPE_PACK), axis=1)
                             for n in range(1, ROPE_PACK)]
        blocks = []
        for n in range(ROPE_PACK):
            blk = rolled[(ROPE_PACK - 1 - n) % ROPE_PACK]
            for qq in range(ROPE_PACK - 2, -1, -1):
                blk = jnp.where(quarter == qq, rolled[(qq - n) % ROPE_PACK], blk)
            blocks.append(blk)
        return jnp.concatenate(blocks, axis=0)

    def rotary(j, qkv):
        rows = rows_of(j)
        packed_rows = slice(j * sub // ROPE_PACK, (j + 1) * sub // ROPE_PACK)
        cos = unpack(cos_ref[packed_rows, :])
        sin = jnp.where(first_half, -1.0, 1.0) * unpack(sin_ref[packed_rows, :])
        yield
        for s in range(QKV_DIM // LANES):
            c, half = divmod(s, n_blk // LANES)
            blk = qkv[c][:, half * LANES:(half + 1) * LANES]
            if s < n_rot:
                rot = jnp.where(first_half, pltpu.roll(blk, LANES - HEAD // 2, axis=1),
                                pltpu.roll(blk, HEAD // 2, axis=1))
                blk = blk * cos + rot * sin
            if s < n_q:
                blk = blk * Q_SCALE
            o_ref[rows, s * LANES:(s + 1) * LANES] = blk.astype(o_ref.dtype)
            if s % 2 == 1:
                yield

    ins = [dict() for _ in range(n_sub)]
    outs = [dict() for _ in range(n_sub)]
    _drain(norm(0, ins[0]))
    for j in range(n_sub):
        fillers = []
        if j + 1 < n_sub:
            fillers.append(norm(j + 1, ins[j + 1]))
        if j > 0:
            fillers.append(rotary(j - 1, outs[j - 1]["qkv"]))
        _interleave(project(j, ins[j], outs[j]), fillers)
    _drain(rotary(n_sub - 1, outs[n_sub - 1]["qkv"]))


def _attn_qkv(x, g, w, b, cos, sin, tm=1024, sub=ROPE_SUB):
    t, d = x.shape
    row = lambda width: pl.BlockSpec((tm, width), lambda i: (i, 0))
    packed = pl.BlockSpec((tm // ROPE_PACK, LANES), lambda i: (i, 0))
    return pl.pallas_call(
        functools.partial(_attn_qkv_kernel, sub),
        out_shape=jax.ShapeDtypeStruct((t, QKV_DIM), BF16),
        grid=(t // tm,),
        in_specs=[row(d), _const_spec(g.shape), _const_spec(w.shape), _const_spec(b.shape),
                  packed, packed],
        out_specs=row(QKV_DIM),
        compiler_params=pltpu.CompilerParams(
            dimension_semantics=("parallel",), vmem_limit_bytes=VMEM_LIMIT),
        name="attn_qkv",
    )(x, g, w, b, cos, sin)


def _attn_core_kernel(q_blocks, q_ref, kp_ref, kc_ref, vp_ref, vc_ref, sink_ref, o_ref):
    n = pl.program_id(1)
    kr = lax.broadcasted_iota(jnp.int32, (2 * WINDOW, WINDOW), 0)
    qc = lax.broadcasted_iota(jnp.int32, (2 * WINDOW, WINDOW), 1)
    own = kr >= WINDOW
    band = (own & (kr - WINDOW <= qc)) | (jnp.logical_not(own) & (kr > qc))
    first_band = band & (own | (n > 0))
    neg = -jnp.inf
    probs = [(i, j) for i in range(q_blocks) for j in range(N_KV_HEADS)]
    blk = lambda i: slice(i * WINDOW, (i + 1) * WINDOW)
    kvl = lambda j: slice(j * HEAD, (j + 1) * HEAD)
    head = lambda j, g: j * GROUP + g

    def keys(prev_ref, cur_ref, i, j):
        prev = prev_ref[0, :, kvl(j)] if i == 0 else cur_ref[0, blk(i - 1), kvl(j)]
        return jnp.concatenate([prev, cur_ref[0, blk(i), kvl(j)]], axis=0)

    q = [jnp.concatenate([q_ref[0, blk(i), head(j, g) * HEAD:(head(j, g) + 1) * HEAD]
                          for g in range(GROUP)], axis=0) for i, j in probs]
    s_t = [_dot_nt(keys(kp_ref, kc_ref, i, j), q[p]) for p, (i, j) in enumerate(probs)]
    p_t = [[] for _ in probs]
    for g in range(GROUP):
        sink = [sink_ref[:, head(j, g):head(j, g) + 1] * LOG2E for _, j in probs]
        x = [jnp.where(first_band if i == 0 else band, s_t[p][:, blk(g)], neg)
             for p, (i, j) in enumerate(probs)]
        m = [jnp.maximum(jnp.max(x[p], axis=0, keepdims=True), sink[p])
             for p in range(len(probs))]
        e = [jnp.exp2(x[p] - m[p]) for p in range(len(probs))]
        den = [jnp.sum(e[p], axis=0, keepdims=True) + jnp.exp2(sink[p] - m[p])
               for p in range(len(probs))]
        for p in range(len(probs)):
            p_t[p].append((e[p] * (1.0 / den[p])).astype(BF16))
    for p, (i, j) in enumerate(probs):
        o = lax.dot_general(jnp.concatenate(p_t[p], axis=1), keys(vp_ref, vc_ref, i, j),
                            (((0,), (0,)), ((), ())), preferred_element_type=F32)
        for g in range(GROUP):
            o_ref[0, blk(i), head(j, g) * HEAD:(head(j, g) + 1) * HEAD] = (
                o[blk(g)].astype(o_ref.dtype))


def _attn_core(qkv, sinks, q_blocks=8):
    b, s, _ = qkv.shape
    nq = N_HEADS * HEAD
    tq = q_blocks * WINDOW
    k_blk = nq // LANES
    v_blk = k_blk + N_KV_HEADS * HEAD // LANES
    q_spec = pl.BlockSpec((1, tq, nq), lambda bi, n: (bi, n, 0))
    cur = lambda blk: pl.BlockSpec((1, tq, LANES), lambda bi, n: (bi, n, blk))
    prev = lambda blk: pl.BlockSpec((1, WINDOW, LANES),
                                    lambda bi, n: (bi, jnp.maximum(n * q_blocks - 1, 0), blk))
    return pl.pallas_call(
        functools.partial(_attn_core_kernel, q_blocks),
        out_shape=jax.ShapeDtypeStruct((b, s, nq), BF16),
        grid=(b, s // tq),
        in_specs=[q_spec, prev(k_blk), cur(k_blk), prev(v_blk), cur(v_blk),
                  _const_spec(sinks.shape)],
        out_specs=q_spec,
        compiler_params=pltpu.CompilerParams(
            dimension_semantics=("parallel", "arbitrary"), vmem_limit_bytes=VMEM_LIMIT),
        name="attn_core",
    )(qkv, qkv, qkv, qkv, qkv, sinks)


def kernel(x, positions, norm_mix_g, norm_mlp_g, norm_final_g, rwkv_mu, rwkv_w_r, rwkv_w_k, rwkv_w_v, rwkv_w_o, rwkv_w0, rwkv_w1, rwkv_w2, rwkv_a0, rwkv_a1, rwkv_a2, rwkv_g1, rwkv_g2, rwkv_k_k, rwkv_k_a, rwkv_r_k, rwkv_ln_w, rwkv_ln_b, attn_w_qkv, attn_b_qkv, attn_sinks, attn_w_o, attn_b_o, mlp_w_in, mlp_w_out):
    b, s, d = x.shape
    t = b * s
    x2d = x.reshape(t, d)
    row = lambda p: p.reshape(1, -1)
    bf = lambda w: w.astype(BF16)
    zero_bias = jnp.zeros((1, d), F32)

    r, k, v, lw, a, gate = _rwkv_pre(
        x2d, s, row(norm_mix_g[0]), rwkv_mu[0], bf(rwkv_w_r[0]), bf(rwkv_w_k[0]),
        bf(rwkv_w_v[0]), row(rwkv_w0[0]), bf(rwkv_w1[0]), bf(rwkv_w2[0]), row(rwkv_a0[0]),
        bf(rwkv_a1[0]), bf(rwkv_a2[0]), bf(rwkv_g1[0]), bf(rwkv_g2[0]))
    seq = lambda z: z.reshape(b, s, d)
    later_weights = [rwkv_w_o[0], attn_w_qkv[0], attn_w_o[0],
                     mlp_w_in.reshape(-1, D_FF), mlp_w_out.reshape(-1, d)]
    y, (w_o0, w_qkv, w_o1, w_in_all, w_out_all) = _wkv(
        seq(r), seq(k), seq(v), seq(lw), seq(a), row(rwkv_k_k[0]), row(rwkv_k_a[0]),
        row(rwkv_r_k[0]), row(rwkv_ln_w[0]), row(rwkv_ln_b[0]), later_weights)
    inv_freq = ROPE_THETA ** (-jnp.arange(0, HEAD, 2, dtype=F32) / HEAD)
    freq_row = jnp.tile(inv_freq, LANES // (HEAD // 2)).reshape(1, LANES)
    x2d, cos, sin = _proj_mlp(y.reshape(t, d), gate, x2d, w_o0, zero_bias,
                              row(norm_mlp_g[0]), w_in_all, w_out_all, 0, None,
                              rope=(_pack_positions(positions.reshape(t)), freq_row))

    qkv = _attn_qkv(x2d, row(norm_mix_g[1]), w_qkv, row(attn_b_qkv[0]), cos, sin)
    o = _attn_core(qkv.reshape(b, s, QKV_DIM), row(attn_sinks[0]))
    out = _proj_mlp(o.reshape(t, d), None, x2d, w_o1, row(attn_b_o[0]),
                    row(norm_mlp_g[1]), w_in_all, w_out_all, 1, row(norm_final_g))
    return out.reshape(b, s, d)
```

```python
import functools
import math

import jax
import jax.numpy as jnp
from jax import lax
from jax.experimental import pallas as pl
from jax.experimental.pallas import tpu as pltpu

F32 = jnp.float32
BF16 = jnp.bfloat16

D_MODEL = 1024
HEAD = 64
N_HEADS = D_MODEL // HEAD
N_KV_HEADS = 2
GROUP = N_HEADS // N_KV_HEADS
WINDOW = 128
QKV_DIM = (N_HEADS + 2 * N_KV_HEADS) * HEAD
D_FF = 4 * D_MODEL
ROPE_THETA = 10000.0
RMS_EPS = 1e-5
GN_EPS = 64e-5

LANES = 128
ROPE_PACK = LANES // (HEAD // 2)
ROPE_SUB = 256
LOG2E = math.log2(math.e)
Q_SCALE = HEAD ** -0.5 * LOG2E
CHUNK = 64
VMEM_LIMIT = 56 * 1024 * 1024


def _rms(x, g):
    return x * lax.rsqrt(jnp.mean(x * x, axis=-1, keepdims=True) + RMS_EPS) * g


def _dot(a, b):
    return jnp.dot(a.astype(BF16), b.astype(BF16), preferred_element_type=F32)


def _dot_nt(a, b):
    return lax.dot_general(a.astype(BF16), b.astype(BF16), (((1,), (1,)), ((), ())),
                           preferred_element_type=F32)


def _dot_tn(a, b):
    return lax.dot_general(a.astype(BF16), b.astype(BF16), (((0,), (0,)), ((), ())),
                           preferred_element_type=F32)


def _const_spec(shape):
    nd = len(shape)
    return pl.BlockSpec(shape, lambda *_: (0,) * nd)


def _drain(gen):
    for _ in gen:
        pass


def _interleave(main, fillers):
    fillers = [f if isinstance(f, tuple) else (f, 0, 1) for f in fillers]
    for n, _ in enumerate(main):
        for f, start, stride in fillers:
            if n >= start and (n - start) % stride == 0:
                next(f, None)
    for f, _, _ in fillers:
        _drain(f)


def _rwkv_pre_kernel(seq_tiles, sub, x_ref, xp_ref, g_ref, mu_ref, wr_ref, wk_ref, wv_ref,
                     w0_ref, w1_ref, w2_ref, a0_ref, a1_ref, a2_ref, g1_ref, g2_ref,
                     r_out, k_out, v_out, lw_out, a_out, g_out):
    i = pl.program_id(0)
    n_sub = x_ref.shape[0] // sub
    g = g_ref[...]
    mu = mu_ref[...]
    rows_of = lambda j: slice(j * sub, (j + 1) * sub)
    hp = _rms(xp_ref[...], g)
    carry_row = {0: jnp.where(i % seq_tiles == 0, 0.0, hp[7:8, :])}

    def mix(j, out):
        h = _rms(x_ref[rows_of(j), :], g)
        carry_row[j + 1] = h[sub - 1:sub, :]
        yield
        row = lax.broadcasted_iota(jnp.int32, h.shape, 0)
        dx = jnp.where(row == 0, carry_row[j], pltpu.roll(h, 1, axis=0)) - h
        yield
        for n, name in enumerate(("r", "w", "k", "v", "a", "g")):
            out[name] = (h + dx * mu[n:n + 1]).astype(BF16)
            if n % 2 == 1:
                yield

    def project(j, xs):
        rows = rows_of(j)
        dw = _dot(xs["w"], w1_ref[...])
        da = _dot(xs["a"], a1_ref[...])
        dg = _dot(xs["g"], g1_ref[...])
        yield
        r_out[rows, :] = jnp.dot(xs["r"], wr_ref[...], preferred_element_type=F32)
        yield
        w_pre = w0_ref[...] + _dot(jnp.tanh(dw), w2_ref[...])
        a_pre = a0_ref[...] + _dot(da, a2_ref[...])
        g_out[rows, :] = _dot(jax.nn.sigmoid(dg), g2_ref[...])
        yield
        k_out[rows, :] = jnp.dot(xs["k"], wk_ref[...], preferred_element_type=F32)
        yield
        lw_out[rows, :] = jax.nn.sigmoid(w_pre) * (-math.exp(-0.5))
        a_out[rows, :] = jax.nn.sigmoid(a_pre)
        yield
        v_out[rows, :] = jnp.dot(xs["v"], wv_ref[...], preferred_element_type=F32)
        yield

    xs = [dict() for _ in range(n_sub)]
    _drain(mix(0, xs[0]))
    for j in range(n_sub):
        fillers = [mix(j + 1, xs[j + 1])] if j + 1 < n_sub else []
        _interleave(project(j, xs[j]), fillers)


def _rwkv_pre(x2d, seq_len, g, mu, wr, wk, wv, w0, w1, w2, a0, a1, a2, g1, g2, tm=512, sub=256):
    t, d = x2d.shape
    seq_tiles = seq_len // tm
    row_spec = pl.BlockSpec((tm, d), lambda i: (i, 0))
    prev_spec = pl.BlockSpec((8, d), lambda i: (jnp.maximum(i * (tm // 8) - 1, 0), 0))
    consts = (g, mu, wr, wk, wv, w0, w1, w2, a0, a1, a2, g1, g2)
    out = jax.ShapeDtypeStruct((t, d), F32)
    return pl.pallas_call(
        functools.partial(_rwkv_pre_kernel, seq_tiles, sub),
        out_shape=(out,) * 6,
        grid=(t // tm,),
        in_specs=[row_spec, prev_spec] + [_const_spec(c.shape) for c in consts],
        out_specs=(row_spec,) * 6,
        compiler_params=pltpu.CompilerParams(
            dimension_semantics=("parallel",), vmem_limit_bytes=VMEM_LIMIT),
        name="rwkv_pre",
    )(x2d, x2d, *consts)


def _cumsum_rows(x):
    n = x.shape[0]
    row = lax.broadcasted_iota(jnp.int32, x.shape, 0)
    s = 1
    while s < n:
        x = x + jnp.where(row >= s, pltpu.roll(x, s, axis=0), 0.0)
        s *= 2
    return x


INV_BASE = 8
GROUP_HEADS = 4
GROUP_W = GROUP_HEADS * HEAD


def _head_sum(x):
    low = lax.broadcasted_iota(jnp.int32, (x.shape[0], LANES), 1) < HEAD
    outs = []
    for c in range(x.shape[1] // LANES):
        xc = x[:, c * LANES:(c + 1) * LANES]
        s_lo = jnp.sum(jnp.where(low, xc, 0.0), axis=-1, keepdims=True)
        s_hi = jnp.sum(jnp.where(low, 0.0, xc), axis=-1, keepdims=True)
        outs.append(jnp.where(low, s_lo, s_hi))
    return jnp.concatenate(outs, axis=1)


def _wkv_kernel(chunks, n_w, r_ref, k_ref, v_ref, lw_ref, a_ref,
                kk_ref, ka_ref, rk_ref, lnw_ref, lnb_ref, *rest):
    o_ref, state_ref = rest[n_w], rest[-1]
    w_refs = rest[:n_w] + rest[n_w + 1:-1]
    n_batch = r_ref.shape[0]
    n_groups = D_MODEL // GROUP_W
    probs = [(b, g) for b in range(n_batch) for g in range(n_groups)]
    ps = range(len(probs))

    @pl.when(pl.program_id(0) == 0)
    def _():
        state_ref[...] = jnp.zeros_like(state_ref)

    ri = lax.broadcasted_iota(jnp.int32, (CHUNK, GROUP_W), 0)
    lane = lax.broadcasted_iota(jnp.int32, (CHUNK, GROUP_W), 1)
    cj = lane & (HEAD - 1)
    lane_head = lax.shift_right_logical(lane, HEAD.bit_length() - 1)
    head_masks = [lane_head == h for h in range(GROUP_HEADS)]
    blk = lambda idx, size: lax.shift_right_logical(idx, size.bit_length() - 1)
    strict = cj < ri
    incl = cj <= ri
    eye = (cj == ri).astype(F32)
    diag = blk(ri, INV_BASE) == blk(cj, INV_BASE)
    bands = []
    size = INV_BASE
    while size < CHUNK:
        bands.append((blk(ri, 2 * size) == blk(cj, 2 * size))
                     & (blk(ri, size) != blk(cj, size)))
        size *= 2
    sr = lax.broadcasted_iota(jnp.int32, (GROUP_W, GROUP_W), 0)
    sc = lax.broadcasted_iota(jnp.int32, (GROUP_W, GROUP_W), 1)
    state_mask = blk(sr, HEAD) == blk(sc, HEAD)

    def bd(y):
        return jnp.concatenate([jnp.where(m, y, 0.0) for m in head_masks], axis=0).astype(BF16)

    def mm(x, w):
        return jnp.dot(x.astype(BF16), w, preferred_element_type=F32)

    def mm_nt(x, w):
        return lax.dot_general(x.astype(BF16), w, (((1,), (1,)), ((), ())),
                               preferred_element_type=F32)

    cols = [slice(g * GROUP_W, (g + 1) * GROUP_W) for g in range(n_groups)]
    rows_of = lambda c: slice(c * CHUNK, (c + 1) * CHUNK)


    def exact_zero(x):
        bits = pltpu.bitcast(x[:8, :LANES], jnp.uint32)
        half = jnp.uint32(16)
        zero = lax.shift_right_logical(lax.shift_right_logical(bits, half), half)
        return pltpu.bitcast(zero, F32)[:1, :1]

    def prepare(c, b, out, after=None):
        rows = rows_of(c)
        lw = lw_ref[b, rows, :]
        if after is not None:
            lw = lw + exact_zero(after())
        linc = _cumsum_rows(lw)
        yield
        p_inc = jnp.exp(linc)
        p_exc = jnp.exp(linc - lw)
        p_inv = jnp.exp(-linc)
        yield
        k = k_ref[b, rows, :]
        a = a_ref[b, rows, :]
        kk = k * kk_ref[...]
        kk = kk * lax.rsqrt(jnp.maximum(_head_sum(kk * kk), 1e-24))
        yield
        kmod = k * (1.0 + (a - 1.0) * ka_ref[...])
        at_f = (-kk * p_exc).astype(BF16)
        rt_f = (r_ref[b, rows, :] * p_inc).astype(BF16)
        yield
        out["lhs"] = [jnp.concatenate([at_f[:, cl], rt_f[:, cl]], axis=0) for cl in cols]
        out["bt"] = kk * a * p_inv
        out["kt"] = kmod * p_inv
        out["kmod"] = kmod
        out["p_end"] = p_inc[CHUNK - 1:CHUNK, :]
        yield

    def matmuls(c, ops, out):
        v = [v_ref[b, rows_of(c), :] for b in range(n_batch)]
        lhs = [ops[b]["lhs"][g] for b, g in probs]
        bt = [ops[b]["bt"][:, cols[g]] for b, g in probs]
        kt = [ops[b]["kt"][:, cols[g]] for b, g in probs]
        vg = [v[b][:, cols[g]] for b, g in probs]
        ab = [mm_nt(lhs[p], bd(bt[p])) for p in ps]
        ak = [mm_nt(lhs[p], bd(kt[p])) for p in ps]
        yield
        a_ab = [jnp.where(strict, x[:CHUNK], 0.0) for x in ab]
        a_rb = [jnp.where(incl, x[CHUNK:], 0.0) for x in ab]
        a_k = [jnp.concatenate([jnp.where(strict, x[:CHUNK], 0.0),
                                jnp.where(incl, x[CHUNK:], 0.0)], axis=0) for x in ak]
        d1 = [jnp.where(diag, x, 0.0) for x in a_ab]
        d2 = [mm(d1[p], bd(d1[p])) for p in ps]
        yield
        t = [eye + x for x in d1]
        td = [mm(jnp.concatenate([t[p], d2[p]], axis=0), bd(d2[p])) for p in ps]
        yield
        t = [t[p] + td[p][:CHUNK] for p in ps]
        out["anchors"] = [td[0]]
        t = [t[p] + mm(t[p], bd(td[p][CHUNK:])) for p in ps]
        yield
        for band in bands:
            te = [mm(t[p], bd(jnp.where(band, a_ab[p], 0.0))) for p in ps]
            out["anchors"].append(te[0])
            yield
            t = [t[p] + mm(te[p], bd(t[p])) for p in ps]
            yield
        s0 = [state_ref[p] for p in ps]
        zy = [mm_nt(lhs[p], s0[p].astype(BF16)) for p in ps]
        av = [mm(a_k[p], bd(vg[p])) for p in ps]
        yield
        u = [mm(t[p], bd(zy[p][:CHUNK] + av[p][:CHUNK])) for p in ps]
        yield
        y = [zy[p][CHUNK:] + av[p][CHUNK:] + mm(a_rb[p], bd(u[p])) for p in ps]
        for p, (b, g) in enumerate(probs):
            uv = jnp.concatenate([u[p], vg[p]], axis=0)
            bk = jnp.concatenate([bt[p], kt[p]], axis=0)
            state_ref[p] = (jnp.where(state_mask, s0[p] + _dot_tn(uv, bk), 0.0)
                            * ops[b]["p_end"][:, cols[g]])
        out["y"] = [jnp.concatenate(y[b * n_groups:(b + 1) * n_groups], axis=1)
                    for b in range(n_batch)]
        yield

    def finish(c, b, ops, y_f, after=None):
        rows = rows_of(c)
        inv_n = 1.0 / HEAD
        if after is not None:
            y_f = y_f + exact_zero(after())
        yc = y_f - _head_sum(y_f) * inv_n
        yield
        var = _head_sum(yc * yc) * inv_n
        yn = yc * lax.rsqrt(var + GN_EPS) * lnw_ref[...] + lnb_ref[...]
        yield
        bonus = _head_sum(r_ref[b, rows, :] * ops["kmod"] * rk_ref[...])
        o_ref[b, rows, :] = yn + bonus * v_ref[b, rows, :]
        yield

    def cast_weights():
        for src, dst in zip(w_refs[:len(w_refs) // 2], w_refs[len(w_refs) // 2:]):
            dst[...] = src[...].astype(dst.dtype)
            yield

    bs = range(n_batch)
    ops = [[dict() for _ in bs] for _ in range(chunks)]
    res = [dict() for _ in range(chunks)]
    for b in bs:
        _drain(prepare(0, b, ops[0][b]))
    for c in range(chunks):
        fillers = [cast_weights()] if c == 0 else []
        tie = lambda n, c=c: (lambda: res[c]["anchors"][n])
        if c + 1 < chunks:
            fillers += [(prepare(c + 1, b, ops[c + 1][b], tie(b)), 3 + b, 1) for b in bs]
        if c > 0:
            fillers += [(finish(c - 1, b, ops[c - 1][b], res[c - 1]["y"][b], tie(3)), 8, 1)
                        for b in bs]
        _interleave(matmuls(c, ops[c], res[c]), fillers)
    for b in bs:
        _drain(finish(chunks - 1, b, ops[chunks - 1][b], res[chunks - 1]["y"][b]))


def _wkv(r, k, v, lw, a, kk_p, ka_p, rk_p, lnw_p, lnb_p, weights, rows=256):
    b, s, d = r.shape
    steps = s // rows
    seq_spec = pl.BlockSpec((b, rows, d), lambda ci: (0, ci, 0))
    par_spec = pl.BlockSpec((1, d), lambda ci: (0, 0))
    w_specs = [pl.BlockSpec((w.shape[0] // steps, w.shape[1]), lambda ci: (ci, 0))
               for w in weights]
    w_out = [jax.ShapeDtypeStruct(w.shape, BF16) for w in weights]
    outs = pl.pallas_call(
        functools.partial(_wkv_kernel, rows // CHUNK, len(weights)),
        out_shape=[jax.ShapeDtypeStruct((b, s, d), F32)] + w_out,
        grid=(steps,),
        in_specs=[seq_spec] * 5 + [par_spec] * 5 + w_specs,
        out_specs=[seq_spec] + w_specs,
        scratch_shapes=[pltpu.VMEM((b * d // GROUP_W, GROUP_W, GROUP_W), F32)],
        compiler_params=pltpu.CompilerParams(
            dimension_semantics=("arbitrary",), vmem_limit_bytes=VMEM_LIMIT),
        name="wkv",
    )(r, k, v, lw, a, kk_p, ka_p, rk_p, lnw_p, lnb_p, *weights)
    return outs[0], outs[1:]


def _proj_mlp_kernel(has_gate, final_norm, with_rope, ff_chunk, sub, *refs):
    refs = list(refs)
    y_ref = refs.pop(0)
    gate_ref = refs.pop(0) if has_gate else None
    x_ref, wo_ref, bo_ref, gm_ref, win_ref, wout_ref = refs[:6]
    refs = refs[6:]
    gf_ref = refs.pop(0) if final_norm else None
    if with_rope:
        pos_ref, freq_ref = refs.pop(0), refs.pop(0)
    o_ref = refs.pop(0)
    if with_rope:
        cos_ref, sin_ref = refs.pop(0), refs.pop(0)

    def rope_tables(part, parts, pieces=2):
        rows = pos_ref.shape[0] // (parts * pieces)
        for n in range(part * pieces, (part + 1) * pieces):
            sl = slice(n * rows, (n + 1) * rows)
            ang = pos_ref[sl, :] * freq_ref[...]
            cos_ref[sl, :] = jnp.cos(ang)
            sin_ref[sl, :] = jnp.sin(ang)
            yield

    n_sub = x_ref.shape[0] // sub
    rows_of = lambda j: slice(j * sub, (j + 1) * sub)

    def head(j, out):
        rows = rows_of(j)
        y = y_ref[rows, :].astype(F32)
        if has_gate:
            y = y * gate_ref[rows, :]
        yield
        x1 = x_ref[rows, :] + _dot(y, wo_ref[...]) + bo_ref[...]
        yield
        out["x1"] = x1
        out["hb"] = _rms(x1, gm_ref[...]).astype(BF16)
        yield

    def mlp(j, ins, out):
        acc = ins["x1"]
        for c in range(D_FF // ff_chunk):
            cols = slice(c * ff_chunk, (c + 1) * ff_chunk)
            hid = jnp.maximum(jnp.dot(ins["hb"], win_ref[:, cols], preferred_element_type=F32), 0.0)
            yield
            acc = acc + _dot(hid * hid, wout_ref[cols, :])
            yield
        out["acc"] = acc

    def tail(j, acc):
        if final_norm:
            acc = _rms(acc, gf_ref[...])
        o_ref[rows_of(j), :] = acc
        yield

    ins = [dict() for _ in range(n_sub)]
    outs = [dict() for _ in range(n_sub)]
    _drain(head(0, ins[0]))
    for j in range(n_sub):
        fillers = []
        if j + 1 < n_sub:
            fillers.append(head(j + 1, ins[j + 1]))
        if j > 0:
            fillers.append(tail(j - 1, outs[j - 1]["acc"]))
        if with_rope:
            fillers.append(rope_tables(j, n_sub))
        _interleave(mlp(j, ins[j], outs[j]), fillers)
    _drain(tail(n_sub - 1, outs[n_sub - 1]["acc"]))


def _proj_mlp(y, gate, x, wo, bo, gm, win_all, wout_all, layer, gf, rope=None, tm=1024,
              sub=256, ff_chunk=1024):
    t, d = x.shape
    row_spec = pl.BlockSpec((tm, d), lambda i: (i, 0))
    args = [y] + ([gate] if gate is not None else []) + [x, wo, bo, gm, win_all, wout_all]
    resident = lambda shape, blk: pl.BlockSpec(shape, lambda i: (blk, 0),
                                               pipeline_mode=pl.Buffered(1))
    specs = [row_spec] * (len(args) - 5) + [
        resident(wo.shape, 0), resident(bo.shape, 0), resident(gm.shape, 0),
        resident((d, D_FF), layer), resident((D_FF, d), layer)]
    if gf is not None:
        args.append(gf)
        specs.append(_const_spec(gf.shape))
    out_shape = [jax.ShapeDtypeStruct((t, d), F32)]
    out_specs = [row_spec]
    if rope is not None:
        pos_packed, freq_row = rope
        packed_spec = pl.BlockSpec((tm // ROPE_PACK, LANES), lambda i: (i, 0))
        args += [pos_packed, freq_row]
        specs += [packed_spec, _const_spec(freq_row.shape)]
        out_shape += [jax.ShapeDtypeStruct(pos_packed.shape, F32)] * 2
        out_specs += [packed_spec] * 2
    outs = pl.pallas_call(
        functools.partial(_proj_mlp_kernel, gate is not None, gf is not None,
                          rope is not None, ff_chunk, sub),
        out_shape=out_shape,
        grid=(t // tm,),
        in_specs=specs,
        out_specs=out_specs,
        compiler_params=pltpu.CompilerParams(
            dimension_semantics=("parallel",), vmem_limit_bytes=VMEM_LIMIT),
        name="proj_mlp",
    )(*args)
    return outs[0] if rope is None else outs


def _pack_positions(pos):
    per = ROPE_SUB // ROPE_PACK
    grouped = pos.astype(F32).reshape(-1, ROPE_PACK, per).transpose(0, 2, 1)
    return jnp.repeat(grouped, LANES // ROPE_PACK, axis=-1).reshape(-1, LANES)


def _attn_qkv_kernel(sub, x_ref, g_ref, w_ref, b_ref, cos_ref, sin_ref, o_ref):
    n_sub = x_ref.shape[0] // sub
    rows_of = lambda j: slice(j * sub, (j + 1) * sub)
    lane = lax.broadcasted_iota(jnp.int32, (sub, LANES), 1)
    first_half = lane % HEAD < HEAD // 2
    n_q = N_HEADS * HEAD // LANES
    n_rot = (N_HEADS + N_KV_HEADS) * HEAD // LANES
    n_blk = 2 * LANES

    def norm(j, out):
        out["h"] = _rms(x_ref[rows_of(j), :], g_ref[...]).astype(BF16)
        yield

    def project(j, ins, out):
        out["qkv"] = []
        for c in range(QKV_DIM // n_blk):
            cols = slice(c * n_blk, (c + 1) * n_blk)
            out["qkv"].append(jnp.dot(ins["h"], w_ref[:, cols], preferred_element_type=F32)
                              + b_ref[:, cols])
            yield

    def unpack(packed):
        quarter = lax.shift_right_logical(
            lax.broadcasted_iota(jnp.int32, packed.shape, 1), (LANES // ROPE_PACK).bit_length() - 1)
        rolled = [packed] + [pltpu.roll(packed, n * (LANES // ROPE_PACK), axis=1)
                             for n in range(1, ROPE_PACK)]
        blocks = []
        for n in range(ROPE_PACK):
            blk = rolled[(ROPE_PACK - 1 - n) % ROPE_PACK]
            for qq in range(ROPE_PACK - 2, -1, -1):
                blk = jnp.where(quarter == qq, rolled[(qq - n) % ROPE_PACK], blk)
            blocks.append(blk)
        return jnp.concatenate(blocks, axis=0)

    def rotary(j, qkv):
        rows = rows_of(j)
        packed_rows = slice(j * sub // ROPE_PACK, (j + 1) * sub // ROPE_PACK)
        cos = unpack(cos_ref[packed_rows, :])
        sin = jnp.where(first_half, -1.0, 1.0) * unpack(sin_ref[packed_rows, :])
        yield
        for s in range(QKV_DIM // LANES):
            c, half = divmod(s, n_blk // LANES)
            blk = qkv[c][:, half * LANES:(half + 1) * LANES]
            if s < n_rot:
                rot = jnp.where(first_half, pltpu.roll(blk, LANES - HEAD // 2, axis=1),
                                pltpu.roll(blk, HEAD // 2, axis=1))
                blk = blk * cos + rot * sin
            if s < n_q:
                blk = blk * Q_SCALE
            o_ref[rows, s * LANES:(s + 1) * LANES] = blk.astype(o_ref.dtype)
            if s % 2 == 1:
                yield

    ins = [dict() for _ in range(n_sub)]
    outs = [dict() for _ in range(n_sub)]
    _drain(norm(0, ins[0]))
    for j in range(n_sub):
        fillers = []
        if j + 1 < n_sub:
            fillers.append(norm(j + 1, ins[j + 1]))
        if j > 0:
            fillers.append(rotary(j - 1, outs[j - 1]["qkv"]))
        _interleave(project(j, ins[j], outs[j]), fillers)
    _drain(rotary(n_sub - 1, outs[n_sub - 1]["qkv"]))


def _attn_qkv(x, g, w, b, cos, sin, tm=1024, sub=ROPE_SUB):
    t, d = x.shape
    row = lambda width: pl.BlockSpec((tm, width), lambda i: (i, 0))
    packed = pl.BlockSpec((tm // ROPE_PACK, LANES), lambda i: (i, 0))
    return pl.pallas_call(
        functools.partial(_attn_qkv_kernel, sub),
        out_shape=jax.ShapeDtypeStruct((t, QKV_DIM), BF16),
        grid=(t // tm,),
        in_specs=[row(d), _const_spec(g.shape), _const_spec(w.shape), _const_spec(b.shape),
                  packed, packed],
        out_specs=row(QKV_DIM),
        compiler_params=pltpu.CompilerParams(
            dimension_semantics=("parallel",), vmem_limit_bytes=VMEM_LIMIT),
        name="attn_qkv",
    )(x, g, w, b, cos, sin)


def _attn_core_kernel(q_blocks, q_ref, kp_ref, kc_ref, vp_ref, vc_ref, sink_ref, o_ref):
    n = pl.program_id(1)
    kr = lax.broadcasted_iota(jnp.int32, (2 * WINDOW, WINDOW), 0)
    qc = lax.broadcasted_iota(jnp.int32, (2 * WINDOW, WINDOW), 1)
    own = kr >= WINDOW
    band = (own & (kr - WINDOW <= qc)) | (jnp.logical_not(own) & (kr > qc))
    first_band = band & (own | (n > 0))
    neg = -jnp.inf
    probs = [(i, j) for i in range(q_blocks) for j in range(N_KV_HEADS)]
    blk = lambda i: slice(i * WINDOW, (i + 1) * WINDOW)
    kvl = lambda j: slice(j * HEAD, (j + 1) * HEAD)
    head = lambda j, g: j * GROUP + g

    def keys(prev_ref, cur_ref, i, j):
        prev = prev_ref[0, :, kvl(j)] if i == 0 else cur_ref[0, blk(i - 1), kvl(j)]
        return jnp.concatenate([prev, cur_ref[0, blk(i), kvl(j)]], axis=0)

    q = [jnp.concatenate([q_ref[0, blk(i), head(j, g) * HEAD:(head(j, g) + 1) * HEAD]
                          for g in range(GROUP)], axis=0) for i, j in probs]
    s_t = [_dot_nt(keys(kp_ref, kc_ref, i, j), q[p]) for p, (i, j) in enumerate(probs)]
    p_t = [[] for _ in probs]
    for g in range(GROUP):
        sink = [sink_ref[:, head(j, g):head(j, g) + 1] * LOG2E for _, j in probs]
        x = [jnp.where(first_band if i == 0 else band, s_t[p][:, blk(g)], neg)
             for p, (i, j) in enumerate(probs)]
        m = [jnp.maximum(jnp.max(x[p], axis=0, keepdims=True), sink[p])
             for p in range(len(probs))]
        e = [jnp.exp2(x[p] - m[p]) for p in range(len(probs))]
        den = [jnp.sum(e[p], axis=0, keepdims=True) + jnp.exp2(sink[p] - m[p])
               for p in range(len(probs))]
        for p in range(len(probs)):
            p_t[p].append((e[p] * (1.0 / den[p])).astype(BF16))
    for p, (i, j) in enumerate(probs):
        o = lax.dot_general(jnp.concatenate(p_t[p], axis=1), keys(vp_ref, vc_ref, i, j),
                            (((0,), (0,)), ((), ())), preferred_element_type=F32)
        for g in range(GROUP):
            o_ref[0, blk(i), head(j, g) * HEAD:(head(j, g) + 1) * HEAD] = (
                o[blk(g)].astype(o_ref.dtype))


def _attn_core(qkv, sinks, q_blocks=8):
    b, s, _ = qkv.shape
    nq = N_HEADS * HEAD
    tq = q_blocks * WINDOW
    k_blk = nq // LANES
    v_blk = k_blk + N_KV_HEADS * HEAD // LANES
    q_spec = pl.BlockSpec((1, tq, nq), lambda bi, n: (bi, n, 0))
    cur = lambda blk: pl.BlockSpec((1, tq, LANES), lambda bi, n: (bi, n, blk))
    prev = lambda blk: pl.BlockSpec((1, WINDOW, LANES),
                                    lambda bi, n: (bi, jnp.maximum(n * q_blocks - 1, 0), blk))
    return pl.pallas_call(
        functools.partial(_attn_core_kernel, q_blocks),
        out_shape=jax.ShapeDtypeStruct((b, s, nq), BF16),
        grid=(b, s // tq),
        in_specs=[q_spec, prev(k_blk), cur(k_blk), prev(v_blk), cur(v_blk),
                  _const_spec(sinks.shape)],
        out_specs=q_spec,
        compiler_params=pltpu.CompilerParams(
            dimension_semantics=("parallel", "arbitrary"), vmem_limit_bytes=VMEM_LIMIT),
        name="attn_core",
    )(qkv, qkv, qkv, qkv, qkv, sinks)


def kernel(x, positions, norm_mix_g, norm_mlp_g, norm_final_g, rwkv_mu, rwkv_w_r, rwkv_w_k, rwkv_w_v, rwkv_w_o, rwkv_w0, rwkv_w1, rwkv_w2, rwkv_a0, rwkv_a1, rwkv_a2, rwkv_g1, rwkv_g2, rwkv_k_k, rwkv_k_a, rwkv_r_k, rwkv_ln_w, rwkv_ln_b, attn_w_qkv, attn_b_qkv, attn_sinks, attn_w_o, attn_b_o, mlp_w_in, mlp_w_out):
    b, s, d = x.shape
    t = b * s
    x2d = x.reshape(t, d)
    row = lambda p: p.reshape(1, -1)
    bf = lambda w: w.astype(BF16)
    zero_bias = jnp.zeros((1, d), F32)

    r, k, v, lw, a, gate = _rwkv_pre(
        x2d, s, row(norm_mix_g[0]), rwkv_mu[0], bf(rwkv_w_r[0]), bf(rwkv_w_k[0]),
        bf(rwkv_w_v[0]), row(rwkv_w0[0]), bf(rwkv_w1[0]), bf(rwkv_w2[0]), row(rwkv_a0[0]),
        bf(rwkv_a1[0]), bf(rwkv_a2[0]), bf(rwkv_g1[0]), bf(rwkv_g2[0]))
    seq = lambda z: z.reshape(b, s, d)
    later_weights = [rwkv_w_o[0], attn_w_qkv[0], attn_w_o[0],
                     mlp_w_in.reshape(-1, D_FF), mlp_w_out.reshape(-1, d)]
    y, (w_o0, w_qkv, w_o1, w_in_all, w_out_all) = _wkv(
        seq(r), seq(k), seq(v), seq(lw), seq(a), row(rwkv_k_k[0]), row(rwkv_k_a[0]),
        row(rwkv_r_k[0]), row(rwkv_ln_w[0]), row(rwkv_ln_b[0]), later_weights)
    inv_freq = ROPE_THETA ** (-jnp.arange(0, HEAD, 2, dtype=F32) / HEAD)
    freq_row = jnp.tile(inv_freq, LANES // (HEAD // 2)).reshape(1, LANES)
    x2d, cos, sin = _proj_mlp(y.reshape(t, d), gate, x2d, w_o0, zero_bias,
                              row(norm_mlp_g[0]), w_in_all, w_out_all, 0, None,
                              rope=(_pack_positions(positions.reshape(t)), freq_row))

    qkv = _attn_qkv(x2d, row(norm_mix_g[1]), w_qkv, row(attn_b_qkv[0]), cos, sin)
    o = _attn_core(qkv.reshape(b, s, QKV_DIM), row(attn_sinks[0]))
    out = _proj_mlp(o.reshape(t, d), None, x2d, w_o1, row(attn_b_o[0]),
                    row(norm_mlp_g[1]), w_in_all, w_out_all, 1, row(norm_final_g))
    return out.reshape(b, s, d)
```

```python
import functools
import math

import jax
import jax.numpy as jnp
from jax import lax
from jax.experimental import pallas as pl
from jax.experimental.pallas import tpu as pltpu

F32 = jnp.float32
BF16 = jnp.bfloat16

D_MODEL = 1024
HEAD = 64
N_HEADS = D_MODEL // HEAD
N_KV_HEADS = 2
GROUP = N_HEADS // N_KV_HEADS
WINDOW = 128
QKV_DIM = (N_HEADS + 2 * N_KV_HEADS) * HEAD
D_FF = 4 * D_MODEL
ROPE_THETA = 10000.0
RMS_EPS = 1e-5
GN_EPS = 64e-5

LANES = 128
ROPE_PACK = LANES // (HEAD // 2)
ROPE_SUB = 256
LOG2E = math.log2(math.e)
Q_SCALE = HEAD ** -0.5 * LOG2E
CHUNK = 64
VMEM_LIMIT = 56 * 1024 * 1024


def _rms(x, g):
    return x * lax.rsqrt(jnp.mean(x * x, axis=-1, keepdims=True) + RMS_EPS) * g


def _dot(a, b):
    return jnp.dot(a.astype(BF16), b.astype(BF16), preferred_element_type=F32)


def _dot_nt(a, b):
    return lax.dot_general(a.astype(BF16), b.astype(BF16), (((1,), (1,)), ((), ())),
                           preferred_element_type=F32)


def _dot_tn(a, b):
    return lax.dot_general(a.astype(BF16), b.astype(BF16), (((0,), (0,)), ((), ())),
                           preferred_element_type=F32)


def _const_spec(shape):
    nd = len(shape)
    return pl.BlockSpec(shape, lambda *_: (0,) * nd)


def _drain(gen):
    for _ in gen:
        pass


def _interleave(main, fillers):
    fillers = [f if isinstance(f, tuple) else (f, 0, 1) for f in fillers]
    for n, _ in enumerate(main):
        for f, start, stride in fillers:
            if n >= start and (n - start) % stride == 0:
                next(f, None)
    for f, _, _ in fillers:
        _drain(f)


def _rwkv_pre_kernel(seq_tiles, sub, x_ref, xp_ref, g_ref, mu_ref, wr_ref, wk_ref, wv_ref,
                     w0_ref, w1_ref, w2_ref, a0_ref, a1_ref, a2_ref, g1_ref, g2_ref,
                     r_out, k_out, v_out, lw_out, a_out, g_out):
    i = pl.program_id(0)
    n_sub = x_ref.shape[0] // sub
    g = g_ref[...]
    mu = mu_ref[...]
    rows_of = lambda j: slice(j * sub, (j + 1) * sub)
    hp = _rms(xp_ref[...], g)
    carry_row = {0: jnp.where(i % seq_tiles == 0, 0.0, hp[7:8, :])}

    def mix(j, out):
        h = _rms(x_ref[rows_of(j), :], g)
        carry_row[j + 1] = h[sub - 1:sub, :]
        yield
        row = lax.broadcasted_iota(jnp.int32, h.shape, 0)
        dx = jnp.where(row == 0, carry_row[j], pltpu.roll(h, 1, axis=0)) - h
        yield
        for n, name in enumerate(("r", "w", "k", "v", "a", "g")):
            out[name] = (h + dx * mu[n:n + 1]).astype(BF16)
            if n % 2 == 1:
                yield

    def project(j, xs):
        rows = rows_of(j)
        dw = _dot(xs["w"], w1_ref[...])
        da = _dot(xs["a"], a1_ref[...])
        dg = _dot(xs["g"], g1_ref[...])
        yield
        r_out[rows, :] = jnp.dot(xs["r"], wr_ref[...], preferred_element_type=F32)
        yield
        w_pre = w0_ref[...] + _dot(jnp.tanh(dw), w2_ref[...])
        a_pre = a0_ref[...] + _dot(da, a2_ref[...])
        g_out[rows, :] = _dot(jax.nn.sigmoid(dg), g2_ref[...])
        yield
        k_out[rows, :] = jnp.dot(xs["k"], wk_ref[...], preferred_element_type=F32)
        yield
        lw_out[rows, :] = jax.nn.sigmoid(w_pre) * (-math.exp(-0.5))
        a_out[rows, :] = jax.nn.sigmoid(a_pre)
        yield
        v_out[rows, :] = jnp.dot(xs["v"], wv_ref[...], preferred_element_type=F32)
        yield

    xs = [dict() for _ in range(n_sub)]
    _drain(mix(0, xs[0]))
    for j in range(n_sub):
        fillers = [mix(j + 1, xs[j + 1])] if j + 1 < n_sub else []
        _interleave(project(j, xs[j]), fillers)


def _rwkv_pre(x2d, seq_len, g, mu, wr, wk, wv, w0, w1, w2, a0, a1, a2, g1, g2, tm=512, sub=256):
    t, d = x2d.shape
    seq_tiles = seq_len // tm
    row_spec = pl.BlockSpec((tm, d), lambda i: (i, 0))
    prev_spec = pl.BlockSpec((8, d), lambda i: (jnp.maximum(i * (tm // 8) - 1, 0), 0))
    consts = (g, mu, wr, wk, wv, w0, w1, w2, a0, a1, a2, g1, g2)
    out = jax.ShapeDtypeStruct((t, d), F32)
    return pl.pallas_call(
        functools.partial(_rwkv_pre_kernel, seq_tiles, sub),
        out_shape=(out,) * 6,
        grid=(t // tm,),
        in_specs=[row_spec, prev_spec] + [_const_spec(c.shape) for c in consts],
        out_specs=(row_spec,) * 6,
        compiler_params=pltpu.CompilerParams(
            dimension_semantics=("parallel",), vmem_limit_bytes=VMEM_LIMIT),
        name="rwkv_pre",
    )(x2d, x2d, *consts)


def _cumsum_rows(x):
    n = x.shape[0]
    row = lax.broadcasted_iota(jnp.int32, x.shape, 0)
    s = 1
    while s < n:
        x = x + jnp.where(row >= s, pltpu.roll(x, s, axis=0), 0.0)
        s *= 2
    return x


INV_BASE = 8
GROUP_HEADS = 4
GROUP_W = GROUP_HEADS * HEAD


def _head_sum(x):
    low = lax.broadcasted_iota(jnp.int32, (x.shape[0], LANES), 1) < HEAD
    outs = []
    for c in range(x.shape[1] // LANES):
        xc = x[:, c * LANES:(c + 1) * LANES]
        s_lo = jnp.sum(jnp.where(low, xc, 0.0), axis=-1, keepdims=True)
        s_hi = jnp.sum(jnp.where(low, 0.0, xc), axis=-1, keepdims=True)
        outs.append(jnp.where(low, s_lo, s_hi))
    return jnp.concatenate(outs, axis=1)


def _wkv_kernel(chunks, n_w, r_ref, k_ref, v_ref, lw_ref, a_ref,
                kk_ref, ka_ref, rk_ref, lnw_ref, lnb_ref, *rest):
    o_ref, state_ref = rest[n_w], rest[-1]
    w_refs = rest[:n_w] + rest[n_w + 1:-1]
    n_batch = r_ref.shape[0]
    n_groups = D_MODEL // GROUP_W
    probs = [(b, g) for b in range(n_batch) for g in range(n_groups)]
    ps = range(len(probs))

    @pl.when(pl.program_id(0) == 0)
    def _():
        state_ref[...] = jnp.zeros_like(state_ref)

    ri = lax.broadcasted_iota(jnp.int32, (CHUNK, GROUP_W), 0)
    lane = lax.broadcasted_iota(jnp.int32, (CHUNK, GROUP_W), 1)
    cj = lane & (HEAD - 1)
    lane_head = lax.shift_right_logical(lane, HEAD.bit_length() - 1)
    head_masks = [lane_head == h for h in range(GROUP_HEADS)]
    blk = lambda idx, size: lax.shift_right_logical(idx, size.bit_length() - 1)
    strict = cj < ri
    incl = cj <= ri
    eye = (cj == ri).astype(F32)
    diag = blk(ri, INV_BASE) == blk(cj, INV_BASE)
    bands = []
    size = INV_BASE
    while size < CHUNK:
        bands.append((blk(ri, 2 * size) == blk(cj, 2 * size))
                     & (blk(ri, size) != blk(cj, size)))
        size *= 2
    sr = lax.broadcasted_iota(jnp.int32, (GROUP_W, GROUP_W), 0)
    sc = lax.broadcasted_iota(jnp.int32, (GROUP_W, GROUP_W), 1)
    state_mask = blk(sr, HEAD) == blk(sc, HEAD)

    def bd(y):
        return jnp.concatenate([jnp.where(m, y, 0.0) for m in head_masks], axis=0).astype(BF16)

    def mm(x, w):
        return jnp.dot(x.astype(BF16), w, preferred_element_type=F32)

    def mm_nt(x, w):
        return lax.dot_general(x.astype(BF16), w, (((1,), (1,)), ((), ())),
                               preferred_element_type=F32)

    cols = [slice(g * GROUP_W, (g + 1) * GROUP_W) for g in range(n_groups)]
    rows_of = lambda c: slice(c * CHUNK, (c + 1) * CHUNK)


    def exact_zero(x):
        bits = pltpu.bitcast(x[:8, :LANES], jnp.uint32)
        half = jnp.uint32(16)
        zero = lax.shift_right_logical(lax.shift_right_logical(bits, half), half)
        return pltpu.bitcast(zero, F32)[:1, :1]

    def prepare(c, b, out, after=None):
        rows = rows_of(c)
        half_w = D_MODEL // 2
        lhs, bts, kts, kmods, pends = [], [], [], [], []
        for hs in (slice(0, half_w), slice(half_w, D_MODEL)):
            lw = lw_ref[b, rows, hs]
            if after is not None:
                lw = lw + exact_zero(after())
            linc = _cumsum_rows(lw)
            yield
            p_inc = jnp.exp(linc)
            p_exc = jnp.exp(linc - lw)
            p_inv = jnp.exp(-linc)
            yield
            k = k_ref[b, rows, hs]
            a = a_ref[b, rows, hs]
            kk = k * kk_ref[:, hs]
            kk = kk * lax.rsqrt(jnp.maximum(_head_sum(kk * kk), 1e-24))
            yield
            kmod = k * (1.0 + (a - 1.0) * ka_ref[:, hs])
            at_f = (-kk * p_exc).astype(BF16)
            rt_f = (r_ref[b, rows, hs] * p_inc).astype(BF16)
            yield
            lhs += [jnp.concatenate([at_f[:, cl], rt_f[:, cl]], axis=0)
                    for cl in cols[:n_groups // 2]]
            bts.append(kk * a * p_inv)
            kts.append(kmod * p_inv)
            kmods.append(kmod)
            pends.append(p_inc[CHUNK - 1:CHUNK, :])
        out["lhs"] = lhs
        out["bt"] = jnp.concatenate(bts, axis=1)
        out["kt"] = jnp.concatenate(kts, axis=1)
        out["kmod"] = jnp.concatenate(kmods, axis=1)
        out["p_end"] = jnp.concatenate(pends, axis=1)
        yield

    def matmuls(c, ops, out):
        v = [v_ref[b, rows_of(c), :] for b in range(n_batch)]
        lhs = [ops[b]["lhs"][g] for b, g in probs]
        bt = [ops[b]["bt"][:, cols[g]] for b, g in probs]
        kt = [ops[b]["kt"][:, cols[g]] for b, g in probs]
        vg = [v[b][:, cols[g]] for b, g in probs]
        ab = [mm_nt(lhs[p], bd(bt[p])) for p in ps]
        ak = [mm_nt(lhs[p], bd(kt[p])) for p in ps]
        yield
        a_ab = [jnp.where(strict, x[:CHUNK], 0.0) for x in ab]
        a_rb = [jnp.where(incl, x[CHUNK:], 0.0) for x in ab]
        a_k = [jnp.concatenate([jnp.where(strict, x[:CHUNK], 0.0),
                                jnp.where(incl, x[CHUNK:], 0.0)], axis=0) for x in ak]
        d1 = [jnp.where(diag, x, 0.0) for x in a_ab]
        d2 = [mm(d1[p], bd(d1[p])) for p in ps]
        yield
        t = [eye + x for x in d1]
        td = [mm(jnp.concatenate([t[p], d2[p]], axis=0), bd(d2[p])) for p in ps]
        yield
        t = [t[p] + td[p][:CHUNK] for p in ps]
        out["anchors"] = [td[0]]
        t = [t[p] + mm(t[p], bd(td[p][CHUNK:])) for p in ps]
        yield
        for band in bands:
            te = [mm(t[p], bd(jnp.where(band, a_ab[p], 0.0))) for p in ps]
            out["anchors"].append(te[0])
            yield
            t = [t[p] + mm(te[p], bd(t[p])) for p in ps]
            yield
        s0 = [state_ref[p] for p in ps]
        zy = [mm_nt(lhs[p], s0[p].astype(BF16)) for p in ps]
        av = [mm(a_k[p], bd(vg[p])) for p in ps]
        yield
        u = [mm(t[p], bd(zy[p][:CHUNK] + av[p][:CHUNK])) for p in ps]
        yield
        y = [zy[p][CHUNK:] + av[p][CHUNK:] + mm(a_rb[p], bd(u[p])) for p in ps]
        for p, (b, g) in enumerate(probs):
            uv = jnp.concatenate([u[p], vg[p]], axis=0)
            bk = jnp.concatenate([bt[p], kt[p]], axis=0)
            state_ref[p] = (jnp.where(state_mask, s0[p] + _dot_tn(uv, bk), 0.0)
                            * ops[b]["p_end"][:, cols[g]])
        out["y"] = [jnp.concatenate(y[b * n_groups:(b + 1) * n_groups], axis=1)
                    for b in range(n_batch)]
        yield

    def finish(c, b, ops, y_f, after=None):
        rows = rows_of(c)
        inv_n = 1.0 / HEAD
        if after is not None:
            y_f = y_f + exact_zero(after())
        yc = y_f - _head_sum(y_f) * inv_n
        yield
        var = _head_sum(yc * yc) * inv_n
        yn = yc * lax.rsqrt(var + GN_EPS) * lnw_ref[...] + lnb_ref[...]
        yield
        bonus = _head_sum(r_ref[b, rows, :] * ops["kmod"] * rk_ref[...])
        o_ref[b, rows, :] = yn + bonus * v_ref[b, rows, :]
        yield

    def cast_weights():
        for src, dst in zip(w_refs[:len(w_refs) // 2], w_refs[len(w_refs) // 2:]):
            dst[...] = src[...].astype(dst.dtype)
            yield

    bs = range(n_batch)
    ops = [[dict() for _ in bs] for _ in range(chunks)]
    res = [dict() for _ in range(chunks)]
    for b in bs:
        _drain(prepare(0, b, ops[0][b]))
    for c in range(chunks):
        fillers = [cast_weights()] if c == 0 else []
        tie = lambda n, c=c: (lambda: res[c]["anchors"][n])
        if c + 1 < chunks:
            fillers += [(prepare(c + 1, b, ops[c + 1][b], tie(b)), 3 + b, 1) for b in bs]
        if c > 0:
            fillers += [(finish(c - 1, b, ops[c - 1][b], res[c - 1]["y"][b], tie(2 + b)),
                         6 + 2 * b, 1) for b in bs]
        _interleave(matmuls(c, ops[c], res[c]), fillers)
    for b in bs:
        _drain(finish(chunks - 1, b, ops[chunks - 1][b], res[chunks - 1]["y"][b]))


def _wkv(r, k, v, lw, a, kk_p, ka_p, rk_p, lnw_p, lnb_p, weights, rows=256):
    b, s, d = r.shape
    steps = s // rows
    seq_spec = pl.BlockSpec((b, rows, d), lambda ci: (0, ci, 0))
    par_spec = pl.BlockSpec((1, d), lambda ci: (0, 0))
    w_specs = [pl.BlockSpec((w.shape[0] // steps, w.shape[1]), lambda ci: (ci, 0))
               for w in weights]
    w_out = [jax.ShapeDtypeStruct(w.shape, BF16) for w in weights]
    outs = pl.pallas_call(
        functools.partial(_wkv_kernel, rows // CHUNK, len(weights)),
        out_shape=[jax.ShapeDtypeStruct((b, s, d), F32)] + w_out,
        grid=(steps,),
        in_specs=[seq_spec] * 5 + [par_spec] * 5 + w_specs,
        out_specs=[seq_spec] + w_specs,
        scratch_shapes=[pltpu.VMEM((b * d // GROUP_W, GROUP_W, GROUP_W), F32)],
        compiler_params=pltpu.CompilerParams(
            dimension_semantics=("arbitrary",), vmem_limit_bytes=VMEM_LIMIT),
        name="wkv",
    )(r, k, v, lw, a, kk_p, ka_p, rk_p, lnw_p, lnb_p, *weights)
    return outs[0], outs[1:]


def _proj_mlp_kernel(has_gate, final_norm, with_rope, ff_chunk, sub, *refs):
    refs = list(refs)
    y_ref = refs.pop(0)
    gate_ref = refs.pop(0) if has_gate else None
    x_ref, wo_ref, bo_ref, gm_ref, win_ref, wout_ref = refs[:6]
    refs = refs[6:]
    gf_ref = refs.pop(0) if final_norm else None
    if with_rope:
        pos_ref, freq_ref = refs.pop(0), refs.pop(0)
    o_ref = refs.pop(0)
    if with_rope:
        cos_ref, sin_ref = refs.pop(0), refs.pop(0)

    def rope_tables(part, parts, pieces=2):
        rows = pos_ref.shape[0] // (parts * pieces)
        for n in range(part * pieces, (part + 1) * pieces):
            sl = slice(n * rows, (n + 1) * rows)
            ang = pos_ref[sl, :] * freq_ref[...]
            cos_ref[sl, :] = jnp.cos(ang)
            sin_ref[sl, :] = jnp.sin(ang)
            yield

    n_sub = x_ref.shape[0] // sub
    rows_of = lambda j: slice(j * sub, (j + 1) * sub)

    def head(j, out):
        rows = rows_of(j)
        y = y_ref[rows, :].astype(F32)
        if has_gate:
            y = y * gate_ref[rows, :]
        yield
        x1 = x_ref[rows, :] + _dot(y, wo_ref[...]) + bo_ref[...]
        yield
        out["x1"] = x1
        out["hb"] = _rms(x1, gm_ref[...]).astype(BF16)
        yield

    def mlp(j, ins, out):
        acc = ins["x1"]
        for c in range(D_FF // ff_chunk):
            cols = slice(c * ff_chunk, (c + 1) * ff_chunk)
            hid = jnp.maximum(jnp.dot(ins["hb"], win_ref[:, cols], preferred_element_type=F32), 0.0)
            yield
            acc = acc + _dot(hid * hid, wout_ref[cols, :])
            yield
        out["acc"] = acc

    def tail(j, acc):
        if final_norm:
            acc = _rms(acc, gf_ref[...])
        o_ref[rows_of(j), :] = acc
        yield

    ins = [dict() for _ in range(n_sub)]
    outs = [dict() for _ in range(n_sub)]
    _drain(head(0, ins[0]))
    for j in range(n_sub):
        fillers = []
        if j + 1 < n_sub:
            fillers.append(head(j + 1, ins[j + 1]))
        if j > 0:
            fillers.append(tail(j - 1, outs[j - 1]["acc"]))
        if with_rope:
            fillers.append(rope_tables(j, n_sub))
        _interleave(mlp(j, ins[j], outs[j]), fillers)
    _drain(tail(n_sub - 1, outs[n_sub - 1]["acc"]))


def _proj_mlp(y, gate, x, wo, bo, gm, win_all, wout_all, layer, gf, rope=None, tm=1024,
              sub=256, ff_chunk=1024):
    t, d = x.shape
    row_spec = pl.BlockSpec((tm, d), lambda i: (i, 0))
    args = [y] + ([gate] if gate is not None else []) + [x, wo, bo, gm, win_all, wout_all]
    resident = lambda shape, blk: pl.BlockSpec(shape, lambda i: (blk, 0),
                                               pipeline_mode=pl.Buffered(1))
    specs = [row_spec] * (len(args) - 5) + [
        resident(wo.shape, 0), resident(bo.shape, 0), resident(gm.shape, 0),
        resident((d, D_FF), layer), resident((D_FF, d), layer)]
    if gf is not None:
        args.append(gf)
        specs.append(_const_spec(gf.shape))
    out_shape = [jax.ShapeDtypeStruct((t, d), F32)]
    out_specs = [row_spec]
    if rope is not None:
        pos_packed, freq_row = rope
        packed_spec = pl.BlockSpec((tm // ROPE_PACK, LANES), lambda i: (i, 0))
        args += [pos_packed, freq_row]
        specs += [packed_spec, _const_spec(freq_row.shape)]
        out_shape += [jax.ShapeDtypeStruct(pos_packed.shape, F32)] * 2
        out_specs += [packed_spec] * 2
    outs = pl.pallas_call(
        functools.partial(_proj_mlp_kernel, gate is not None, gf is not None,
                          rope is not None, ff_chunk, sub),
        out_shape=out_shape,
        grid=(t // tm,),
        in_specs=specs,
        out_specs=out_specs,
        compiler_params=pltpu.CompilerParams(
            dimension_semantics=("parallel",), vmem_limit_bytes=VMEM_LIMIT),
        name="proj_mlp",
    )(*args)
    return outs[0] if rope is None else outs


def _pack_positions(pos):
    per = ROPE_SUB // ROPE_PACK
    grouped = pos.astype(F32).reshape(-1, ROPE_PACK, per).transpose(0, 2, 1)
    return jnp.repeat(grouped, LANES // ROPE_PACK, axis=-1).reshape(-1, LANES)


def _attn_qkv_kernel(sub, x_ref, g_ref, w_ref, b_ref, cos_ref, sin_ref, o_ref):
    n_sub = x_ref.shape[0] // sub
    rows_of = lambda j: slice(j * sub, (j + 1) * sub)
    lane = lax.broadcasted_iota(jnp.int32, (sub, LANES), 1)
    first_half = lane % HEAD < HEAD // 2
    n_q = N_HEADS * HEAD // LANES
    n_rot = (N_HEADS + N_KV_HEADS) * HEAD // LANES
    n_blk = 2 * LANES

    def norm(j, out):
        out["h"] = _rms(x_ref[rows_of(j), :], g_ref[...]).astype(BF16)
        yield

    def project(j, ins, out):
        out["qkv"] = []
        for c in range(QKV_DIM // n_blk):
            cols = slice(c * n_blk, (c + 1) * n_blk)
            out["qkv"].append(jnp.dot(ins["h"], w_ref[:, cols], preferred_element_type=F32)
                              + b_ref[:, cols])
            yield

    def unpack(packed):
        quarter = lax.shift_right_logical(
            lax.broadcasted_iota(jnp.int32, packed.shape, 1), (LANES // ROPE_PACK).bit_length() - 1)
        rolled = [packed] + [pltpu.roll(packed, n * (LANES // ROPE_PACK), axis=1)
                             for n in range(1, ROPE_PACK)]
        blocks = []
        for n in range(ROPE_PACK):
            blk = rolled[(ROPE_PACK - 1 - n) % ROPE_PACK]
            for qq in range(ROPE_PACK - 2, -1, -1):
                blk = jnp.where(quarter == qq, rolled[(qq - n) % ROPE_PACK], blk)
            blocks.append(blk)
        return jnp.concatenate(blocks, axis=0)

    def rotary(j, qkv):
        rows = rows_of(j)
        packed_rows = slice(j * sub // ROPE_PACK, (j + 1) * sub // ROPE_PACK)
        cos = unpack(cos_ref[packed_rows, :])
        sin = jnp.where(first_half, -1.0, 1.0) * unpack(sin_ref[packed_rows, :])
        yield
        for s in range(QKV_DIM // LANES):
            c, half = divmod(s, n_blk // LANES)
            blk = qkv[c][:, half * LANES:(half + 1) * LANES]
            if s < n_rot:
                rot = jnp.where(first_half, pltpu.roll(blk, LANES - HEAD // 2, axis=1),
                                pltpu.roll(blk, HEAD // 2, axis=1))
                blk = blk * cos + rot * sin
            if s < n_q:
                blk = blk * Q_SCALE
            o_ref[rows, s * LANES:(s + 1) * LANES] = blk.astype(o_ref.dtype)
            if s % 2 == 1:
                yield

    ins = [dict() for _ in range(n_sub)]
    outs = [dict() for _ in range(n_sub)]
    _drain(norm(0, ins[0]))
    for j in range(n_sub):
        fillers = []
        if j + 1 < n_sub:
            fillers.append(norm(j + 1, ins[j + 1]))
        if j > 0:
            fillers.append(rotary(j - 1, outs[j - 1]["qkv"]))
        _interleave(project(j, ins[j], outs[j]), fillers)
    _drain(rotary(n_sub - 1, outs[n_sub - 1]["qkv"]))


def _attn_qkv(x, g, w, b, cos, sin, tm=1024, sub=ROPE_SUB):
    t, d = x.shape
    row = lambda width: pl.BlockSpec((tm, width), lambda i: (i, 0))
    packed = pl.BlockSpec((tm // ROPE_PACK, LANES), lambda i: (i, 0))
    return pl.pallas_call(
        functools.partial(_attn_qkv_kernel, sub),
        out_shape=jax.ShapeDtypeStruct((t, QKV_DIM), BF16),
        grid=(t // tm,),
        in_specs=[row(d), _const_spec(g.shape), _const_spec(w.shape), _const_spec(b.shape),
                  packed, packed],
        out_specs=row(QKV_DIM),
        compiler_params=pltpu.CompilerParams(
            dimension_semantics=("parallel",), vmem_limit_bytes=VMEM_LIMIT),
        name="attn_qkv",
    )(x, g, w, b, cos, sin)


def _attn_core_kernel(q_blocks, q_ref, kp_ref, kc_ref, vp_ref, vc_ref, sink_ref, o_ref):
    n = pl.program_id(1)
    kr = lax.broadcasted_iota(jnp.int32, (2 * WINDOW, WINDOW), 0)
    qc = lax.broadcasted_iota(jnp.int32, (2 * WINDOW, WINDOW), 1)
    own = kr >= WINDOW
    band = (own & (kr - WINDOW <= qc)) | (jnp.logical_not(own) & (kr > qc))
    first_band = band & (own | (n > 0))
    neg = -jnp.inf
    probs = [(i, j) for i in range(q_blocks) for j in range(N_KV_HEADS)]
    blk = lambda i: slice(i * WINDOW, (i + 1) * WINDOW)
    kvl = lambda j: slice(j * HEAD, (j + 1) * HEAD)
    head = lambda j, g: j * GROUP + g

    def keys(prev_ref, cur_ref, i, j):
        prev = prev_ref[0, :, kvl(j)] if i == 0 else cur_ref[0, blk(i - 1), kvl(j)]
        return jnp.concatenate([prev, cur_ref[0, blk(i), kvl(j)]], axis=0)

    q = [jnp.concatenate([q_ref[0, blk(i), head(j, g) * HEAD:(head(j, g) + 1) * HEAD]
                          for g in range(GROUP)], axis=0) for i, j in probs]
    s_t = [_dot_nt(keys(kp_ref, kc_ref, i, j), q[p]) for p, (i, j) in enumerate(probs)]
    p_t = [[] for _ in probs]
    for g in range(GROUP):
        sink = [sink_ref[:, head(j, g):head(j, g) + 1] * LOG2E for _, j in probs]
        x = [jnp.where(first_band if i == 0 else band, s_t[p][:, blk(g)], neg)
             for p, (i, j) in enumerate(probs)]
        m = [jnp.maximum(jnp.max(x[p], axis=0, keepdims=True), sink[p])
             for p in range(len(probs))]
        e = [jnp.exp2(x[p] - m[p]) for p in range(len(probs))]
        den = [jnp.sum(e[p], axis=0, keepdims=True) + jnp.exp2(sink[p] - m[p])
               for p in range(len(probs))]
        for p in range(len(probs)):
            p_t[p].append((e[p] * (1.0 / den[p])).astype(BF16))
    for p, (i, j) in enumerate(probs):
        o = lax.dot_general(jnp.concatenate(p_t[p], axis=1), keys(vp_ref, vc_ref, i, j),
                            (((0,), (0,)), ((), ())), preferred_element_type=F32)
        for g in range(GROUP):
            o_ref[0, blk(i), head(j, g) * HEAD:(head(j, g) + 1) * HEAD] = (
                o[blk(g)].astype(o_ref.dtype))


def _attn_core(qkv, sinks, q_blocks=8):
    b, s, _ = qkv.shape
    nq = N_HEADS * HEAD
    tq = q_blocks * WINDOW
    k_blk = nq // LANES
    v_blk = k_blk + N_KV_HEADS * HEAD // LANES
    q_spec = pl.BlockSpec((1, tq, nq), lambda bi, n: (bi, n, 0))
    cur = lambda blk: pl.BlockSpec((1, tq, LANES), lambda bi, n: (bi, n, blk))
    prev = lambda blk: pl.BlockSpec((1, WINDOW, LANES),
                                    lambda bi, n: (bi, jnp.maximum(n * q_blocks - 1, 0), blk))
    return pl.pallas_call(
        functools.partial(_attn_core_kernel, q_blocks),
        out_shape=jax.ShapeDtypeStruct((b, s, nq), BF16),
        grid=(b, s // tq),
        in_specs=[q_spec, prev(k_blk), cur(k_blk), prev(v_blk), cur(v_blk),
                  _const_spec(sinks.shape)],
        out_specs=q_spec,
        compiler_params=pltpu.CompilerParams(
            dimension_semantics=("parallel", "arbitrary"), vmem_limit_bytes=VMEM_LIMIT),
        name="attn_core",
    )(qkv, qkv, qkv, qkv, qkv, sinks)


def kernel(x, positions, norm_mix_g, norm_mlp_g, norm_final_g, rwkv_mu, rwkv_w_r, rwkv_w_k, rwkv_w_v, rwkv_w_o, rwkv_w0, rwkv_w1, rwkv_w2, rwkv_a0, rwkv_a1, rwkv_a2, rwkv_g1, rwkv_g2, rwkv_k_k, rwkv_k_a, rwkv_r_k, rwkv_ln_w, rwkv_ln_b, attn_w_qkv, attn_b_qkv, attn_sinks, attn_w_o, attn_b_o, mlp_w_in, mlp_w_out):
    b, s, d = x.shape
    t = b * s
    x2d = x.reshape(t, d)
    row = lambda p: p.reshape(1, -1)
    bf = lambda w: w.astype(BF16)
    zero_bias = jnp.zeros((1, d), F32)

    r, k, v, lw, a, gate = _rwkv_pre(
        x2d, s, row(norm_mix_g[0]), rwkv_mu[0], bf(rwkv_w_r[0]), bf(rwkv_w_k[0]),
        bf(rwkv_w_v[0]), row(rwkv_w0[0]), bf(rwkv_w1[0]), bf(rwkv_w2[0]), row(rwkv_a0[0]),
        bf(rwkv_a1[0]), bf(rwkv_a2[0]), bf(rwkv_g1[0]), bf(rwkv_g2[0]))
    seq = lambda z: z.reshape(b, s, d)
    later_weights = [rwkv_w_o[0], attn_w_qkv[0], attn_w_o[0],
                     mlp_w_in.reshape(-1, D_FF), mlp_w_out.reshape(-1, d)]
    y, (w_o0, w_qkv, w_o1, w_in_all, w_out_all) = _wkv(
        seq(r), seq(k), seq(v), seq(lw), seq(a), row(rwkv_k_k[0]), row(rwkv_k_a[0]),
        row(rwkv_r_k[0]), row(rwkv_ln_w[0]), row(rwkv_ln_b[0]), later_weights)
    inv_freq = ROPE_THETA ** (-jnp.arange(0, HEAD, 2, dtype=F32) / HEAD)
    freq_row = jnp.tile(inv_freq, LANES // (HEAD // 2)).reshape(1, LANES)
    x2d, cos, sin = _proj_mlp(y.reshape(t, d), gate, x2d, w_o0, zero_bias,
                              row(norm_mlp_g[0]), w_in_all, w_out_all, 0, None,
                              rope=(_pack_positions(positions.reshape(t)), freq_row))

    qkv = _attn_qkv(x2d, row(norm_mix_g[1]), w_qkv, row(attn_b_qkv[0]), cos, sin)
    o = _attn_core(qkv.reshape(b, s, QKV_DIM), row(attn_sinks[0]))
    out = _proj_mlp(o.reshape(t, d), None, x2d, w_o1, row(attn_b_o[0]),
                    row(norm_mlp_g[1]), w_in_all, w_out_all, 1, row(norm_final_g))
    return out.reshape(b, s, d)
```
